```python
import jax, jax.numpy as jnp
from jax import lax
import numpy as np

D_MODEL = 1024
BATCH = 8
SEQ = 4096
DEPTH = 2

N_A = DEPTH // 2
N_B = DEPTH - N_A
M_HEADS = 4
M_QK = D_MODEL // (2 * M_HEADS)
M_V = D_MODEL // M_HEADS
M_CHUNK = 64
GATE_CAP = 15.0
M_IN = 2 * M_HEADS * M_QK + 2 * D_MODEL + 2 * M_HEADS
F_HEADS = 16
F_HD = D_MODEL // F_HEADS
Q_BLOCK = 128
KV_OUT = 2 * D_MODEL + F_HEADS
D_FF = ((-(-8 * D_MODEL // 3) + 255) // 256) * 256
EPS = 1e-6

kernel_name = "yoco_mlstm_fox_hybrid"


def rmsnorm(x, g):
    xf = x.astype(jnp.float32)
    y = xf * lax.rsqrt(jnp.mean(xf * xf, axis=-1, keepdims=True) + EPS)
    return (y * g.astype(jnp.float32)).astype(x.dtype)


def softcap(a, cap):
    return cap * jnp.tanh(a / cap)


def mlstm_chunkwise(q, k, v, ig, lf):
    B, H, S, dk = q.shape
    dv = v.shape[-1]
    L = M_CHUNK
    nc = S // L
    q = q * dk ** -0.5

    def to_chunks(a):
        return jnp.moveaxis(a.reshape(B, H, nc, L, *a.shape[3:]), 2, 0)

    qc, kc, vc, ic, fc = (to_chunks(a) for a in (q, k, v, ig, lf))
    causal = jnp.tril(jnp.ones((L, L), dtype=bool))

    def step(carry, inp):
        C, n, m = carry
        qb, kb, vb, ib, fb = inp
        b = jnp.cumsum(fb, axis=-1)
        dmat = jnp.where(causal, b[..., :, None] - b[..., None, :] + ib[..., None, :], -jnp.inf)
        inter = b + m[..., None]
        m_t = jnp.maximum(inter, jnp.max(dmat, axis=-1))
        w = jnp.exp(dmat - m_t[..., None])
        a = jnp.exp(inter - m_t)
        s = jnp.einsum('bhtd,bhsd->bhts', qb, kb) * w
        num = a[..., None] * jnp.einsum('bhtd,bhde->bhte', qb, C) + jnp.einsum('bhts,bhse->bhte', s, vb)
        den = a * jnp.einsum('bhtd,bhd->bht', qb, n) + jnp.sum(s, axis=-1)
        h = num / jnp.maximum(jnp.abs(den), jnp.exp(-m_t))[..., None]
        bl = b[..., -1]
        ws = bl[..., None] - b + ib
        m_new = jnp.maximum(bl + m, jnp.max(ws, axis=-1))
        a_c = jnp.exp(bl + m - m_new)
        ws = jnp.exp(ws - m_new[..., None])
        C = a_c[..., None, None] * C + jnp.einsum('bhs,bhsd,bhse->bhde', ws, kb, vb)
        n = a_c[..., None] * n + jnp.einsum('bhs,bhsd->bhd', ws, kb)
        return (C, n, m_new), h

    init = (jnp.zeros((B, H, dk, dv), jnp.float32), jnp.zeros((B, H, dk), jnp.float32),
            jnp.zeros((B, H), jnp.float32))
    _, h = lax.scan(step, init, (qc, kc, vc, ic, fc))
    return jnp.moveaxis(h, 0, 2).reshape(B, H, S, dv)


def mlstm_mixer(h, w_in, gate_b, hn_g, w_out):
    B, S, _ = h.shape
    dq = M_HEADS * M_QK
    proj = h @ w_in
    q, k, v, o, gi, gf = jnp.split(
        proj, [dq, 2 * dq, 2 * dq + D_MODEL, 2 * dq + 2 * D_MODEL, 2 * dq + 2 * D_MODEL + M_HEADS], axis=-1)
    heads = lambda a, d: a.reshape(B, S, M_HEADS, d).transpose(0, 2, 1, 3).astype(jnp.float32)
    q, k, v = heads(q, M_QK), heads(k, M_QK), heads(v, M_V)
    ig = softcap((gi + gate_b[:M_HEADS]).astype(jnp.float32), GATE_CAP).transpose(0, 2, 1)
    lf = jax.nn.log_sigmoid(softcap((gf + gate_b[M_HEADS:]).astype(jnp.float32), GATE_CAP)).transpose(0, 2, 1)
    hh = mlstm_chunkwise(q, k, v, ig, lf)
    hh = hh * lax.rsqrt(jnp.mean(hh * hh, axis=-1, keepdims=True) + EPS)
    hh = hh.transpose(0, 2, 1, 3).reshape(B, S, D_MODEL) * hn_g.astype(jnp.float32)
    y = jax.nn.sigmoid(o.astype(jnp.float32)) * hh
    return y.astype(h.dtype) @ w_out


def shared_kv(x, cs, g, mod_w, mod_b, w, f_b):
    B, S, _ = x.shape
    sh, sc = jnp.split((cs @ mod_w + mod_b)[:, None, :], 2, axis=-1)
    h = rmsnorm(x, g) * (1 + sc) + sh
    p = h @ w
    k, v, fg = jnp.split(p, [D_MODEL, 2 * D_MODEL], axis=-1)
    heads = lambda a: a.reshape(B, S, F_HEADS, F_HD).transpose(0, 2, 1, 3)
    logf = jax.nn.log_sigmoid((fg + f_b).astype(jnp.float32))
    F = jnp.cumsum(logf, axis=1).transpose(0, 2, 1)
    return heads(k), heads(v), F


def fox_attention(q, k, v, F):
    S = q.shape[2]
    scale = F_HD ** -0.5
    outs = []
    for i in range(S // Q_BLOCK):
        q0, q1 = i * Q_BLOCK, (i + 1) * Q_BLOCK
        qb, kb, vb = q[:, :, q0:q1], k[:, :, :q1], v[:, :, :q1]
        logits = (jnp.einsum('bhqd,bhkd->bhqk', qb, kb).astype(jnp.float32) * scale
                  + F[:, :, q0:q1, None] - F[:, :, None, :q1])
        mask = (q0 + jnp.arange(Q_BLOCK))[:, None] >= jnp.arange(q1)[None, :]
        p = jax.nn.softmax(jnp.where(mask, logits, -jnp.inf), axis=-1).astype(vb.dtype)
        outs.append(jnp.einsum('bhqk,bhkd->bhqd', p, vb))
    return jnp.concatenate(outs, axis=2)


def fox_mixer(h, k, v, F, w_q, w_out):
    B, S, _ = h.shape
    q = (h @ w_q).reshape(B, S, F_HEADS, F_HD).transpose(0, 2, 1, 3)
    o = fox_attention(q, k, v, F)
    return o.transpose(0, 2, 1, 3).reshape(B, S, D_MODEL) @ w_out


def swiglu(h, w_in, w_out):
    g, u = jnp.split(h @ w_in, 2, axis=-1)
    return (jax.nn.silu(g) * u) @ w_out


def setup_inputs(seed: int = 0) -> dict:
    key = jax.random.key(seed)
    ks = iter(jax.random.split(key, 32))
    D = D_MODEL
    nrm = lambda shape, s: jax.random.normal(next(ks), shape, jnp.float32) * s
    return {
        "x": nrm((BATCH, SEQ, D), 1.0),
        "c": nrm((BATCH, D), 1.0),
        "mod_w": nrm((DEPTH, D, 6 * D), D ** -0.5),
        "mod_b": nrm((DEPTH, 6 * D), 0.02),
        "mix_pre_g": 1.0 + nrm((DEPTH, D), 0.02),
        "mix_post_g": 1.0 + nrm((DEPTH, D), 0.02),
        "ffn_pre_g": 1.0 + nrm((DEPTH, D), 0.02),
        "ffn_post_g": 1.0 + nrm((DEPTH, D), 0.02),
        "ffn_w_in": nrm((DEPTH, D, 2 * D_FF), D ** -0.5),
        "ffn_w_out": nrm((DEPTH, D_FF, D), D_FF ** -0.5),
        "a_w_in": nrm((N_A, D, M_IN), D ** -0.5),
        "a_gate_b": nrm((N_A, 2 * M_HEADS), 0.02),
        "a_hnorm_g": 1.0 + nrm((N_A, D), 0.02),
        "a_w_out": nrm((N_A, D, D), D ** -0.5),
        "kv_norm_g": 1.0 + nrm((D,), 0.02),
        "kv_mod_w": nrm((D, 2 * D), D ** -0.5),
        "kv_mod_b": nrm((2 * D,), 0.02),
        "kv_w": nrm((D, KV_OUT), D ** -0.5),
        "kv_fgate_b": nrm((F_HEADS,), 0.02),
        "b_w_q": nrm((N_B, D, D), D ** -0.5),
        "b_w_out": nrm((N_B, D, D), D ** -0.5),
    }


def reference(x, c, mod_w, mod_b, mix_pre_g, mix_post_g, ffn_pre_g, ffn_post_g, ffn_w_in, ffn_w_out,
              a_w_in, a_gate_b, a_hnorm_g, a_w_out, kv_norm_g, kv_mod_w, kv_mod_b, kv_w, kv_fgate_b,
              b_w_q, b_w_out):
    cs = jax.nn.silu(c)
    k_sh = v_sh = F_sh = None
    for l in range(DEPTH):
        sh1, sc1, g1, sh2, sc2, g2 = jnp.split((cs @ mod_w[l] + mod_b[l])[:, None, :], 6, axis=-1)
        h = rmsnorm(x, mix_pre_g[l]) * (1 + sc1) + sh1
        if l < N_A:
            y = mlstm_mixer(h, a_w_in[l], a_gate_b[l], a_hnorm_g[l], a_w_out[l])
        else:
            if l == N_A:
                k_sh, v_sh, F_sh = shared_kv(x, cs, kv_norm_g, kv_mod_w, kv_mod_b, kv_w, kv_fgate_b)
            y = fox_mixer(h, k_sh, v_sh, F_sh, b_w_q[l - N_A], b_w_out[l - N_A])
        x = x + g1 * rmsnorm(y, mix_post_g[l])
        h = rmsnorm(x, ffn_pre_g[l]) * (1 + sc2) + sh2
        x = x + g2 * rmsnorm(swiglu(h, ffn_w_in[l], ffn_w_out[l]), ffn_post_g[l])
    return x
```

```python
import functools

import jax
import jax.numpy as jnp
from jax import lax
from jax.experimental import pallas as pl
from jax.experimental.pallas import tpu as pltpu

F32 = jnp.float32
BF16 = jnp.bfloat16

EPS = 1e-6
GATE_CAP = 15.0
LANES = 128
MLSTM_CHUNK = 256
ROW_TILE = 512
ATTN_TILE = 512
FFN_COL_CHUNK = 256
VMEM_LIMIT = 56 * 1024 * 1024


def _rms(x):
    return x * lax.rsqrt(jnp.mean(x * x, axis=-1, keepdims=True) + EPS)


def _log_sigmoid(x):
    return jnp.minimum(x, 0.0) - jnp.log(1.0 + jnp.exp(-jnp.abs(x)))


def _softcap(a):
    return GATE_CAP * jnp.tanh(a / GATE_CAP)


def _split3(x):
    hi = x.astype(BF16)
    r1 = x - hi.astype(F32)
    mid = r1.astype(BF16)
    lo = (r1 - mid.astype(F32)).astype(BF16)
    return hi, mid, lo


def _cumsum_rows(tri_lower, x):
    return sum(jnp.dot(tri_lower, p, preferred_element_type=F32) for p in _split3(x))


def _cumsum_lanes(x, tri_upper):
    return sum(jnp.dot(p, tri_upper, preferred_element_type=F32) for p in _split3(x))


def _tri(n):
    r = lax.broadcasted_iota(jnp.int32, (n, n), 0)
    c = lax.broadcasted_iota(jnp.int32, (n, n), 1)
    return r >= c, (r <= c)


def _resident(shape):
    nd = len(shape)
    return pl.BlockSpec(shape, lambda *_: (0,) * nd, pipeline_mode=pl.Buffered(1))


def _mod_kernel(c_ref, w_ref, b_ref, o_ref):
    c = c_ref[...]
    cs = (c * jax.nn.sigmoid(c)).astype(BF16)
    o_ref[0] = jnp.dot(cs, w_ref[0].astype(BF16), preferred_element_type=F32) + b_ref[0]


def _mod_call(c, w, b):
    G, D, N = w.shape
    B = c.shape[0]
    tn = 1024
    return pl.pallas_call(
        _mod_kernel,
        grid=(G, N // tn),
        in_specs=[
            pl.BlockSpec((B, D), lambda g, j: (0, 0)),
            pl.BlockSpec((1, D, tn), lambda g, j: (g, 0, j)),
            pl.BlockSpec((1, 1, tn), lambda g, j: (g, 0, j)),
        ],
        out_specs=pl.BlockSpec((1, B, tn), lambda g, j: (g, 0, j)),
        out_shape=jax.ShapeDtypeStruct((G, B, N), F32),
        name="adaln_mod",
    )(c, w, b.reshape(G, 1, N))


def _inproj_kernel(x_ref, sh_ref, sc_ref, g_ref, w_ref, wgc_ref, wgr_ref, bc_ref, br_ref,
                   q_ref, k_ref, v_ref, o_ref, gcol_ref, grow_ref, *, dq, dm, heads, qscale):
    x = x_ref[0]
    h = (_rms(x) * g_ref[...] * (1.0 + sc_ref[0]) + sh_ref[0]).astype(BF16)
    dot = functools.partial(jnp.dot, preferred_element_type=F32)
    q_ref[0] = (dot(h, w_ref[:, 0:dq]) * qscale).astype(BF16)
    k_ref[0] = dot(h, w_ref[:, dq:2 * dq]).astype(BF16)
    v_ref[0] = dot(h, w_ref[:, 2 * dq:2 * dq + dm]).astype(BF16)
    o_ref[0] = dot(h, w_ref[:, 2 * dq + dm:2 * dq + 2 * dm]).astype(BF16)
    gc = _softcap(dot(h, wgc_ref[...]) + bc_ref[...])
    lane = lax.broadcasted_iota(jnp.int32, gc.shape, 1)
    gcol_ref[0] = jnp.where(lane < heads, gc, _log_sigmoid(gc))
    gr = lax.dot_general(wgr_ref[...], h, (((1,), (1,)), ((), ())), preferred_element_type=F32)
    gr = _softcap(gr + br_ref[...])
    sub = lax.broadcasted_iota(jnp.int32, gr.shape, 0)
    grow_ref[0] = jnp.where(sub < heads, gr, _log_sigmoid(gr))


def _inproj_call(x, sh, sc, g, w_main, wg_col, wg_row, b_col, b_row, *, dq, dm, heads):
    B, S, D = x.shape
    tm = min(ROW_TILE, S)
    nmain = w_main.shape[1]
    row = lambda b, i: (b, i, 0)
    per_b = lambda b, i: (b, 0, 0)
    kern = functools.partial(_inproj_kernel, dq=dq, dm=dm, heads=heads, qscale=(dq // heads) ** -0.5)
    return pl.pallas_call(
        kern,
        grid=(B, S // tm),
        in_specs=[
            pl.BlockSpec((1, tm, D), row),
            pl.BlockSpec((1, 1, D), per_b),
            pl.BlockSpec((1, 1, D), per_b),
            _resident((1, D)),
            _resident((D, nmain)),
            _resident((D, LANES)),
            _resident((16, D)),
            _resident((1, LANES)),
            _resident((16, 1)),
        ],
        out_specs=[
            pl.BlockSpec((1, tm, dq), row),
            pl.BlockSpec((1, tm, dq), row),
            pl.BlockSpec((1, tm, dm), row),
            pl.BlockSpec((1, tm, dm), row),
            pl.BlockSpec((1, tm, LANES), row),
            pl.BlockSpec((1, 16, tm), lambda b, i: (b, 0, i)),
        ],
        out_shape=[
            jax.ShapeDtypeStruct((B, S, dq), BF16),
            jax.ShapeDtypeStruct((B, S, dq), BF16),
            jax.ShapeDtypeStruct((B, S, dm), BF16),
            jax.ShapeDtypeStruct((B, S, dm), BF16),
            jax.ShapeDtypeStruct((B, S, LANES), F32),
            jax.ShapeDtypeStruct((B, 16, S), F32),
        ],
        compiler_params=pltpu.CompilerParams(
            dimension_semantics=("arbitrary", "arbitrary"), vmem_limit_bytes=VMEM_LIMIT),
        name="mlstm_inproj",
    )(x, sh, sc, g, w_main, wg_col, wg_row, b_col, b_row)


def _mlstm_kernel(q_ref, k_ref, v_ref, o_ref, gcol_ref, grow_ref, hn_ref, y_ref,
                  C_ref, n_ref, m_ref, *, heads, dk, dv, L):
    @pl.when(pl.program_id(1) == 0)
    def _():
        C_ref[...] = jnp.zeros_like(C_ref)
        n_ref[...] = jnp.zeros_like(n_ref)
        m_ref[...] = jnp.zeros_like(m_ref)

    gcol = gcol_ref[0]
    grow = grow_ref[0]
    lower, upper = _tri(L)
    bcol = _cumsum_rows(lower.astype(BF16), gcol)
    brow = _cumsum_lanes(grow, upper.astype(BF16))
    nt = (((1,), (1,)), ((), ()))
    tn = (((0,), (0,)), ((), ()))

    for h in range(heads):
        i_row = grow[h:h + 1, :]
        b_row = brow[heads + h:heads + h + 1, :]
        i_col = gcol[:, h:h + 1]
        b_col = bcol[:, heads + h:heads + h + 1]
        bl = b_row[:, L - 1:L]
        m = m_ref[h][0:1, 0:1]
        qh = q_ref[0, :, h * dk:(h + 1) * dk]
        kh = k_ref[0, :, h * dk:(h + 1) * dk]
        vh = v_ref[0, :, h * dv:(h + 1) * dv]
        Ch = C_ref[h]
        nh = n_ref[h][0:1, :]

        dmat = jnp.where(lower, b_col + (i_row - b_row), -jnp.inf)
        inter = b_col + m
        m_t = jnp.maximum(inter, jnp.max(dmat, axis=1, keepdims=True))
        w = jnp.exp(dmat - m_t)
        a = jnp.exp(inter - m_t)
        s = lax.dot_general(qh, kh, nt, preferred_element_type=F32) * w
        num = (a * jnp.dot(qh, Ch.astype(BF16), preferred_element_type=F32)
               + jnp.dot(s.astype(BF16), vh, preferred_element_type=F32))
        qn = jnp.sum(qh.astype(F32) * nh, axis=1, keepdims=True)
        den = a * qn + jnp.sum(s, axis=1, keepdims=True)
        hh = num / jnp.maximum(jnp.abs(den), jnp.exp(-m_t))
        hn = _rms(hh) * hn_ref[:, h * dv:(h + 1) * dv]
        og = o_ref[0, :, h * dv:(h + 1) * dv].astype(F32)
        y_ref[0, :, h * dv:(h + 1) * dv] = (jax.nn.sigmoid(og) * hn).astype(BF16)

        ws = bl - b_col + i_col
        m_new = jnp.maximum(bl + m, jnp.max(ws, axis=0, keepdims=True))
        a_c = jnp.exp(bl + m - m_new)
        kw = kh.astype(F32) * jnp.exp(ws - m_new)
        C_ref[h] = a_c * Ch + lax.dot_general(kw.astype(BF16), vh, tn, preferred_element_type=F32)
        n_ref[h] = jnp.broadcast_to(a_c * nh + jnp.sum(kw, axis=0, keepdims=True), n_ref.shape[1:])
        m_ref[h] = jnp.broadcast_to(m_new, m_ref.shape[1:])


def _mlstm_call(q, k, v, o, gcol, grow, hn_g, *, heads):
    B, S, dq = q.shape
    dm = v.shape[-1]
    dk, dv = dq // heads, dm // heads
    L = min(MLSTM_CHUNK, S)
    row = lambda b, c: (b, c, 0)
    kern = functools.partial(_mlstm_kernel, heads=heads, dk=dk, dv=dv, L=L)
    return pl.pallas_call(
        kern,
        grid=(B, S // L),
        in_specs=[
            pl.BlockSpec((1, L, dq), row),
            pl.BlockSpec((1, L, dq), row),
            pl.BlockSpec((1, L, dm), row),
            pl.BlockSpec((1, L, dm), row),
            pl.BlockSpec((1, L, LANES), row),
            pl.BlockSpec((1, 16, L), lambda b, c: (b, 0, c)),
            _resident((1, dm)),
        ],
        out_specs=pl.BlockSpec((1, L, dm), row),
        out_shape=jax.ShapeDtypeStruct((B, S, dm), BF16),
        scratch_shapes=[
            pltpu.VMEM((heads, dk, dv), F32),
            pltpu.VMEM((heads, 8, dk), F32),
            pltpu.VMEM((heads, 8, LANES), F32),
        ],
        compiler_params=pltpu.CompilerParams(
            dimension_semantics=("arbitrary", "arbitrary"), vmem_limit_bytes=VMEM_LIMIT),
        name="mlstm_recurrence",
    )(q, k, v, o, gcol, grow, hn_g)


def _mix_ffn_kernel(y_ref, x_ref, wo_ref, pg_ref, g1_ref, fg_ref, sh2_ref, sc2_ref, g2_ref, fpg_ref,
                    wg_ref, wu_ref, wd_ref, out_ref, act_ref, *, dff):
    dot = functools.partial(jnp.dot, preferred_element_type=F32)
    t = dot(y_ref[0], wo_ref[...])
    x1 = x_ref[0] + g1_ref[0] * (_rms(t) * pg_ref[...])
    h2 = (_rms(x1) * fg_ref[...] * (1.0 + sc2_ref[0]) + sh2_ref[0]).astype(BF16)
    cw = FFN_COL_CHUNK
    for j in range(dff // cw):
        g = dot(h2, wg_ref[:, j * cw:(j + 1) * cw])
        u = dot(h2, wu_ref[:, j * cw:(j + 1) * cw])
        act_ref[:, j * cw:(j + 1) * cw] = (g * jax.nn.sigmoid(g) * u).astype(BF16)
    f = dot(act_ref[...], wd_ref[...])
    out_ref[0] = x1 + g2_ref[0] * (_rms(f) * fpg_ref[...])


def _mix_ffn_call(y, x, w_o, post_g, g1, ffn_pre_g, sh2, sc2, g2, ffn_post_g, w_gate, w_up, w_down):
    B, S, D = x.shape
    dff = w_down.shape[0]
    tm = min(ROW_TILE, S)
    row = lambda b, i: (b, i, 0)
    per_b = lambda b, i: (b, 0, 0)
    vec = pl.BlockSpec((1, 1, D), per_b)
    return pl.pallas_call(
        functools.partial(_mix_ffn_kernel, dff=dff),
        grid=(B, S // tm),
        in_specs=[
            pl.BlockSpec((1, tm, D), row),
            pl.BlockSpec((1, tm, D), row),
            _resident((D, D)),
            _resident((1, D)),
            vec,
            _resident((1, D)),
            vec, vec, vec,
            _resident((1, D)),
            _resident((D, dff)),
            _resident((D, dff)),
            _resident((dff, D)),
        ],
        out_specs=pl.BlockSpec((1, tm, D), row),
        out_shape=jax.ShapeDtypeStruct((B, S, D), F32),
        scratch_shapes=[pltpu.VMEM((tm, dff), BF16)],
        compiler_params=pltpu.CompilerParams(
            dimension_semantics=("arbitrary", "arbitrary"), vmem_limit_bytes=VMEM_LIMIT),
        name="mix_ffn",
    )(y, x, w_o, post_g, g1, ffn_pre_g, sh2, sc2, g2, ffn_post_g, w_gate, w_up, w_down)


def _qkv_kernel(x_ref, gq_ref, shq_ref, scq_ref, gkv_ref, shkv_ref, sckv_ref,
                wq_ref, wk_ref, wv_ref, wfc_ref, wfr_ref, bfc_ref, bfr_ref,
                q_ref, k_ref, v_ref, fcol_ref, frow_ref, ccol_ref, crow_ref, *, qscale):
    @pl.when(pl.program_id(1) == 0)
    def _():
        ccol_ref[...] = jnp.zeros_like(ccol_ref)
        crow_ref[...] = jnp.zeros_like(crow_ref)

    dot = functools.partial(jnp.dot, preferred_element_type=F32)
    xn = _rms(x_ref[0])
    hq = (xn * gq_ref[...] * (1.0 + scq_ref[0]) + shq_ref[0]).astype(BF16)
    hkv = (xn * gkv_ref[...] * (1.0 + sckv_ref[0]) + shkv_ref[0]).astype(BF16)
    q_ref[0] = (dot(hq, wq_ref[...]) * qscale).astype(BF16)
    k_ref[0] = dot(hkv, wk_ref[...]).astype(BF16)
    v_ref[0] = dot(hkv, wv_ref[...]).astype(BF16)
    tm = x_ref.shape[1]
    lower, upper = _tri(tm)
    lf_col = _log_sigmoid(dot(hkv, wfc_ref[...]) + bfc_ref[...])
    f_col = _cumsum_rows(lower.astype(BF16), lf_col) + ccol_ref[0:1, :]
    fcol_ref[0] = f_col
    ccol_ref[...] = jnp.broadcast_to(f_col[tm - 1:tm, :], ccol_ref.shape)
    lf_row = _log_sigmoid(
        lax.dot_general(wfr_ref[...], hkv, (((1,), (1,)), ((), ())), preferred_element_type=F32) + bfr_ref[...])
    f_row = _cumsum_lanes(lf_row, upper.astype(BF16)) + crow_ref[:, 0:1]
    frow_ref[0] = f_row
    crow_ref[...] = jnp.broadcast_to(f_row[:, tm - 1:tm], crow_ref.shape)


def _qkv_call(x, gq, shq, scq, gkv, shkv, sckv, wq, wk, wv, wf_col, wf_row, bf_col, bf_row, *, heads):
    B, S, D = x.shape
    tm = min(ROW_TILE, S)
    row = lambda b, i: (b, i, 0)
    per_b = lambda b, i: (b, 0, 0)
    vec = pl.BlockSpec((1, 1, D), per_b)
    return pl.pallas_call(
        functools.partial(_qkv_kernel, qscale=(D // heads) ** -0.5),
        grid=(B, S // tm),
        in_specs=[
            pl.BlockSpec((1, tm, D), row),
            _resident((1, D)), vec, vec,
            _resident((1, D)), vec, vec,
            _resident((D, D)), _resident((D, D)), _resident((D, D)),
            _resident((D, LANES)), _resident((heads, D)),
            _resident((1, LANES)), _resident((heads, 1)),
        ],
        out_specs=[
            pl.BlockSpec((1, tm, D), row),
            pl.BlockSpec((1, tm, D), row),
            pl.BlockSpec((1, tm, D), row),
            pl.BlockSpec((1, tm, LANES), row),
            pl.BlockSpec((1, heads, tm), lambda b, i: (b, 0, i)),
        ],
        out_shape=[
            jax.ShapeDtypeStruct((B, S, D), BF16),
            jax.ShapeDtypeStruct((B, S, D), BF16),
            jax.ShapeDtypeStruct((B, S, D), BF16),
            jax.ShapeDtypeStruct((B, S, LANES), F32),
            jax.ShapeDtypeStruct((B, heads, S), F32),
        ],
        scratch_shapes=[pltpu.VMEM((8, LANES), F32), pltpu.VMEM((heads, LANES), F32)],
        compiler_params=pltpu.CompilerParams(
            dimension_semantics=("arbitrary", "arbitrary"), vmem_limit_bytes=VMEM_LIMIT),
        name="fox_qkv",
    )(x, gq, shq, scq, gkv, shkv, sckv, wq, wk, wv, wf_col, wf_row, bf_col, bf_row)


def _fox_kernel(q_ref, k_ref, v_ref, fcol_ref, frow_ref, o_ref, acc_ref, m_ref, l_ref, *, hd, tq, tk):
    hp = pl.program_id(1)
    i = pl.program_id(2)
    lane = lax.broadcasted_iota(jnp.int32, (1, LANES), 1)
    q2 = q_ref[0]
    fcol = fcol_ref[0]
    first = lane < hd
    qm, fq = [], []
    for h in range(2):
        qm.append(jnp.where(first if h == 0 else jnp.logical_not(first), q2, jnp.zeros_like(q2)))
        fq.append(jnp.sum(jnp.where(lane == 2 * hp + h, fcol, 0.0), axis=1, keepdims=True))
        m_ref[h] = jnp.full(m_ref.shape[1:], -jnp.inf, F32)
        l_ref[h] = jnp.zeros(l_ref.shape[1:], F32)
        acc_ref[h] = jnp.zeros(acc_ref.shape[1:], F32)
    nt = (((1,), (1,)), ((), ()))

    def block(j, diagonal):
        start = pl.multiple_of(j * tk, tk)
        k2 = k_ref[0, pl.ds(start, tk), :]
        v2 = v_ref[0, pl.ds(start, tk), :]
        for h in range(2):
            fk = frow_ref[0, 0, h:h + 1, pl.ds(start, tk)]
            s = lax.dot_general(qm[h], k2, nt, preferred_element_type=F32) + fq[h] - fk
            if diagonal:
                r = lax.broadcasted_iota(jnp.int32, (tq, tk), 0)
                c = lax.broadcasted_iota(jnp.int32, (tq, tk), 1)
                s = jnp.where(r >= c, s, -jnp.inf)
            m_prev = m_ref[h]
            m_new = jnp.maximum(m_prev, jnp.max(s, axis=1, keepdims=True))
            alpha = jnp.exp(m_prev - m_new)
            p = jnp.exp(s - m_new)
            l_ref[h] = alpha * l_ref[h] + jnp.sum(p, axis=1, keepdims=True)
            acc_ref[h] = alpha * acc_ref[h] + jnp.dot(p.astype(BF16), v2, preferred_element_type=F32)
            m_ref[h] = m_new

    def body(j, carry):
        block(j, False)
        return carry

    lax.fori_loop(0, i, body, 0)
    block(i, True)
    o0 = acc_ref[0] / l_ref[0]
    o1 = acc_ref[1] / l_ref[1]
    o_ref[0] = jnp.where(first, o0, o1).astype(BF16)


def _fox_call(q, k, v, fcol, frow, *, heads):
    B, S, D = q.shape
    hd = D // heads
    assert 2 * hd == LANES
    t = min(ATTN_TILE, S)
    frow = frow.reshape(B, heads // 2, 2, S)
    return pl.pallas_call(
        functools.partial(_fox_kernel, hd=hd, tq=t, tk=t),
        grid=(B, heads // 2, S // t),
        in_specs=[
            pl.BlockSpec((1, t, LANES), lambda b, p, i: (b, i, p)),
            pl.BlockSpec((1, S, LANES), lambda b, p, i: (b, 0, p)),
            pl.BlockSpec((1, S, LANES), lambda b, p, i: (b, 0, p)),
            pl.BlockSpec((1, t, LANES), lambda b, p, i: (b, i, 0)),
            pl.BlockSpec((1, 1, 2, S), lambda b, p, i: (b, p, 0, 0)),
        ],
        out_specs=pl.BlockSpec((1, t, LANES), lambda b, p, i: (b, i, p)),
        out_shape=jax.ShapeDtypeStruct((B, S, D), BF16),
        scratch_shapes=[
            pltpu.VMEM((2, t, LANES), F32),
            pltpu.VMEM((2, t, 1), F32),
            pltpu.VMEM((2, t, 1), F32),
        ],
        compiler_params=pltpu.CompilerParams(
            dimension_semantics=("arbitrary", "arbitrary", "arbitrary"), vmem_limit_bytes=VMEM_LIMIT),
        name="fox_attention",
    )(q, k, v, fcol, frow)


def _pad_cols(w, n):
    return jnp.pad(w, ((0, 0), (0, n - w.shape[1])))


def kernel(x, c, mod_w, mod_b, mix_pre_g, mix_post_g, ffn_pre_g, ffn_post_g, ffn_w_in, ffn_w_out,
           a_w_in, a_gate_b, a_hnorm_g, a_w_out, kv_norm_g, kv_mod_w, kv_mod_b, kv_w, kv_fgate_b,
           b_w_q, b_w_out):
    B, S, D = x.shape
    depth = mod_w.shape[0]
    n_a = a_w_in.shape[0]
    m_heads = a_gate_b.shape[1] // 2
    f_heads = kv_fgate_b.shape[0]
    dff = ffn_w_out.shape[1]
    dq = (a_w_in.shape[2] - 2 * D - 2 * m_heads) // 2

    mods = _mod_call(c, mod_w, mod_b)
    kv_mods = _mod_call(c, kv_mod_w[None], kv_mod_b[None])[0]
    vecs = lambda m, n: [v.reshape(B, 1, D) for v in jnp.split(m, n, axis=-1)]
    row = lambda g: g.reshape(1, D)

    k_sh = v_sh = fcol = frow = None
    for l in range(depth):
        sh1, sc1, g1, sh2, sc2, g2 = vecs(mods[l], 6)
        if l < n_a:
            w = a_w_in[l]
            nmain = 2 * dq + 2 * D
            wg = w[:, nmain:]
            q, k, v, o, gcol, grow = _inproj_call(
                x, sh1, sc1, row(mix_pre_g[l]),
                w[:, :nmain].astype(BF16), _pad_cols(wg, LANES).astype(BF16),
                _pad_cols(wg, 16).T.astype(BF16),
                _pad_cols(a_gate_b[l][None, :], LANES), _pad_cols(a_gate_b[l][None, :], 16).T,
                dq=dq, dm=D, heads=m_heads)
            y = _mlstm_call(q, k, v, o, gcol, grow, row(a_hnorm_g[l]), heads=m_heads)
            w_o = a_w_out[l]
        else:
            if l == n_a:
                kv_sh, kv_sc = vecs(kv_mods, 2)
                wf = kv_w[:, 2 * D:]
                shared = dict(
                    gkv=row(kv_norm_g), shkv=kv_sh, sckv=kv_sc,
                    wk=kv_w[:, :D].astype(BF16), wv=kv_w[:, D:2 * D].astype(BF16),
                    wf_col=_pad_cols(wf, LANES).astype(BF16), wf_row=wf.T.astype(BF16),
                    bf_col=_pad_cols(kv_fgate_b[None, :], LANES), bf_row=kv_fgate_b[:, None])
            q, k_new, v_new, fcol_new, frow_new = _qkv_call(
                x, row(mix_pre_g[l]), sh1, sc1, shared["gkv"], shared["shkv"], shared["sckv"],
                b_w_q[l - n_a].astype(BF16), shared["wk"], shared["wv"],
                shared["wf_col"], shared["wf_row"], shared["bf_col"], shared["bf_row"], heads=f_heads)
            if l == n_a:
                k_sh, v_sh, fcol, frow = k_new, v_new, fcol_new, frow_new
            y = _fox_call(q, k_sh, v_sh, fcol, frow, heads=f_heads)
            w_o = b_w_out[l - n_a]
        w_in = ffn_w_in[l]
        x = _mix_ffn_call(
            y, x, w_o.astype(BF16), row(mix_post_g[l]), g1, row(ffn_pre_g[l]), sh2, sc2, g2,
            row(ffn_post_g[l]), w_in[:, :dff].astype(BF16), w_in[:, dff:].astype(BF16),
            ffn_w_out[l].astype(BF16))
    return x
```

```python
import functools

import jax
import jax.numpy as jnp
from jax import lax
from jax.experimental import pallas as pl
from jax.experimental.pallas import tpu as pltpu

F32 = jnp.float32
BF16 = jnp.bfloat16

EPS = 1e-6
LOG2E = 1.4426950408889634
GATE_CAP = 15.0
LANES = 128
MLSTM_CHUNK = 256
ROW_TILE = 512
ATTN_TILE = 512
FFN_COL_CHUNK = 256
VMEM_LIMIT = 56 * 1024 * 1024


def _rms(x):
    return x * lax.rsqrt(jnp.mean(x * x, axis=-1, keepdims=True) + EPS)


def _log_sigmoid(x):
    return jnp.minimum(x, 0.0) - jnp.log(1.0 + jnp.exp(-jnp.abs(x)))


def _softcap(a):
    return GATE_CAP * jnp.tanh(a / GATE_CAP)


def _split3(x):
    hi = x.astype(BF16)
    r1 = x - hi.astype(F32)
    mid = r1.astype(BF16)
    lo = (r1 - mid.astype(F32)).astype(BF16)
    return hi, mid, lo


def _cumsum_rows(tri_lower, x):
    return sum(jnp.dot(tri_lower, p, preferred_element_type=F32) for p in _split3(x))


def _cumsum_lanes(x, tri_upper):
    return sum(jnp.dot(p, tri_upper, preferred_element_type=F32) for p in _split3(x))


def _tri(n):
    r = lax.broadcasted_iota(jnp.int32, (n, n), 0)
    c = lax.broadcasted_iota(jnp.int32, (n, n), 1)
    return r >= c, (r <= c)


def _resident(shape):
    nd = len(shape)
    return pl.BlockSpec(shape, lambda *_: (0,) * nd, pipeline_mode=pl.Buffered(1))


def _mod_kernel(c_ref, w_ref, b_ref, o_ref):
    c = c_ref[...]
    cs = (c * jax.nn.sigmoid(c)).astype(BF16)
    o_ref[0] = jnp.dot(cs, w_ref[0].astype(BF16), preferred_element_type=F32) + b_ref[0]


def _mod_call(c, w, b):
    G, D, N = w.shape
    B = c.shape[0]
    tn = 1024
    return pl.pallas_call(
        _mod_kernel,
        grid=(G, N // tn),
        in_specs=[
            pl.BlockSpec((B, D), lambda g, j: (0, 0)),
            pl.BlockSpec((1, D, tn), lambda g, j: (g, 0, j)),
            pl.BlockSpec((1, 1, tn), lambda g, j: (g, 0, j)),
        ],
        out_specs=pl.BlockSpec((1, B, tn), lambda g, j: (g, 0, j)),
        out_shape=jax.ShapeDtypeStruct((G, B, N), F32),
        name="adaln_mod",
    )(c, w, b.reshape(G, 1, N))


def _inproj_kernel(x_ref, sh_ref, sc_ref, g_ref, w_ref, wgc_ref, wgr_ref, bc_ref, br_ref,
                   q_ref, k_ref, v_ref, o_ref, gcol_ref, grow_ref, *, dq, dm, heads, qscale):
    x = x_ref[0]
    h = (_rms(x) * g_ref[...] * (1.0 + sc_ref[0]) + sh_ref[0]).astype(BF16)
    dot = functools.partial(jnp.dot, preferred_element_type=F32)
    q_ref[0] = (dot(h, w_ref[:, 0:dq]) * qscale).astype(BF16)
    k_ref[0] = dot(h, w_ref[:, dq:2 * dq]).astype(BF16)
    v_ref[0] = dot(h, w_ref[:, 2 * dq:2 * dq + dm]).astype(BF16)
    o_ref[0] = dot(h, w_ref[:, 2 * dq + dm:2 * dq + 2 * dm]).astype(BF16)
    gc = _softcap(dot(h, wgc_ref[...]) + bc_ref[...])
    lane = lax.broadcasted_iota(jnp.int32, gc.shape, 1)
    gcol_ref[0] = jnp.where(lane < heads, gc, _log_sigmoid(gc))
    gr = lax.dot_general(wgr_ref[...], h, (((1,), (1,)), ((), ())), preferred_element_type=F32)
    gr = _softcap(gr + br_ref[...])
    sub = lax.broadcasted_iota(jnp.int32, gr.shape, 0)
    grow_ref[0] = jnp.where(sub < heads, gr, _log_sigmoid(gr))


def _inproj_call(x, sh, sc, g, w_main, wg_col, wg_row, b_col, b_row, *, dq, dm, heads):
    B, S, D = x.shape
    tm = min(ROW_TILE, S)
    nmain = w_main.shape[1]
    row = lambda b, i: (b, i, 0)
    per_b = lambda b, i: (b, 0, 0)
    kern = functools.partial(_inproj_kernel, dq=dq, dm=dm, heads=heads, qscale=(dq // heads) ** -0.5)
    return pl.pallas_call(
        kern,
        grid=(B, S // tm),
        in_specs=[
            pl.BlockSpec((1, tm, D), row),
            pl.BlockSpec((1, 1, D), per_b),
            pl.BlockSpec((1, 1, D), per_b),
            _resident((1, D)),
            _resident((D, nmain)),
            _resident((D, LANES)),
            _resident((16, D)),
            _resident((1, LANES)),
            _resident((16, 1)),
        ],
        out_specs=[
            pl.BlockSpec((1, tm, dq), row),
            pl.BlockSpec((1, tm, dq), row),
            pl.BlockSpec((1, tm, dm), row),
            pl.BlockSpec((1, tm, dm), row),
            pl.BlockSpec((1, tm, LANES), row),
            pl.BlockSpec((1, 16, tm), lambda b, i: (b, 0, i)),
        ],
        out_shape=[
            jax.ShapeDtypeStruct((B, S, dq), BF16),
            jax.ShapeDtypeStruct((B, S, dq), BF16),
            jax.ShapeDtypeStruct((B, S, dm), BF16),
            jax.ShapeDtypeStruct((B, S, dm), BF16),
            jax.ShapeDtypeStruct((B, S, LANES), F32),
            jax.ShapeDtypeStruct((B, 16, S), F32),
        ],
        compiler_params=pltpu.CompilerParams(
            dimension_semantics=("arbitrary", "arbitrary"), vmem_limit_bytes=VMEM_LIMIT),
        name="mlstm_inproj",
    )(x, sh, sc, g, w_main, wg_col, wg_row, b_col, b_row)


def _mlstm_kernel(q_ref, k_ref, v_ref, o_ref, gcol_ref, grow_ref, hn_ref, y_ref,
                  C_ref, n_ref, m_ref, *, heads, dk, dv, L):
    @pl.when(pl.program_id(1) == 0)
    def _():
        C_ref[...] = jnp.zeros_like(C_ref)
        n_ref[...] = jnp.zeros_like(n_ref)
        m_ref[...] = jnp.zeros_like(m_ref)

    gcol = gcol_ref[0]
    grow = grow_ref[0]
    lower, upper = _tri(L)
    bcol = _cumsum_rows(lower.astype(BF16), gcol)
    brow = _cumsum_lanes(grow, upper.astype(BF16))
    nt = (((1,), (1,)), ((), ()))
    tn = (((0,), (0,)), ((), ()))

    for h in range(heads):
        i_row = grow[h:h + 1, :]
        b_row = brow[heads + h:heads + h + 1, :]
        i_col = gcol[:, h:h + 1]
        b_col = bcol[:, heads + h:heads + h + 1]
        bl = b_row[:, L - 1:L]
        m = m_ref[h][0:1, 0:1]
        qh = q_ref[0, :, h * dk:(h + 1) * dk]
        kh = k_ref[0, :, h * dk:(h + 1) * dk]
        vh = v_ref[0, :, h * dv:(h + 1) * dv]
        Ch = C_ref[h]
        nh = n_ref[h][0:1, :]

        dmat = jnp.where(lower, b_col + (i_row - b_row), -jnp.inf)
        inter = b_col + m
        m_t = jnp.maximum(inter, jnp.max(dmat, axis=1, keepdims=True))
        w = jnp.exp(dmat - m_t)
        a = jnp.exp(inter - m_t)
        s = lax.dot_general(qh, kh, nt, preferred_element_type=F32) * w
        num = (a * jnp.dot(qh, Ch.astype(BF16), preferred_element_type=F32)
               + jnp.dot(s.astype(BF16), vh, preferred_element_type=F32))
        qn = jnp.sum(qh.astype(F32) * nh, axis=1, keepdims=True)
        den = a * qn + jnp.sum(s, axis=1, keepdims=True)
        hh = num / jnp.maximum(jnp.abs(den), jnp.exp(-m_t))
        hn = _rms(hh) * hn_ref[:, h * dv:(h + 1) * dv]
        og = o_ref[0, :, h * dv:(h + 1) * dv].astype(F32)
        y_ref[0, :, h * dv:(h + 1) * dv] = (jax.nn.sigmoid(og) * hn).astype(BF16)

        ws = bl - b_col + i_col
        m_new = jnp.maximum(bl + m, jnp.max(ws, axis=0, keepdims=True))
        a_c = jnp.exp(bl + m - m_new)
        kw = kh.astype(F32) * jnp.exp(ws - m_new)
        C_ref[h] = a_c * Ch + lax.dot_general(kw.astype(BF16), vh, tn, preferred_element_type=F32)
        n_ref[h] = jnp.broadcast_to(a_c * nh + jnp.sum(kw, axis=0, keepdims=True), n_ref.shape[1:])
        m_ref[h] = jnp.broadcast_to(m_new, m_ref.shape[1:])


def _mlstm_call(q, k, v, o, gcol, grow, hn_g, *, heads):
    B, S, dq = q.shape
    dm = v.shape[-1]
    dk, dv = dq // heads, dm // heads
    L = min(MLSTM_CHUNK, S)
    row = lambda b, c: (b, c, 0)
    kern = functools.partial(_mlstm_kernel, heads=heads, dk=dk, dv=dv, L=L)
    return pl.pallas_call(
        kern,
        grid=(B, S // L),
        in_specs=[
            pl.BlockSpec((1, L, dq), row),
            pl.BlockSpec((1, L, dq), row),
            pl.BlockSpec((1, L, dm), row),
            pl.BlockSpec((1, L, dm), row),
            pl.BlockSpec((1, L, LANES), row),
            pl.BlockSpec((1, 16, L), lambda b, c: (b, 0, c)),
            _resident((1, dm)),
        ],
        out_specs=pl.BlockSpec((1, L, dm), row),
        out_shape=jax.ShapeDtypeStruct((B, S, dm), BF16),
        scratch_shapes=[
            pltpu.VMEM((heads, dk, dv), F32),
            pltpu.VMEM((heads, 8, dk), F32),
            pltpu.VMEM((heads, 8, LANES), F32),
        ],
        compiler_params=pltpu.CompilerParams(
            dimension_semantics=("arbitrary", "arbitrary"), vmem_limit_bytes=VMEM_LIMIT),
        name="mlstm_recurrence",
    )(q, k, v, o, gcol, grow, hn_g)


def _mix_ffn_kernel(y_ref, x_ref, wo_ref, pg_ref, g1_ref, fg_ref, sh2_ref, sc2_ref, g2_ref, fpg_ref,
                    wg_ref, wu_ref, wd_ref, out_ref, act_ref, *, dff):
    dot = functools.partial(jnp.dot, preferred_element_type=F32)
    t = dot(y_ref[0], wo_ref[...])
    x1 = x_ref[0] + g1_ref[0] * (_rms(t) * pg_ref[...])
    h2 = (_rms(x1) * fg_ref[...] * (1.0 + sc2_ref[0]) + sh2_ref[0]).astype(BF16)
    cw = FFN_COL_CHUNK
    for j in range(dff // cw):
        g = dot(h2, wg_ref[:, j * cw:(j + 1) * cw])
        u = dot(h2, wu_ref[:, j * cw:(j + 1) * cw])
        act_ref[:, j * cw:(j + 1) * cw] = (g * jax.nn.sigmoid(g) * u).astype(BF16)
    f = dot(act_ref[...], wd_ref[...])
    out_ref[0] = x1 + g2_ref[0] * (_rms(f) * fpg_ref[...])


def _mix_ffn_call(y, x, w_o, post_g, g1, ffn_pre_g, sh2, sc2, g2, ffn_post_g, w_gate, w_up, w_down):
    B, S, D = x.shape
    dff = w_down.shape[0]
    tm = min(ROW_TILE, S)
    row = lambda b, i: (b, i, 0)
    per_b = lambda b, i: (b, 0, 0)
    vec = pl.BlockSpec((1, 1, D), per_b)
    return pl.pallas_call(
        functools.partial(_mix_ffn_kernel, dff=dff),
        grid=(B, S // tm),
        in_specs=[
            pl.BlockSpec((1, tm, D), row),
            pl.BlockSpec((1, tm, D), row),
            _resident((D, D)),
            _resident((1, D)),
            vec,
            _resident((1, D)),
            vec, vec, vec,
            _resident((1, D)),
            _resident((D, dff)),
            _resident((D, dff)),
            _resident((dff, D)),
        ],
        out_specs=pl.BlockSpec((1, tm, D), row),
        out_shape=jax.ShapeDtypeStruct((B, S, D), F32),
        scratch_shapes=[pltpu.VMEM((tm, dff), BF16)],
        compiler_params=pltpu.CompilerParams(
            dimension_semantics=("arbitrary", "arbitrary"), vmem_limit_bytes=VMEM_LIMIT),
        name="mix_ffn",
    )(y, x, w_o, post_g, g1, ffn_pre_g, sh2, sc2, g2, ffn_post_g, w_gate, w_up, w_down)


def _qkv_kernel(x_ref, gq_ref, shq_ref, scq_ref, gkv_ref, shkv_ref, sckv_ref,
                wq_ref, wk_ref, wv_ref, wfc_ref, wfr_ref, bfc_ref, bfr_ref,
                q_ref, k_ref, v_ref, fcol_ref, frow_ref, ccol_ref, crow_ref, *, qscale):
    @pl.when(pl.program_id(1) == 0)
    def _():
        ccol_ref[...] = jnp.zeros_like(ccol_ref)
        crow_ref[...] = jnp.zeros_like(crow_ref)

    dot = functools.partial(jnp.dot, preferred_element_type=F32)
    xn = _rms(x_ref[0])
    hq = (xn * gq_ref[...] * (1.0 + scq_ref[0]) + shq_ref[0]).astype(BF16)
    hkv = (xn * gkv_ref[...] * (1.0 + sckv_ref[0]) + shkv_ref[0]).astype(BF16)
    q_ref[0] = (dot(hq, wq_ref[...]) * qscale).astype(BF16)
    k_ref[0] = dot(hkv, wk_ref[...]).astype(BF16)
    v_ref[0] = dot(hkv, wv_ref[...]).astype(BF16)
    tm = x_ref.shape[1]
    lower, upper = _tri(tm)
    lf_col = _log_sigmoid(dot(hkv, wfc_ref[...]) + bfc_ref[...])
    f_col = _cumsum_rows(lower.astype(BF16), lf_col) + ccol_ref[0:1, :]
    fcol_ref[0] = f_col * LOG2E
    ccol_ref[...] = jnp.broadcast_to(f_col[tm - 1:tm, :], ccol_ref.shape)
    lf_row = _log_sigmoid(
        lax.dot_general(wfr_ref[...], hkv, (((1,), (1,)), ((), ())), preferred_element_type=F32) + bfr_ref[...])
    f_row = _cumsum_lanes(lf_row, upper.astype(BF16)) + crow_ref[:, 0:1]
    frow_ref[0] = f_row * LOG2E
    crow_ref[...] = jnp.broadcast_to(f_row[:, tm - 1:tm], crow_ref.shape)


def _qkv_call(x, gq, shq, scq, gkv, shkv, sckv, wq, wk, wv, wf_col, wf_row, bf_col, bf_row, *, heads):
    B, S, D = x.shape
    tm = min(ROW_TILE, S)
    row = lambda b, i: (b, i, 0)
    per_b = lambda b, i: (b, 0, 0)
    vec = pl.BlockSpec((1, 1, D), per_b)
    return pl.pallas_call(
        functools.partial(_qkv_kernel, qscale=(D // heads) ** -0.5 * LOG2E),
        grid=(B, S // tm),
        in_specs=[
            pl.BlockSpec((1, tm, D), row),
            _resident((1, D)), vec, vec,
            _resident((1, D)), vec, vec,
            _resident((D, D)), _resident((D, D)), _resident((D, D)),
            _resident((D, LANES)), _resident((heads, D)),
            _resident((1, LANES)), _resident((heads, 1)),
        ],
        out_specs=[
            pl.BlockSpec((1, tm, D), row),
            pl.BlockSpec((1, tm, D), row),
            pl.BlockSpec((1, tm, D), row),
            pl.BlockSpec((1, tm, LANES), row),
            pl.BlockSpec((1, heads, tm), lambda b, i: (b, 0, i)),
        ],
        out_shape=[
            jax.ShapeDtypeStruct((B, S, D), BF16),
            jax.ShapeDtypeStruct((B, S, D), BF16),
            jax.ShapeDtypeStruct((B, S, D), BF16),
            jax.ShapeDtypeStruct((B, S, LANES), F32),
            jax.ShapeDtypeStruct((B, heads, S), F32),
        ],
        scratch_shapes=[pltpu.VMEM((8, LANES), F32), pltpu.VMEM((heads, LANES), F32)],
        compiler_params=pltpu.CompilerParams(
            dimension_semantics=("arbitrary", "arbitrary"), vmem_limit_bytes=VMEM_LIMIT),
        name="fox_qkv",
    )(x, gq, shq, scq, gkv, shkv, sckv, wq, wk, wv, wf_col, wf_row, bf_col, bf_row)


def _fox_kernel(q_ref, k_ref, v_ref, fcol_ref, frow_ref, o_ref, vt_ref, fkb_ref, acc_ref, *, hd, tq, tk):
    hp = pl.program_id(1)
    i = pl.program_id(2)
    S = k_ref.shape[1]
    lane = lax.broadcasted_iota(jnp.int32, (1, LANES), 1)

    @pl.when(i == 0)
    def _():
        for c in range(S // tk):
            rows = slice(c * tk, (c + 1) * tk)
            vt_ref[:, rows] = v_ref[0, rows, :].astype(F32).T.astype(BF16)
            fc = fcol_ref[0, rows, :]
            for h in range(2):
                col = jnp.sum(jnp.where(lane == 2 * hp + h, fc, 0.0), axis=1, keepdims=True)
                fkb_ref[h, rows, :] = jnp.broadcast_to(col, (tk, LANES))

    qT = q_ref[0].astype(F32).T
    sub = lax.broadcasted_iota(jnp.int32, (LANES, 1), 0)
    qstart = pl.multiple_of(i * tq, tq)
    qmT, fq = [], []
    for h in range(2):
        qmT.append(jnp.where((sub < hd) == (h == 0), qT, 0.0).astype(BF16))
        fq.append(frow_ref[0, 0, h:h + 1, pl.ds(qstart, tq)])
        acc_ref[h] = jnp.zeros(acc_ref.shape[1:], F32)

    def block(j, carry, diagonal):
        start = pl.multiple_of(j * tk, tk)
        k2 = k_ref[0, pl.ds(start, tk), :]
        v2t = vt_ref[:, pl.ds(start, tk)]
        out = []
        for h in range(2):
            m_prev, l_prev = carry[h]
            fk = fkb_ref[h, pl.ds(start, tk), :]
            a = jnp.dot(k2, qmT[h], preferred_element_type=F32) - jnp.concatenate([fk] * (tq // LANES), axis=1)
            if diagonal:
                r = lax.broadcasted_iota(jnp.int32, (tk, tq), 0)
                c = lax.broadcasted_iota(jnp.int32, (tk, tq), 1)
                a = jnp.where(r <= c, a, -jnp.inf)
            m_new = jnp.maximum(m_prev, jnp.max(a, axis=0, keepdims=True) + fq[h])
            alpha = jnp.exp2(m_prev - m_new)
            p = jnp.exp2(a - (m_new - fq[h]))
            l_new = alpha * l_prev + jnp.sum(p, axis=0, keepdims=True)
            acc_ref[h] = alpha * acc_ref[h] + jnp.dot(v2t, p.astype(BF16), preferred_element_type=F32)
            out.append((m_new, l_new))
        return tuple(out)

    init = tuple((jnp.full((1, tq), -jnp.inf, F32), jnp.zeros((1, tq), F32)) for _ in range(2))
    carry = lax.fori_loop(0, i, lambda j, c: block(j, c, False), init)
    (_, l0), (_, l1) = block(i, carry, True)
    ot = jnp.concatenate([acc_ref[0, 0:hd, :] / l0, acc_ref[1, hd:2 * hd, :] / l1], axis=0)
    o_ref[0] = ot.T.astype(BF16)


def _fox_call(q, k, v, fcol, frow, *, heads):
    B, S, D = q.shape
    hd = D // heads
    assert 2 * hd == LANES
    t = min(ATTN_TILE, S)
    frow = frow.reshape(B, heads // 2, 2, S)
    return pl.pallas_call(
        functools.partial(_fox_kernel, hd=hd, tq=t, tk=t),
        grid=(B, heads // 2, S // t),
        in_specs=[
            pl.BlockSpec((1, t, LANES), lambda b, p, i: (b, i, p)),
            pl.BlockSpec((1, S, LANES), lambda b, p, i: (b, 0, p)),
            pl.BlockSpec((1, S, LANES), lambda b, p, i: (b, 0, p)),
            pl.BlockSpec((1, S, LANES), lambda b, p, i: (b, 0, 0)),
            pl.BlockSpec((1, 1, 2, S), lambda b, p, i: (b, p, 0, 0)),
        ],
        out_specs=pl.BlockSpec((1, t, LANES), lambda b, p, i: (b, i, p)),
        out_shape=jax.ShapeDtypeStruct((B, S, D), BF16),
        scratch_shapes=[
            pltpu.VMEM((LANES, S), BF16),
            pltpu.VMEM((2, S, LANES), F32),
            pltpu.VMEM((2, LANES, t), F32),
        ],
        compiler_params=pltpu.CompilerParams(
            dimension_semantics=("arbitrary", "arbitrary", "arbitrary"), vmem_limit_bytes=VMEM_LIMIT),
        name="fox_attention",
    )(q, k, v, fcol, frow)


def _pad_cols(w, n):
    return jnp.pad(w, ((0, 0), (0, n - w.shape[1])))


def kernel(x, c, mod_w, mod_b, mix_pre_g, mix_post_g, ffn_pre_g, ffn_post_g, ffn_w_in, ffn_w_out,
           a_w_in, a_gate_b, a_hnorm_g, a_w_out, kv_norm_g, kv_mod_w, kv_mod_b, kv_w, kv_fgate_b,
           b_w_q, b_w_out):
    B, S, D = x.shape
    depth = mod_w.shape[0]
    n_a = a_w_in.shape[0]
    m_heads = a_gate_b.shape[1] // 2
    f_heads = kv_fgate_b.shape[0]
    dff = ffn_w_out.shape[1]
    dq = (a_w_in.shape[2] - 2 * D - 2 * m_heads) // 2

    mods = _mod_call(c, mod_w, mod_b)
    kv_mods = _mod_call(c, kv_mod_w[None], kv_mod_b[None])[0]
    vecs = lambda m, n: [v.reshape(B, 1, D) for v in jnp.split(m, n, axis=-1)]
    row = lambda g: g.reshape(1, D)

    k_sh = v_sh = fcol = frow = None
    for l in range(depth):
        sh1, sc1, g1, sh2, sc2, g2 = vecs(mods[l], 6)
        if l < n_a:
            w = a_w_in[l]
            nmain = 2 * dq + 2 * D
            wg = w[:, nmain:]
            q, k, v, o, gcol, grow = _inproj_call(
                x, sh1, sc1, row(mix_pre_g[l]),
                w[:, :nmain].astype(BF16), _pad_cols(wg, LANES).astype(BF16),
                _pad_cols(wg, 16).T.astype(BF16),
                _pad_cols(a_gate_b[l][None, :], LANES), _pad_cols(a_gate_b[l][None, :], 16).T,
                dq=dq, dm=D, heads=m_heads)
            y = _mlstm_call(q, k, v, o, gcol, grow, row(a_hnorm_g[l]), heads=m_heads)
            w_o = a_w_out[l]
        else:
            if l == n_a:
                kv_sh, kv_sc = vecs(kv_mods, 2)
                wf = kv_w[:, 2 * D:]
                shared = dict(
                    gkv=row(kv_norm_g), shkv=kv_sh, sckv=kv_sc,
                    wk=kv_w[:, :D].astype(BF16), wv=kv_w[:, D:2 * D].astype(BF16),
                    wf_col=_pad_cols(wf, LANES).astype(BF16), wf_row=wf.T.astype(BF16),
                    bf_col=_pad_cols(kv_fgate_b[None, :], LANES), bf_row=kv_fgate_b[:, None])
            q, k_new, v_new, fcol_new, frow_new = _qkv_call(
                x, row(mix_pre_g[l]), sh1, sc1, shared["gkv"], shared["shkv"], shared["sckv"],
                b_w_q[l - n_a].astype(BF16), shared["wk"], shared["wv"],
                shared["wf_col"], shared["wf_row"], shared["bf_col"], shared["bf_row"], heads=f_heads)
            if l == n_a:
                k_sh, v_sh, fcol, frow = k_new, v_new, fcol_new, frow_new
            y = _fox_call(q, k_sh, v_sh, fcol, frow, heads=f_heads)
            w_o = b_w_out[l - n_a]
        w_in = ffn_w_in[l]
        x = _mix_ffn_call(
            y, x, w_o.astype(BF16), row(mix_post_g[l]), g1, row(ffn_pre_g[l]), sh2, sc2, g2,
            row(ffn_post_g[l]), w_in[:, :dff].astype(BF16), w_in[:, dff:].astype(BF16),
            ffn_w_out[l].astype(BF16))
    return x
```

```python
import functools

import jax
import jax.numpy as jnp
from jax import lax
from jax.experimental import pallas as pl
from jax.experimental.pallas import tpu as pltpu

F32 = jnp.float32
BF16 = jnp.bfloat16

EPS = 1e-6
LOG2E = 1.4426950408889634
GATE_CAP = 15.0
LANES = 128
MLSTM_CHUNK = 256
ROW_TILE = 512
ATTN_TILE = 512
FFN_COL_CHUNK = 256
VMEM_LIMIT = 56 * 1024 * 1024


def _rms(x):
    return x * lax.rsqrt(jnp.mean(x * x, axis=-1, keepdims=True) + EPS)


def _log_sigmoid(x):
    return jnp.minimum(x, 0.0) - jnp.log(1.0 + jnp.exp(-jnp.abs(x)))


def _softcap(a):
    return GATE_CAP * jnp.tanh(a / GATE_CAP)


def _split3(x):
    hi = x.astype(BF16)
    r1 = x - hi.astype(F32)
    mid = r1.astype(BF16)
    lo = (r1 - mid.astype(F32)).astype(BF16)
    return hi, mid, lo


def _cumsum_rows(tri_lower, x):
    return sum(jnp.dot(tri_lower, p, preferred_element_type=F32) for p in _split3(x))


def _cumsum_lanes(x, tri_upper):
    return sum(jnp.dot(p, tri_upper, preferred_element_type=F32) for p in _split3(x))


def _tri(n):
    r = lax.broadcasted_iota(jnp.int32, (n, n), 0)
    c = lax.broadcasted_iota(jnp.int32, (n, n), 1)
    return r >= c, (r <= c)


def _resident(shape):
    nd = len(shape)
    return pl.BlockSpec(shape, lambda *_: (0,) * nd, pipeline_mode=pl.Buffered(1))


def _mod_kernel(c_ref, w_ref, b_ref, o_ref):
    c = c_ref[...]
    cs = (c * jax.nn.sigmoid(c)).astype(BF16)
    o_ref[0] = jnp.dot(cs, w_ref[0].astype(BF16), preferred_element_type=F32) + b_ref[0]


def _mod_call(c, w, b):
    G, D, N = w.shape
    B = c.shape[0]
    tn = 1024
    return pl.pallas_call(
        _mod_kernel,
        grid=(G, N // tn),
        in_specs=[
            pl.BlockSpec((B, D), lambda g, j: (0, 0)),
            pl.BlockSpec((1, D, tn), lambda g, j: (g, 0, j)),
            pl.BlockSpec((1, 1, tn), lambda g, j: (g, 0, j)),
        ],
        out_specs=pl.BlockSpec((1, B, tn), lambda g, j: (g, 0, j)),
        out_shape=jax.ShapeDtypeStruct((G, B, N), F32),
        name="adaln_mod",
    )(c, w, b.reshape(G, 1, N))


def _inproj_kernel(x_ref, sh_ref, sc_ref, g_ref, w_ref, wgc_ref, wgr_ref, bc_ref, br_ref,
                   q_ref, k_ref, v_ref, o_ref, gcol_ref, grow_ref, *, dq, dm, heads, qscale):
    x = x_ref[0]
    h = (_rms(x) * g_ref[...] * (1.0 + sc_ref[0]) + sh_ref[0]).astype(BF16)
    dot = functools.partial(jnp.dot, preferred_element_type=F32)
    q_ref[0] = (dot(h, w_ref[:, 0:dq]) * qscale).astype(BF16)
    k_ref[0] = dot(h, w_ref[:, dq:2 * dq]).astype(BF16)
    v_ref[0] = dot(h, w_ref[:, 2 * dq:2 * dq + dm]).astype(BF16)
    o_ref[0] = dot(h, w_ref[:, 2 * dq + dm:2 * dq + 2 * dm]).astype(BF16)
    gc = _softcap(dot(h, wgc_ref[...]) + bc_ref[...])
    lane = lax.broadcasted_iota(jnp.int32, gc.shape, 1)
    gcol_ref[0] = jnp.where(lane < heads, gc, _log_sigmoid(gc))
    gr = lax.dot_general(wgr_ref[...], h, (((1,), (1,)), ((), ())), preferred_element_type=F32)
    gr = _softcap(gr + br_ref[...])
    sub = lax.broadcasted_iota(jnp.int32, gr.shape, 0)
    grow_ref[0] = jnp.where(sub < heads, gr, _log_sigmoid(gr))


def _inproj_call(x, sh, sc, g, w_main, wg_col, wg_row, b_col, b_row, *, dq, dm, heads):
    B, S, D = x.shape
    tm = min(ROW_TILE, S)
    nmain = w_main.shape[1]
    row = lambda b, i: (b, i, 0)
    per_b = lambda b, i: (b, 0, 0)
    kern = functools.partial(_inproj_kernel, dq=dq, dm=dm, heads=heads, qscale=(dq // heads) ** -0.5)
    return pl.pallas_call(
        kern,
        grid=(B, S // tm),
        in_specs=[
            pl.BlockSpec((1, tm, D), row),
            pl.BlockSpec((1, 1, D), per_b),
            pl.BlockSpec((1, 1, D), per_b),
            _resident((1, D)),
            _resident((D, nmain)),
            _resident((D, LANES)),
            _resident((16, D)),
            _resident((1, LANES)),
            _resident((16, 1)),
        ],
        out_specs=[
            pl.BlockSpec((1, tm, dq), row),
            pl.BlockSpec((1, tm, dq), row),
            pl.BlockSpec((1, tm, dm), row),
            pl.BlockSpec((1, tm, dm), row),
            pl.BlockSpec((1, tm, LANES), row),
            pl.BlockSpec((1, 16, tm), lambda b, i: (b, 0, i)),
        ],
        out_shape=[
            jax.ShapeDtypeStruct((B, S, dq), BF16),
            jax.ShapeDtypeStruct((B, S, dq), BF16),
            jax.ShapeDtypeStruct((B, S, dm), BF16),
            jax.ShapeDtypeStruct((B, S, dm), BF16),
            jax.ShapeDtypeStruct((B, S, LANES), F32),
            jax.ShapeDtypeStruct((B, 16, S), F32),
        ],
        compiler_params=pltpu.CompilerParams(
            dimension_semantics=("arbitrary", "arbitrary"), vmem_limit_bytes=VMEM_LIMIT),
        name="mlstm_inproj",
    )(x, sh, sc, g, w_main, wg_col, wg_row, b_col, b_row)


def _mlstm_kernel(q_ref, k_ref, v_ref, o_ref, gcol_ref, grow_ref, hn_ref, y_ref,
                  C_ref, n_ref, m_ref, *, heads, dk, dv, L):
    @pl.when(pl.program_id(1) == 0)
    def _():
        C_ref[...] = jnp.zeros_like(C_ref)
        n_ref[...] = jnp.zeros_like(n_ref)
        m_ref[...] = jnp.zeros_like(m_ref)

    gcol = gcol_ref[0]
    grow = grow_ref[0]
    lower, upper = _tri(L)
    bcol = _cumsum_rows(lower.astype(BF16), gcol)
    brow = _cumsum_lanes(grow, upper.astype(BF16))
    nt = (((1,), (1,)), ((), ()))
    tn = (((0,), (0,)), ((), ()))

    for h in range(heads):
        i_row = grow[h:h + 1, :]
        b_row = brow[heads + h:heads + h + 1, :]
        i_col = gcol[:, h:h + 1]
        b_col = bcol[:, heads + h:heads + h + 1]
        bl = b_row[:, L - 1:L]
        m = m_ref[h][0:1, 0:1]
        qh = q_ref[0, :, h * dk:(h + 1) * dk]
        kh = k_ref[0, :, h * dk:(h + 1) * dk]
        vh = v_ref[0, :, h * dv:(h + 1) * dv]
        Ch = C_ref[h]
        nh = n_ref[h][0:1, :]

        dmat = jnp.where(lower, b_col + (i_row - b_row), -jnp.inf)
        inter = b_col + m
        m_t = jnp.maximum(inter, jnp.max(dmat, axis=1, keepdims=True))
        w = jnp.exp(dmat - m_t)
        a = jnp.exp(inter - m_t)
        s = lax.dot_general(qh, kh, nt, preferred_element_type=F32) * w
        num = (a * jnp.dot(qh, Ch.astype(BF16), preferred_element_type=F32)
               + jnp.dot(s.astype(BF16), vh, preferred_element_type=F32))
        qn = jnp.sum(qh.astype(F32) * nh, axis=1, keepdims=True)
        den = a * qn + jnp.sum(s, axis=1, keepdims=True)
        hh = num / jnp.maximum(jnp.abs(den), jnp.exp(-m_t))
        hn = _rms(hh) * hn_ref[:, h * dv:(h + 1) * dv]
        og = o_ref[0, :, h * dv:(h + 1) * dv].astype(F32)
        y_ref[0, :, h * dv:(h + 1) * dv] = (jax.nn.sigmoid(og) * hn).astype(BF16)

        ws = bl - b_col + i_col
        m_new = jnp.maximum(bl + m, jnp.max(ws, axis=0, keepdims=True))
        a_c = jnp.exp(bl + m - m_new)
        kw = kh.astype(F32) * jnp.exp(ws - m_new)
        C_ref[h] = a_c * Ch + lax.dot_general(kw.astype(BF16), vh, tn, preferred_element_type=F32)
        n_ref[h] = jnp.broadcast_to(a_c * nh + jnp.sum(kw, axis=0, keepdims=True), n_ref.shape[1:])
        m_ref[h] = jnp.broadcast_to(m_new, m_ref.shape[1:])


def _mlstm_call(q, k, v, o, gcol, grow, hn_g, *, heads):
    B, S, dq = q.shape
    dm = v.shape[-1]
    dk, dv = dq // heads, dm // heads
    L = min(MLSTM_CHUNK, S)
    row = lambda b, c: (b, c, 0)
    kern = functools.partial(_mlstm_kernel, heads=heads, dk=dk, dv=dv, L=L)
    return pl.pallas_call(
        kern,
        grid=(B, S // L),
        in_specs=[
            pl.BlockSpec((1, L, dq), row),
            pl.BlockSpec((1, L, dq), row),
            pl.BlockSpec((1, L, dm), row),
            pl.BlockSpec((1, L, dm), row),
            pl.BlockSpec((1, L, LANES), row),
            pl.BlockSpec((1, 16, L), lambda b, c: (b, 0, c)),
            _resident((1, dm)),
        ],
        out_specs=pl.BlockSpec((1, L, dm), row),
        out_shape=jax.ShapeDtypeStruct((B, S, dm), BF16),
        scratch_shapes=[
            pltpu.VMEM((heads, dk, dv), F32),
            pltpu.VMEM((heads, 8, dk), F32),
            pltpu.VMEM((heads, 8, LANES), F32),
        ],
        compiler_params=pltpu.CompilerParams(
            dimension_semantics=("arbitrary", "arbitrary"), vmem_limit_bytes=VMEM_LIMIT),
        name="mlstm_recurrence",
    )(q, k, v, o, gcol, grow, hn_g)


def _mix_ffn_kernel(y_ref, x_ref, wo_ref, pg_ref, g1_ref, fg_ref, sh2_ref, sc2_ref, g2_ref, fpg_ref,
                    wg_ref, wu_ref, wd_ref, out_ref, act_ref, *, dff):
    dot = functools.partial(jnp.dot, preferred_element_type=F32)
    t = dot(y_ref[0], wo_ref[...])
    x1 = x_ref[0] + g1_ref[0] * (_rms(t) * pg_ref[...])
    h2 = (_rms(x1) * fg_ref[...] * (1.0 + sc2_ref[0]) + sh2_ref[0]).astype(BF16)
    cw = FFN_COL_CHUNK
    for j in range(dff // cw):
        g = dot(h2, wg_ref[:, j * cw:(j + 1) * cw])
        u = dot(h2, wu_ref[:, j * cw:(j + 1) * cw])
        act_ref[:, j * cw:(j + 1) * cw] = (g * jax.nn.sigmoid(g) * u).astype(BF16)
    f = dot(act_ref[...], wd_ref[...])
    out_ref[0] = x1 + g2_ref[0] * (_rms(f) * fpg_ref[...])


def _mix_ffn_call(y, x, w_o, post_g, g1, ffn_pre_g, sh2, sc2, g2, ffn_post_g, w_gate, w_up, w_down):
    B, S, D = x.shape
    dff = w_down.shape[0]
    tm = min(ROW_TILE, S)
    row = lambda b, i: (b, i, 0)
    per_b = lambda b, i: (b, 0, 0)
    vec = pl.BlockSpec((1, 1, D), per_b)
    return pl.pallas_call(
        functools.partial(_mix_ffn_kernel, dff=dff),
        grid=(B, S // tm),
        in_specs=[
            pl.BlockSpec((1, tm, D), row),
            pl.BlockSpec((1, tm, D), row),
            _resident((D, D)),
            _resident((1, D)),
            vec,
            _resident((1, D)),
            vec, vec, vec,
            _resident((1, D)),
            _resident((D, dff)),
            _resident((D, dff)),
            _resident((dff, D)),
        ],
        out_specs=pl.BlockSpec((1, tm, D), row),
        out_shape=jax.ShapeDtypeStruct((B, S, D), F32),
        scratch_shapes=[pltpu.VMEM((tm, dff), BF16)],
        compiler_params=pltpu.CompilerParams(
            dimension_semantics=("arbitrary", "arbitrary"), vmem_limit_bytes=VMEM_LIMIT),
        name="mix_ffn",
    )(y, x, w_o, post_g, g1, ffn_pre_g, sh2, sc2, g2, ffn_post_g, w_gate, w_up, w_down)


def _qkv_kernel(x_ref, gq_ref, shq_ref, scq_ref, gkv_ref, shkv_ref, sckv_ref,
                wq_ref, wk_ref, wv_ref, wfc_ref, wfr_ref, bfc_ref, bfr_ref,
                q_ref, k_ref, v_ref, fcol_ref, frow_ref, ccol_ref, crow_ref, *, qscale):
    @pl.when(pl.program_id(1) == 0)
    def _():
        ccol_ref[...] = jnp.zeros_like(ccol_ref)
        crow_ref[...] = jnp.zeros_like(crow_ref)

    dot = functools.partial(jnp.dot, preferred_element_type=F32)
    xn = _rms(x_ref[0])
    hq = (xn * gq_ref[...] * (1.0 + scq_ref[0]) + shq_ref[0]).astype(BF16)
    hkv = (xn * gkv_ref[...] * (1.0 + sckv_ref[0]) + shkv_ref[0]).astype(BF16)
    q_ref[0] = (dot(hq, wq_ref[...]) * qscale).astype(BF16)
    k_ref[0] = dot(hkv, wk_ref[...]).astype(BF16)
    v_ref[0] = dot(hkv, wv_ref[...]).astype(BF16)
    tm = x_ref.shape[1]
    lower, upper = _tri(tm)
    lf_col = _log_sigmoid(dot(hkv, wfc_ref[...]) + bfc_ref[...])
    f_col = _cumsum_rows(lower.astype(BF16), lf_col) + ccol_ref[0:1, :]
    fcol_ref[0] = f_col * LOG2E
    ccol_ref[...] = jnp.broadcast_to(f_col[tm - 1:tm, :], ccol_ref.shape)
    lf_row = _log_sigmoid(
        lax.dot_general(wfr_ref[...], hkv, (((1,), (1,)), ((), ())), preferred_element_type=F32) + bfr_ref[...])
    f_row = _cumsum_lanes(lf_row, upper.astype(BF16)) + crow_ref[:, 0:1]
    frow_ref[0] = f_row * LOG2E
    crow_ref[...] = jnp.broadcast_to(f_row[:, tm - 1:tm], crow_ref.shape)


def _qkv_call(x, gq, shq, scq, gkv, shkv, sckv, wq, wk, wv, wf_col, wf_row, bf_col, bf_row, *, heads):
    B, S, D = x.shape
    tm = min(ROW_TILE, S)
    row = lambda b, i: (b, i, 0)
    per_b = lambda b, i: (b, 0, 0)
    vec = pl.BlockSpec((1, 1, D), per_b)
    return pl.pallas_call(
        functools.partial(_qkv_kernel, qscale=(D // heads) ** -0.5 * LOG2E),
        grid=(B, S // tm),
        in_specs=[
            pl.BlockSpec((1, tm, D), row),
            _resident((1, D)), vec, vec,
            _resident((1, D)), vec, vec,
            _resident((D, D)), _resident((D, D)), _resident((D, D)),
            _resident((D, LANES)), _resident((heads, D)),
            _resident((1, LANES)), _resident((heads, 1)),
        ],
        out_specs=[
            pl.BlockSpec((1, tm, D), row),
            pl.BlockSpec((1, tm, D), row),
            pl.BlockSpec((1, tm, D), row),
            pl.BlockSpec((1, tm, LANES), row),
            pl.BlockSpec((1, heads, tm), lambda b, i: (b, 0, i)),
        ],
        out_shape=[
            jax.ShapeDtypeStruct((B, S, D), BF16),
            jax.ShapeDtypeStruct((B, S, D), BF16),
            jax.ShapeDtypeStruct((B, S, D), BF16),
            jax.ShapeDtypeStruct((B, S, LANES), F32),
            jax.ShapeDtypeStruct((B, heads, S), F32),
        ],
        scratch_shapes=[pltpu.VMEM((8, LANES), F32), pltpu.VMEM((heads, LANES), F32)],
        compiler_params=pltpu.CompilerParams(
            dimension_semantics=("arbitrary", "arbitrary"), vmem_limit_bytes=VMEM_LIMIT),
        name="fox_qkv",
    )(x, gq, shq, scq, gkv, shkv, sckv, wq, wk, wv, wf_col, wf_row, bf_col, bf_row)


def _fox_kernel(q_ref, k_ref, v_ref, fcol_ref, frow_ref, o_ref, vt_ref, fkb_ref, acc_ref,
                a0_ref, a1_ref, p0_ref, p1_ref, *, hd, tq, tk):
    hp = pl.program_id(1)
    i = pl.program_id(2)
    S = k_ref.shape[1]
    lane = lax.broadcasted_iota(jnp.int32, (1, LANES), 1)
    a_ref = (a0_ref, a1_ref)
    p_ref = (p0_ref, p1_ref)

    @pl.when(i == 0)
    def _():
        for c in range(S // tk):
            rows = slice(c * tk, (c + 1) * tk)
            vt_ref[:, rows] = v_ref[0, rows, :].astype(F32).T.astype(BF16)
            fc = fcol_ref[0, rows, :]
            for h in range(2):
                col = jnp.sum(jnp.where(lane == 2 * hp + h, fc, 0.0), axis=1, keepdims=True)
                fkb_ref[h, rows, :] = jnp.broadcast_to(col, (tk, LANES))

    qT = q_ref[0].astype(F32).T
    sub = lax.broadcasted_iota(jnp.int32, (LANES, 1), 0)
    qstart = pl.multiple_of(i * tq, tq)
    qmT, fq = [], []
    for h in range(2):
        qmT.append(jnp.where((sub < hd) == (h == 0), qT, 0.0).astype(BF16))
        fq.append(frow_ref[0, 0, h:h + 1, pl.ds(qstart, tq)])
        acc_ref[h] = jnp.zeros(acc_ref.shape[1:], F32)

    def qk(j, h):
        start = pl.multiple_of(j * tk, tk)
        fk = fkb_ref[h, pl.ds(start, tk), :]
        a_ref[h][...] = (jnp.dot(k_ref[0, pl.ds(start, tk), :], qmT[h], preferred_element_type=F32)
                         - jnp.concatenate([fk] * (tq // LANES), axis=1))

    def softmax(h, m_prev, l_prev, diagonal):
        a = a_ref[h][...]
        if diagonal:
            r = lax.broadcasted_iota(jnp.int32, (tk, tq), 0)
            c = lax.broadcasted_iota(jnp.int32, (tk, tq), 1)
            a = jnp.where(r <= c, a, -jnp.inf)
        m_new = jnp.maximum(m_prev, jnp.max(a, axis=0, keepdims=True) + fq[h])
        alpha = jnp.exp2(m_prev - m_new)
        p = jnp.exp2(a - (m_new - fq[h]))
        p_ref[h][...] = p.astype(BF16)
        return m_new, alpha * l_prev + jnp.sum(p, axis=0, keepdims=True), alpha

    def pv(j, h, alpha):
        start = pl.multiple_of(j * tk, tk)
        v2t = vt_ref[h * hd:(h + 1) * hd, pl.ds(start, tk)]
        acc_ref[h] = alpha * acc_ref[h] + jnp.dot(v2t, p_ref[h][...], preferred_element_type=F32)

    def body(j, carry, diagonal):
        m0, l0, m1, l1, alpha1 = carry
        pv(jnp.maximum(j - 1, 0), 1, alpha1)
        m0, l0, alpha0 = softmax(0, m0, l0, diagonal)
        qk(j, 1)
        pv(j, 0, alpha0)
        m1, l1, alpha1 = softmax(1, m1, l1, diagonal)
        if not diagonal:
            qk(j + 1, 0)
        return m0, l0, m1, l1, alpha1

    qk(0, 0)
    p1_ref[...] = jnp.zeros(p1_ref.shape, BF16)
    neg = jnp.full((1, tq), -jnp.inf, F32)
    zero = jnp.zeros((1, tq), F32)
    carry = lax.fori_loop(0, i, lambda j, c: body(j, c, False), (neg, zero, neg, zero, zero + 1.0))
    _, l0, _, l1, alpha1 = body(i, carry, True)
    pv(i, 1, alpha1)
    ot = jnp.concatenate([acc_ref[0] / l0, acc_ref[1] / l1], axis=0)
    o_ref[0] = ot.T.astype(BF16)


def _fox_call(q, k, v, fcol, frow, *, heads):
    B, S, D = q.shape
    hd = D // heads
    assert 2 * hd == LANES
    t = min(ATTN_TILE, S)
    frow = frow.reshape(B, heads // 2, 2, S)
    return pl.pallas_call(
        functools.partial(_fox_kernel, hd=hd, tq=t, tk=t),
        grid=(B, heads // 2, S // t),
        in_specs=[
            pl.BlockSpec((1, t, LANES), lambda b, p, i: (b, i, p)),
            pl.BlockSpec((1, S, LANES), lambda b, p, i: (b, 0, p)),
            pl.BlockSpec((1, S, LANES), lambda b, p, i: (b, 0, p)),
            pl.BlockSpec((1, S, LANES), lambda b, p, i: (b, 0, 0)),
            pl.BlockSpec((1, 1, 2, S), lambda b, p, i: (b, p, 0, 0)),
        ],
        out_specs=pl.BlockSpec((1, t, LANES), lambda b, p, i: (b, i, p)),
        out_shape=jax.ShapeDtypeStruct((B, S, D), BF16),
        scratch_shapes=[
            pltpu.VMEM((LANES, S), BF16),
            pltpu.VMEM((2, S, LANES), F32),
            pltpu.VMEM((2, hd, t), F32),
            pltpu.VMEM((t, t), F32), pltpu.VMEM((t, t), F32),
            pltpu.VMEM((t, t), BF16), pltpu.VMEM((t, t), BF16),
        ],
        compiler_params=pltpu.CompilerParams(
            dimension_semantics=("arbitrary", "arbitrary", "arbitrary"), vmem_limit_bytes=VMEM_LIMIT),
        name="fox_attention",
    )(q, k, v, fcol, frow)


def _pad_cols(w, n):
    return jnp.pad(w, ((0, 0), (0, n - w.shape[1])))


def kernel(x, c, mod_w, mod_b, mix_pre_g, mix_post_g, ffn_pre_g, ffn_post_g, ffn_w_in, ffn_w_out,
           a_w_in, a_gate_b, a_hnorm_g, a_w_out, kv_norm_g, kv_mod_w, kv_mod_b, kv_w, kv_fgate_b,
           b_w_q, b_w_out):
    B, S, D = x.shape
    depth = mod_w.shape[0]
    n_a = a_w_in.shape[0]
    m_heads = a_gate_b.shape[1] // 2
    f_heads = kv_fgate_b.shape[0]
    dff = ffn_w_out.shape[1]
    dq = (a_w_in.shape[2] - 2 * D - 2 * m_heads) // 2

    mods = _mod_call(c, mod_w, mod_b)
    kv_mods = _mod_call(c, kv_mod_w[None], kv_mod_b[None])[0]
    vecs = lambda m, n: [v.reshape(B, 1, D) for v in jnp.split(m, n, axis=-1)]
    row = lambda g: g.reshape(1, D)

    k_sh = v_sh = fcol = frow = None
    for l in range(depth):
        sh1, sc1, g1, sh2, sc2, g2 = vecs(mods[l], 6)
        if l < n_a:
            w = a_w_in[l]
            nmain = 2 * dq + 2 * D
            wg = w[:, nmain:]
            q, k, v, o, gcol, grow = _inproj_call(
                x, sh1, sc1, row(mix_pre_g[l]),
                w[:, :nmain].astype(BF16), _pad_cols(wg, LANES).astype(BF16),
                _pad_cols(wg, 16).T.astype(BF16),
                _pad_cols(a_gate_b[l][None, :], LANES), _pad_cols(a_gate_b[l][None, :], 16).T,
                dq=dq, dm=D, heads=m_heads)
            y = _mlstm_call(q, k, v, o, gcol, grow, row(a_hnorm_g[l]), heads=m_heads)
            w_o = a_w_out[l]
        else:
            if l == n_a:
                kv_sh, kv_sc = vecs(kv_mods, 2)
                wf = kv_w[:, 2 * D:]
                shared = dict(
                    gkv=row(kv_norm_g), shkv=kv_sh, sckv=kv_sc,
                    wk=kv_w[:, :D].astype(BF16), wv=kv_w[:, D:2 * D].astype(BF16),
                    wf_col=_pad_cols(wf, LANES).astype(BF16), wf_row=wf.T.astype(BF16),
                    bf_col=_pad_cols(kv_fgate_b[None, :], LANES), bf_row=kv_fgate_b[:, None])
            q, k_new, v_new, fcol_new, frow_new = _qkv_call(
                x, row(mix_pre_g[l]), sh1, sc1, shared["gkv"], shared["shkv"], shared["sckv"],
                b_w_q[l - n_a].astype(BF16), shared["wk"], shared["wv"],
                shared["wf_col"], shared["wf_row"], shared["bf_col"], shared["bf_row"], heads=f_heads)
            if l == n_a:
                k_sh, v_sh, fcol, frow = k_new, v_new, fcol_new, frow_new
            y = _fox_call(q, k_sh, v_sh, fcol, frow, heads=f_heads)
            w_o = b_w_out[l - n_a]
        w_in = ffn_w_in[l]
        x = _mix_ffn_call(
            y, x, w_o.astype(BF16), row(mix_post_g[l]), g1, row(ffn_pre_g[l]), sh2, sc2, g2,
            row(ffn_post_g[l]), w_in[:, :dff].astype(BF16), w_in[:, dff:].astype(BF16),
            ffn_w_out[l].astype(BF16))
    return x
```

```python
import functools

import jax
import jax.numpy as jnp
from jax import lax
from jax.experimental import pallas as pl
from jax.experimental.pallas import tpu as pltpu

F32 = jnp.float32
BF16 = jnp.bfloat16

EPS = 1e-6
LOG2E = 1.4426950408889634
GATE_CAP = 15.0
LANES = 128
MLSTM_CHUNK = 256
ROW_TILE = 512
ATTN_TILE = 512
FFN_COL_CHUNK = 256
PV_EXTRA_ROWS = 16
MLSTM_EXTRA_ROWS = 16
VMEM_LIMIT = 56 * 1024 * 1024


def _rms(x):
    return x * lax.rsqrt(jnp.mean(x * x, axis=-1, keepdims=True) + EPS)


def _log_sigmoid(x):
    return jnp.minimum(x, 0.0) - jnp.log(1.0 + jnp.exp(-jnp.abs(x)))


def _softcap(a):
    return GATE_CAP * jnp.tanh(a / GATE_CAP)


def _split3(x):
    hi = x.astype(BF16)
    r1 = x - hi.astype(F32)
    mid = r1.astype(BF16)
    lo = (r1 - mid.astype(F32)).astype(BF16)
    return hi, mid, lo


def _cumsum_rows(tri_lower, x):
    return sum(jnp.dot(tri_lower, p, preferred_element_type=F32) for p in _split3(x))


def _cumsum_lanes(x, tri_upper):
    return sum(jnp.dot(p, tri_upper, preferred_element_type=F32) for p in _split3(x))


def _tri(n):
    r = lax.broadcasted_iota(jnp.int32, (n, n), 0)
    c = lax.broadcasted_iota(jnp.int32, (n, n), 1)
    return r >= c, (r <= c)


def _resident(shape):
    nd = len(shape)
    return pl.BlockSpec(shape, lambda *_: (0,) * nd, pipeline_mode=pl.Buffered(1))


def _mod_kernel(c_ref, w_ref, b_ref, o_ref):
    c = c_ref[...]
    cs = (c * jax.nn.sigmoid(c)).astype(BF16)
    o_ref[0] = jnp.dot(cs, w_ref[0].astype(BF16), preferred_element_type=F32) + b_ref[0]


def _mod_call(c, w, b):
    G, D, N = w.shape
    B = c.shape[0]
    tn = 1024
    return pl.pallas_call(
        _mod_kernel,
        grid=(G, N // tn),
        in_specs=[
            pl.BlockSpec((B, D), lambda g, j: (0, 0)),
            pl.BlockSpec((1, D, tn), lambda g, j: (g, 0, j)),
            pl.BlockSpec((1, 1, tn), lambda g, j: (g, 0, j)),
        ],
        out_specs=pl.BlockSpec((1, B, tn), lambda g, j: (g, 0, j)),
        out_shape=jax.ShapeDtypeStruct((G, B, N), F32),
        name="adaln_mod",
    )(c, w, b.reshape(G, 1, N))


def _inproj_kernel(x_ref, sh_ref, sc_ref, g_ref, wko_ref, wqvt_ref, wgc_ref, wgr_ref, bc_ref, br_ref,
                   qt_ref, k_ref, vt_ref, o_ref, gcol_ref, grow_ref, *, dq, dm, heads, qscale):
    x = x_ref[0]
    tm = x.shape[0]
    h = (_rms(x) * g_ref[...] * (1.0 + sc_ref[0]) + sh_ref[0]).astype(BF16)
    nt = (((1,), (1,)), ((), ()))
    ko = jnp.dot(h, wko_ref[...], preferred_element_type=F32)
    k_ref[0] = ko[:, 0:dq].astype(BF16)
    o_ref[0] = ko[:, dq:dq + dm].astype(BF16)
    qvt = lax.dot_general(wqvt_ref[...], h, nt, preferred_element_type=F32)
    qt_ref[0] = (qvt[0:dq] * qscale).astype(BF16)
    dv = dm // heads
    dvx = dv + MLSTM_EXTRA_ROWS
    ones_row = (lax.broadcasted_iota(jnp.int32, (MLSTM_EXTRA_ROWS, tm), 0) == 0).astype(BF16)
    for hh in range(heads):
        vt_ref[0, hh * dvx:hh * dvx + dv, :] = qvt[dq + hh * dv:dq + (hh + 1) * dv].astype(BF16)
        vt_ref[0, hh * dvx + dv:(hh + 1) * dvx, :] = ones_row
    gc = _softcap(jnp.dot(h, wgc_ref[...], preferred_element_type=F32) + bc_ref[...])
    lane = lax.broadcasted_iota(jnp.int32, gc.shape, 1)
    gcol_ref[0] = jnp.where(lane < heads, gc, _log_sigmoid(gc)) * LOG2E
    gr = _softcap(lax.dot_general(wgr_ref[...], h, nt, preferred_element_type=F32) + br_ref[...])
    sub = lax.broadcasted_iota(jnp.int32, gr.shape, 0)
    grow_ref[0] = jnp.where(sub < heads, gr, _log_sigmoid(gr)) * LOG2E


def _inproj_call(x, sh, sc, g, w_ko, w_qv_t, wg_col, wg_row, b_col, b_row, *, dq, dm, heads):
    B, S, D = x.shape
    tm = min(ROW_TILE, S)
    dvx_all = dm + heads * MLSTM_EXTRA_ROWS
    row = lambda b, i: (b, i, 0)
    col = lambda b, i: (b, 0, i)
    per_b = lambda b, i: (b, 0, 0)
    kern = functools.partial(_inproj_kernel, dq=dq, dm=dm, heads=heads, qscale=(dq // heads) ** -0.5)
    return pl.pallas_call(
        kern,
        grid=(B, S // tm),
        in_specs=[
            pl.BlockSpec((1, tm, D), row),
            pl.BlockSpec((1, 1, D), per_b),
            pl.BlockSpec((1, 1, D), per_b),
            _resident((1, D)),
            _resident((D, dq + dm)),
            _resident((dq + dm, D)),
            _resident((D, LANES)),
            _resident((16, D)),
            _resident((1, LANES)),
            _resident((16, 1)),
        ],
        out_specs=[
            pl.BlockSpec((1, dq, tm), col),
            pl.BlockSpec((1, tm, dq), row),
            pl.BlockSpec((1, dvx_all, tm), col),
            pl.BlockSpec((1, tm, dm), row),
            pl.BlockSpec((1, tm, LANES), row),
            pl.BlockSpec((1, 16, tm), col),
        ],
        out_shape=[
            jax.ShapeDtypeStruct((B, dq, S), BF16),
            jax.ShapeDtypeStruct((B, S, dq), BF16),
            jax.ShapeDtypeStruct((B, dvx_all, S), BF16),
            jax.ShapeDtypeStruct((B, S, dm), BF16),
            jax.ShapeDtypeStruct((B, S, LANES), F32),
            jax.ShapeDtypeStruct((B, 16, S), F32),
        ],
        compiler_params=pltpu.CompilerParams(
            dimension_semantics=("arbitrary", "arbitrary"), vmem_limit_bytes=VMEM_LIMIT),
        name="mlstm_inproj",
    )(x, sh, sc, g, w_ko, w_qv_t, wg_col, wg_row, b_col, b_row)


def _mlstm_kernel(qt_ref, k_ref, vt_ref, o_ref, gcol_ref, grow_ref, hn_ref, y_ref,
                  st_ref, m_ref, *, heads, dk, dv, L):
    @pl.when(pl.program_id(1) == 0)
    def _():
        st_ref[...] = jnp.zeros_like(st_ref)
        m_ref[...] = jnp.zeros_like(m_ref)

    gcol = gcol_ref[0]
    grow = grow_ref[0]
    lower, upper = _tri(L)
    bcol = _cumsum_rows(lower.astype(BF16), gcol)
    brow = _cumsum_lanes(grow, upper.astype(BF16))
    dvx = dv + MLSTM_EXTRA_ROWS
    dot = functools.partial(jnp.dot, preferred_element_type=F32)

    for h in range(heads):
        i_row = grow[h:h + 1, :]
        b_row = brow[heads + h:heads + h + 1, :]
        i_col = gcol[:, h:h + 1]
        b_col = bcol[:, heads + h:heads + h + 1]
        bl = b_row[:, L - 1:L]
        m = m_ref[h][0:1, 0:1]
        qt = qt_ref[0, h * dk:(h + 1) * dk, :]
        kh = k_ref[0, :, h * dk:(h + 1) * dk]
        vt = vt_ref[0, h * dvx:(h + 1) * dvx, :]
        st = st_ref[h]

        dmat = jnp.where(upper, b_row + (i_col - b_col), -jnp.inf)
        inter = b_row + m
        m_t = jnp.maximum(inter, jnp.max(dmat, axis=0, keepdims=True))
        a = jnp.exp2(inter - m_t)
        s = dot(kh, qt) * jnp.exp2(dmat - m_t)
        na = a * dot(st.astype(BF16), qt) + dot(vt, s.astype(BF16))
        hT = na[0:dv] * (1.0 / jnp.maximum(jnp.abs(na[dv:dv + 1]), jnp.exp2(-m_t)))
        hnT = hT * lax.rsqrt(jnp.mean(hT * hT, axis=0, keepdims=True) + EPS)
        og = o_ref[0, :, h * dv:(h + 1) * dv].astype(F32)
        y_ref[0, :, h * dv:(h + 1) * dv] = (jax.nn.sigmoid(og) * (hnT.T * hn_ref[:, h * dv:(h + 1) * dv])).astype(BF16)

        ws = bl - b_col + i_col
        m_new = jnp.maximum(bl + m, jnp.max(ws, axis=0, keepdims=True))
        kw = (kh.astype(F32) * jnp.exp2(ws - m_new)).astype(BF16)
        st_ref[h] = jnp.exp2(bl + m - m_new) * st + dot(vt, kw)
        m_ref[h] = jnp.broadcast_to(m_new, m_ref.shape[1:])


def _mlstm_call(qt, k, vt, o, gcol, grow, hn_g, *, heads):
    B, S, dq = k.shape
    dm = o.shape[-1]
    dk, dv = dq // heads, dm // heads
    dvx_all = vt.shape[1]
    L = min(MLSTM_CHUNK, S)
    row = lambda b, c: (b, c, 0)
    col = lambda b, c: (b, 0, c)
    kern = functools.partial(_mlstm_kernel, heads=heads, dk=dk, dv=dv, L=L)
    return pl.pallas_call(
        kern,
        grid=(B, S // L),
        in_specs=[
            pl.BlockSpec((1, dq, L), col),
            pl.BlockSpec((1, L, dq), row),
            pl.BlockSpec((1, dvx_all, L), col),
            pl.BlockSpec((1, L, dm), row),
            pl.BlockSpec((1, L, LANES), row),
            pl.BlockSpec((1, 16, L), col),
            _resident((1, dm)),
        ],
        out_specs=pl.BlockSpec((1, L, dm), row),
        out_shape=jax.ShapeDtypeStruct((B, S, dm), BF16),
        scratch_shapes=[
            pltpu.VMEM((heads, dv + MLSTM_EXTRA_ROWS, dk), F32),
            pltpu.VMEM((heads, 8, LANES), F32),
        ],
        compiler_params=pltpu.CompilerParams(
            dimension_semantics=("arbitrary", "arbitrary"), vmem_limit_bytes=VMEM_LIMIT),
        name="mlstm_recurrence",
    )(qt, k, vt, o, gcol, grow, hn_g)


def _mix_ffn_kernel(y_ref, x_ref, wo_ref, pg_ref, g1_ref, fg_ref, sh2_ref, sc2_ref, g2_ref, fpg_ref,
                    wg_ref, wu_ref, wd_ref, out_ref, act_ref, *, dff):
    dot = functools.partial(jnp.dot, preferred_element_type=F32)
    t = dot(y_ref[0], wo_ref[...])
    x1 = x_ref[0] + g1_ref[0] * (_rms(t) * pg_ref[...])
    h2 = (_rms(x1) * fg_ref[...] * (1.0 + sc2_ref[0]) + sh2_ref[0]).astype(BF16)
    cw = FFN_COL_CHUNK
    for j in range(dff // cw):
        g = dot(h2, wg_ref[:, j * cw:(j + 1) * cw])
        u = dot(h2, wu_ref[:, j * cw:(j + 1) * cw])
        act_ref[:, j * cw:(j + 1) * cw] = (g * jax.nn.sigmoid(g) * u).astype(BF16)
    f = dot(act_ref[...], wd_ref[...])
    out_ref[0] = x1 + g2_ref[0] * (_rms(f) * fpg_ref[...])


def _mix_ffn_call(y, x, w_o, post_g, g1, ffn_pre_g, sh2, sc2, g2, ffn_post_g, w_gate, w_up, w_down):
    B, S, D = x.shape
    dff = w_down.shape[0]
    tm = min(ROW_TILE, S)
    row = lambda b, i: (b, i, 0)
    per_b = lambda b, i: (b, 0, 0)
    vec = pl.BlockSpec((1, 1, D), per_b)
    return pl.pallas_call(
        functools.partial(_mix_ffn_kernel, dff=dff),
        grid=(B, S // tm),
        in_specs=[
            pl.BlockSpec((1, tm, D), row),
            pl.BlockSpec((1, tm, D), row),
            _resident((D, D)),
            _resident((1, D)),
            vec,
            _resident((1, D)),
            vec, vec, vec,
            _resident((1, D)),
            _resident((D, dff)),
            _resident((D, dff)),
            _resident((dff, D)),
        ],
        out_specs=pl.BlockSpec((1, tm, D), row),
        out_shape=jax.ShapeDtypeStruct((B, S, D), F32),
        scratch_shapes=[pltpu.VMEM((tm, dff), BF16)],
        compiler_params=pltpu.CompilerParams(
            dimension_semantics=("arbitrary", "arbitrary"), vmem_limit_bytes=VMEM_LIMIT),
        name="mix_ffn",
    )(y, x, w_o, post_g, g1, ffn_pre_g, sh2, sc2, g2, ffn_post_g, w_gate, w_up, w_down)


def _qkv_kernel(x_ref, gq_ref, shq_ref, scq_ref, gkv_ref, shkv_ref, sckv_ref,
                wq_ref, wk_ref, wv_ref, wfc_ref, wfr_ref, bfc_ref, bfr_ref,
                q_ref, k_ref, v_ref, fcol_ref, frow_ref, ccol_ref, crow_ref, *, qscale):
    @pl.when(pl.program_id(1) == 0)
    def _():
        ccol_ref[...] = jnp.zeros_like(ccol_ref)
        crow_ref[...] = jnp.zeros_like(crow_ref)

    dot = functools.partial(jnp.dot, preferred_element_type=F32)
    xn = _rms(x_ref[0])
    hq = (xn * gq_ref[...] * (1.0 + scq_ref[0]) + shq_ref[0]).astype(BF16)
    hkv = (xn * gkv_ref[...] * (1.0 + sckv_ref[0]) + shkv_ref[0]).astype(BF16)
    q_ref[0] = (dot(hq, wq_ref[...]) * qscale).astype(BF16)
    k_ref[0] = dot(hkv, wk_ref[...]).astype(BF16)
    v_ref[0] = dot(hkv, wv_ref[...]).astype(BF16)
    tm = x_ref.shape[1]
    lower, upper = _tri(tm)
    lf_col = _log_sigmoid(dot(hkv, wfc_ref[...]) + bfc_ref[...])
    f_col = _cumsum_rows(lower.astype(BF16), lf_col) + ccol_ref[0:1, :]
    fcol_ref[0] = f_col * LOG2E
    ccol_ref[...] = jnp.broadcast_to(f_col[tm - 1:tm, :], ccol_ref.shape)
    lf_row = _log_sigmoid(
        lax.dot_general(wfr_ref[...], hkv, (((1,), (1,)), ((), ())), preferred_element_type=F32) + bfr_ref[...])
    f_row = _cumsum_lanes(lf_row, upper.astype(BF16)) + crow_ref[:, 0:1]
    frow_ref[0] = f_row * LOG2E
    crow_ref[...] = jnp.broadcast_to(f_row[:, tm - 1:tm], crow_ref.shape)


def _qkv_call(x, gq, shq, scq, gkv, shkv, sckv, wq, wk, wv, wf_col, wf_row, bf_col, bf_row, *, heads):
    B, S, D = x.shape
    tm = min(ROW_TILE, S)
    row = lambda b, i: (b, i, 0)
    per_b = lambda b, i: (b, 0, 0)
    vec = pl.BlockSpec((1, 1, D), per_b)
    return pl.pallas_call(
        functools.partial(_qkv_kernel, qscale=(D // heads) ** -0.5 * LOG2E),
        grid=(B, S // tm),
        in_specs=[
            pl.BlockSpec((1, tm, D), row),
            _resident((1, D)), vec, vec,
            _resident((1, D)), vec, vec,
            _resident((D, D)), _resident((D, D)), _resident((D, D)),
            _resident((D, LANES)), _resident((heads, D)),
            _resident((1, LANES)), _resident((heads, 1)),
        ],
        out_specs=[
            pl.BlockSpec((1, tm, D), row),
            pl.BlockSpec((1, tm, D), row),
            pl.BlockSpec((1, tm, D), row),
            pl.BlockSpec((1, tm, LANES), row),
            pl.BlockSpec((1, heads, tm), lambda b, i: (b, 0, i)),
        ],
        out_shape=[
            jax.ShapeDtypeStruct((B, S, D), BF16),
            jax.ShapeDtypeStruct((B, S, D), BF16),
            jax.ShapeDtypeStruct((B, S, D), BF16),
            jax.ShapeDtypeStruct((B, S, LANES), F32),
            jax.ShapeDtypeStruct((B, heads, S), F32),
        ],
        scratch_shapes=[pltpu.VMEM((8, LANES), F32), pltpu.VMEM((heads, LANES), F32)],
        compiler_params=pltpu.CompilerParams(
            dimension_semantics=("arbitrary", "arbitrary"), vmem_limit_bytes=VMEM_LIMIT),
        name="fox_qkv",
    )(x, gq, shq, scq, gkv, shkv, sckv, wq, wk, wv, wf_col, wf_row, bf_col, bf_row)


def _fox_kernel(q_ref, k_ref, v_ref, fcol_ref, frow_ref, o_ref, vt_ref, fkb_ref, acc_ref,
                a0_ref, a1_ref, p0_ref, p1_ref, *, hd, tq, tk):
    hp = pl.program_id(1)
    i = pl.program_id(2)
    S = k_ref.shape[1]
    lane = lax.broadcasted_iota(jnp.int32, (1, LANES), 1)
    a_ref = (a0_ref, a1_ref)
    p_ref = (p0_ref, p1_ref)

    @pl.when(i == 0)
    def _():
        for c in range(S // tk):
            rows = slice(c * tk, (c + 1) * tk)
            vT = v_ref[0, rows, :].astype(F32).T.astype(BF16)
            ones_row = (lax.broadcasted_iota(jnp.int32, (PV_EXTRA_ROWS, tk), 0) == 0).astype(BF16)
            fc = fcol_ref[0, rows, :]
            for h in range(2):
                vt_ref[h, 0:hd, rows] = vT[h * hd:(h + 1) * hd]
                vt_ref[h, hd:hd + PV_EXTRA_ROWS, rows] = ones_row
                col = jnp.sum(jnp.where(lane == 2 * hp + h, fc, 0.0), axis=1, keepdims=True)
                fkb_ref[h, rows, :] = jnp.broadcast_to(col, (tk, LANES))

    qT = q_ref[0].astype(F32).T
    sub = lax.broadcasted_iota(jnp.int32, (LANES, 1), 0)
    qstart = pl.multiple_of(i * tq, tq)
    qmT, fq = [], []
    for h in range(2):
        qmT.append(jnp.where((sub < hd) == (h == 0), qT, 0.0).astype(BF16))
        fq.append(frow_ref[0, 0, h:h + 1, pl.ds(qstart, tq)])
        acc_ref[h] = jnp.zeros(acc_ref.shape[1:], F32)

    def qk(j, h):
        start = pl.multiple_of(j * tk, tk)
        fk = fkb_ref[h, pl.ds(start, tk), :]
        a_ref[h][...] = (jnp.dot(k_ref[0, pl.ds(start, tk), :], qmT[h], preferred_element_type=F32)
                         - jnp.concatenate([fk] * (tq // LANES), axis=1))

    def softmax(h, m_prev, diagonal):
        a = a_ref[h][...]
        if diagonal:
            r = lax.broadcasted_iota(jnp.int32, (tk, tq), 0)
            c = lax.broadcasted_iota(jnp.int32, (tk, tq), 1)
            a = jnp.where(r <= c, a, -jnp.inf)
        m_new = jnp.maximum(m_prev, jnp.max(a, axis=0, keepdims=True) + fq[h])
        alpha = jnp.exp2(m_prev - m_new)
        p_ref[h][...] = jnp.exp2(a - (m_new - fq[h])).astype(BF16)
        return m_new, alpha

    def pv(j, h, alpha):
        start = pl.multiple_of(j * tk, tk)
        v2t = vt_ref[h, :, pl.ds(start, tk)]
        acc_ref[h] = alpha * acc_ref[h] + jnp.dot(v2t, p_ref[h][...], preferred_element_type=F32)

    def body(j, carry, diagonal):
        m0, m1, alpha1 = carry
        pv(jnp.maximum(j - 1, 0), 1, alpha1)
        m0, alpha0 = softmax(0, m0, diagonal)
        qk(j, 1)
        pv(j, 0, alpha0)
        m1, alpha1 = softmax(1, m1, diagonal)
        if not diagonal:
            qk(j + 1, 0)
        return m0, m1, alpha1

    qk(0, 0)
    p1_ref[...] = jnp.zeros(p1_ref.shape, BF16)
    neg = jnp.full((1, tq), -jnp.inf, F32)
    carry = lax.fori_loop(0, i, lambda j, c: body(j, c, False), (neg, neg, jnp.ones((1, tq), F32)))
    _, _, alpha1 = body(i, carry, True)
    pv(i, 1, alpha1)
    ot = jnp.concatenate([acc_ref[h, 0:hd, :] / acc_ref[h, hd:hd + 1, :] for h in range(2)], axis=0)
    o_ref[0] = ot.T.astype(BF16)


def _fox_call(q, k, v, fcol, frow, *, heads):
    B, S, D = q.shape
    hd = D // heads
    assert 2 * hd == LANES
    t = min(ATTN_TILE, S)
    frow = frow.reshape(B, heads // 2, 2, S)
    return pl.pallas_call(
        functools.partial(_fox_kernel, hd=hd, tq=t, tk=t),
        grid=(B, heads // 2, S // t),
        in_specs=[
            pl.BlockSpec((1, t, LANES), lambda b, p, i: (b, i, p)),
            pl.BlockSpec((1, S, LANES), lambda b, p, i: (b, 0, p)),
            pl.BlockSpec((1, S, LANES), lambda b, p, i: (b, 0, p)),
            pl.BlockSpec((1, S, LANES), lambda b, p, i: (b, 0, 0)),
            pl.BlockSpec((1, 1, 2, S), lambda b, p, i: (b, p, 0, 0)),
        ],
        out_specs=pl.BlockSpec((1, t, LANES), lambda b, p, i: (b, i, p)),
        out_shape=jax.ShapeDtypeStruct((B, S, D), BF16),
        scratch_shapes=[
            pltpu.VMEM((2, hd + PV_EXTRA_ROWS, S), BF16),
            pltpu.VMEM((2, S, LANES), F32),
            pltpu.VMEM((2, hd + PV_EXTRA_ROWS, t), F32),
            pltpu.VMEM((t, t), F32), pltpu.VMEM((t, t), F32),
            pltpu.VMEM((t, t), BF16), pltpu.VMEM((t, t), BF16),
        ],
        compiler_params=pltpu.CompilerParams(
            dimension_semantics=("arbitrary", "arbitrary", "arbitrary"), vmem_limit_bytes=VMEM_LIMIT),
        name="fox_attention",
    )(q, k, v, fcol, frow)


def _pad_cols(w, n):
    return jnp.pad(w, ((0, 0), (0, n - w.shape[1])))


def kernel(x, c, mod_w, mod_b, mix_pre_g, mix_post_g, ffn_pre_g, ffn_post_g, ffn_w_in, ffn_w_out,
           a_w_in, a_gate_b, a_hnorm_g, a_w_out, kv_norm_g, kv_mod_w, kv_mod_b, kv_w, kv_fgate_b,
           b_w_q, b_w_out):
    B, S, D = x.shape
    depth = mod_w.shape[0]
    n_a = a_w_in.shape[0]
    m_heads = a_gate_b.shape[1] // 2
    f_heads = kv_fgate_b.shape[0]
    dff = ffn_w_out.shape[1]
    dq = (a_w_in.shape[2] - 2 * D - 2 * m_heads) // 2

    mods = _mod_call(c, mod_w, mod_b)
    kv_mods = _mod_call(c, kv_mod_w[None], kv_mod_b[None])[0]
    vecs = lambda m, n: [v.reshape(B, 1, D) for v in jnp.split(m, n, axis=-1)]
    row = lambda g: g.reshape(1, D)

    k_sh = v_sh = fcol = frow = None
    for l in range(depth):
        sh1, sc1, g1, sh2, sc2, g2 = vecs(mods[l], 6)
        if l < n_a:
            w = a_w_in[l]
            nmain = 2 * dq + 2 * D
            wg = w[:, nmain:]
            w_ko = jnp.concatenate([w[:, dq:2 * dq], w[:, 2 * dq + D:nmain]], axis=1)
            w_qv = jnp.concatenate([w[:, :dq], w[:, 2 * dq:2 * dq + D]], axis=1)
            qt, k, vt, o, gcol, grow = _inproj_call(
                x, sh1, sc1, row(mix_pre_g[l]),
                w_ko.astype(BF16), w_qv.T.astype(BF16), _pad_cols(wg, LANES).astype(BF16),
                _pad_cols(wg, 16).T.astype(BF16),
                _pad_cols(a_gate_b[l][None, :], LANES), _pad_cols(a_gate_b[l][None, :], 16).T,
                dq=dq, dm=D, heads=m_heads)
            y = _mlstm_call(qt, k, vt, o, gcol, grow, row(a_hnorm_g[l]), heads=m_heads)
            w_o = a_w_out[l]
        else:
            if l == n_a:
                kv_sh, kv_sc = vecs(kv_mods, 2)
                wf = kv_w[:, 2 * D:]
                shared = dict(
                    gkv=row(kv_norm_g), shkv=kv_sh, sckv=kv_sc,
                    wk=kv_w[:, :D].astype(BF16), wv=kv_w[:, D:2 * D].astype(BF16),
                    wf_col=_pad_cols(wf, LANES).astype(BF16), wf_row=wf.T.astype(BF16),
                    bf_col=_pad_cols(kv_fgate_b[None, :], LANES), bf_row=kv_fgate_b[:, None])
            q, k_new, v_new, fcol_new, frow_new = _qkv_call(
                x, row(mix_pre_g[l]), sh1, sc1, shared["gkv"], shared["shkv"], shared["sckv"],
                b_w_q[l - n_a].astype(BF16), shared["wk"], shared["wv"],
                shared["wf_col"], shared["wf_row"], shared["bf_col"], shared["bf_row"], heads=f_heads)
            if l == n_a:
                k_sh, v_sh, fcol, frow = k_new, v_new, fcol_new, frow_new
            y = _fox_call(q, k_sh, v_sh, fcol, frow, heads=f_heads)
            w_o = b_w_out[l - n_a]
        w_in = ffn_w_in[l]
        x = _mix_ffn_call(
            y, x, w_o.astype(BF16), row(mix_post_g[l]), g1, row(ffn_pre_g[l]), sh2, sc2, g2,
            row(ffn_post_g[l]), w_in[:, :dff].astype(BF16), w_in[:, dff:].astype(BF16),
            ffn_w_out[l].astype(BF16))
    return x
```

```python
import functools

import jax
import jax.numpy as jnp
from jax import lax
from jax.experimental import pallas as pl
from jax.experimental.pallas import tpu as pltpu

F32 = jnp.float32
BF16 = jnp.bfloat16

EPS = 1e-6
LOG2E = 1.4426950408889634
GATE_CAP = 15.0
LANES = 128
MLSTM_CHUNK = 256
ROW_TILE = 512
ATTN_TILE = 512
FFN_COL_CHUNK = 256
PV_EXTRA_ROWS = 16
MLSTM_EXTRA_ROWS = 16
VMEM_LIMIT = 56 * 1024 * 1024


def _rms(x):
    return x * lax.rsqrt(jnp.mean(x * x, axis=-1, keepdims=True) + EPS)


def _log_sigmoid(x):
    return jnp.minimum(x, 0.0) - jnp.log(1.0 + jnp.exp(-jnp.abs(x)))


def _softcap(a):
    return GATE_CAP * jnp.tanh(a / GATE_CAP)


def _split3(x):
    hi = x.astype(BF16)
    r1 = x - hi.astype(F32)
    mid = r1.astype(BF16)
    lo = (r1 - mid.astype(F32)).astype(BF16)
    return hi, mid, lo


def _cumsum_rows(tri_lower, x):
    return sum(jnp.dot(tri_lower, p, preferred_element_type=F32) for p in _split3(x))


def _cumsum_lanes(x, tri_upper):
    return sum(jnp.dot(p, tri_upper, preferred_element_type=F32) for p in _split3(x))


def _tri(n):
    r = lax.broadcasted_iota(jnp.int32, (n, n), 0)
    c = lax.broadcasted_iota(jnp.int32, (n, n), 1)
    return r >= c, (r <= c)


def _resident(shape):
    nd = len(shape)
    return pl.BlockSpec(shape, lambda *_: (0,) * nd, pipeline_mode=pl.Buffered(1))


def _mod_kernel(c_ref, w_ref, b_ref, o_ref):
    c = c_ref[...]
    cs = (c * jax.nn.sigmoid(c)).astype(BF16)
    o_ref[0] = jnp.dot(cs, w_ref[0].astype(BF16), preferred_element_type=F32) + b_ref[0]


def _mod_call(c, w, b):
    G, D, N = w.shape
    B = c.shape[0]
    tn = 1024
    return pl.pallas_call(
        _mod_kernel,
        grid=(G, N // tn),
        in_specs=[
            pl.BlockSpec((B, D), lambda g, j: (0, 0)),
            pl.BlockSpec((1, D, tn), lambda g, j: (g, 0, j)),
            pl.BlockSpec((1, 1, tn), lambda g, j: (g, 0, j)),
        ],
        out_specs=pl.BlockSpec((1, B, tn), lambda g, j: (g, 0, j)),
        out_shape=jax.ShapeDtypeStruct((G, B, N), F32),
        name="adaln_mod",
    )(c, w, b.reshape(G, 1, N))


def _inproj_kernel(x_ref, sh_ref, sc_ref, g_ref, wko_ref, wqvt_ref, wgc_ref, bc_ref,
                   qt_ref, k_ref, vt_ref, o_ref, gcol_ref, grow_ref, *, dq, dm, heads, qscale):
    x = x_ref[0]
    tm = x.shape[0]
    h = (_rms(x) * g_ref[...] * (1.0 + sc_ref[0]) + sh_ref[0]).astype(BF16)
    nt = (((1,), (1,)), ((), ()))
    ko = jnp.dot(h, wko_ref[...], preferred_element_type=F32)
    k_ref[0] = ko[:, 0:dq].astype(BF16)
    o_ref[0] = ko[:, dq:dq + dm].astype(BF16)
    qvt = lax.dot_general(wqvt_ref[...], h, nt, preferred_element_type=F32)
    qt_ref[0] = (qvt[0:dq] * qscale).astype(BF16)
    dv = dm // heads
    dvx = dv + MLSTM_EXTRA_ROWS
    ones_row = (lax.broadcasted_iota(jnp.int32, (MLSTM_EXTRA_ROWS, tm), 0) == 0).astype(BF16)
    for hh in range(heads):
        vt_ref[0, hh * dvx:hh * dvx + dv, :] = qvt[dq + hh * dv:dq + (hh + 1) * dv].astype(BF16)
        vt_ref[0, hh * dvx + dv:(hh + 1) * dvx, :] = ones_row
    gc = _softcap(jnp.dot(h, wgc_ref[...], preferred_element_type=F32) + bc_ref[...])
    lane = lax.broadcasted_iota(jnp.int32, gc.shape, 1)
    g2 = jnp.where(lane < heads, gc, _log_sigmoid(gc)) * LOG2E
    gcol_ref[0] = g2
    grow_ref[0] = g2.T[0:grow_ref.shape[1]]


def _inproj_call(x, sh, sc, g, w_ko, w_qv_t, wg_col, b_col, *, dq, dm, heads):
    B, S, D = x.shape
    tm = min(ROW_TILE, S)
    dvx_all = dm + heads * MLSTM_EXTRA_ROWS
    row = lambda b, i: (b, i, 0)
    col = lambda b, i: (b, 0, i)
    per_b = lambda b, i: (b, 0, 0)
    kern = functools.partial(_inproj_kernel, dq=dq, dm=dm, heads=heads, qscale=(dq // heads) ** -0.5)
    return pl.pallas_call(
        kern,
        grid=(B, S // tm),
        in_specs=[
            pl.BlockSpec((1, tm, D), row),
            pl.BlockSpec((1, 1, D), per_b),
            pl.BlockSpec((1, 1, D), per_b),
            _resident((1, D)),
            _resident((D, dq + dm)),
            _resident((dq + dm, D)),
            _resident((D, LANES)),
            _resident((1, LANES)),
        ],
        out_specs=[
            pl.BlockSpec((1, dq, tm), col),
            pl.BlockSpec((1, tm, dq), row),
            pl.BlockSpec((1, dvx_all, tm), col),
            pl.BlockSpec((1, tm, dm), row),
            pl.BlockSpec((1, tm, LANES), row),
            pl.BlockSpec((1, 16, tm), col),
        ],
        out_shape=[
            jax.ShapeDtypeStruct((B, dq, S), BF16),
            jax.ShapeDtypeStruct((B, S, dq), BF16),
            jax.ShapeDtypeStruct((B, dvx_all, S), BF16),
            jax.ShapeDtypeStruct((B, S, dm), BF16),
            jax.ShapeDtypeStruct((B, S, LANES), F32),
            jax.ShapeDtypeStruct((B, 16, S), F32),
        ],
        compiler_params=pltpu.CompilerParams(
            dimension_semantics=("arbitrary", "arbitrary"), vmem_limit_bytes=VMEM_LIMIT),
        name="mlstm_inproj",
    )(x, sh, sc, g, w_ko, w_qv_t, wg_col, b_col)


def _mlstm_kernel(qt_ref, k_ref, vt_ref, o_ref, gcol_ref, grow_ref, hn_ref, y_ref,
                  st_ref, m_ref, *, heads, dk, dv, L):
    @pl.when(pl.program_id(1) == 0)
    def _():
        st_ref[...] = jnp.zeros_like(st_ref)
        m_ref[...] = jnp.zeros_like(m_ref)

    gcol = gcol_ref[0]
    grow = grow_ref[0]
    lower, upper = _tri(L)
    bcol = _cumsum_rows(lower.astype(BF16), gcol)
    brow = _cumsum_lanes(grow, upper.astype(BF16))
    dvx = dv + MLSTM_EXTRA_ROWS
    dot = functools.partial(jnp.dot, preferred_element_type=F32)

    hs = range(heads)
    i_row = [grow[h:h + 1, :] for h in hs]
    b_row = [brow[heads + h:heads + h + 1, :] for h in hs]
    i_col = [gcol[:, h:h + 1] for h in hs]
    b_col = [bcol[:, heads + h:heads + h + 1] for h in hs]
    bl = [b[:, L - 1:L] for b in b_row]
    m = [m_ref[h][0:1, 0:1] for h in hs]
    qt = [qt_ref[0, h * dk:(h + 1) * dk, :] for h in hs]
    kh = [k_ref[0, :, h * dk:(h + 1) * dk] for h in hs]
    vt = [vt_ref[0, h * dvx:(h + 1) * dvx, :] for h in hs]
    st = [st_ref[h] for h in hs]

    qk = [dot(kh[h], qt[h]) for h in hs]
    sq = [dot(st[h].astype(BF16), qt[h]) for h in hs]
    m_t, a, w = [], [], []
    for h in hs:
        dmat = jnp.where(upper, b_row[h] + (i_col[h] - b_col[h]), -jnp.inf)
        inter = b_row[h] + m[h]
        m_t.append(jnp.maximum(inter, jnp.max(dmat, axis=0, keepdims=True)))
        a.append(jnp.exp2(inter - m_t[h]))
        w.append(jnp.exp2(dmat - m_t[h]))
    na = [a[h] * sq[h] + dot(vt[h], (qk[h] * w[h]).astype(BF16)) for h in hs]
    for h in hs:
        ws = bl[h] - b_col[h] + i_col[h]
        m_new = jnp.maximum(bl[h] + m[h], jnp.max(ws, axis=0, keepdims=True))
        kw = (kh[h].astype(F32) * jnp.exp2(ws - m_new)).astype(BF16)
        st_ref[h] = jnp.exp2(bl[h] + m[h] - m_new) * st[h] + dot(vt[h], kw)
        m_ref[h] = jnp.broadcast_to(m_new, m_ref.shape[1:])

        hT = na[h][0:dv] * (1.0 / jnp.maximum(jnp.abs(na[h][dv:dv + 1]), jnp.exp2(-m_t[h])))
        hnT = hT * lax.rsqrt(jnp.mean(hT * hT, axis=0, keepdims=True) + EPS)
        og = o_ref[0, :, h * dv:(h + 1) * dv].astype(F32)
        y_ref[0, :, h * dv:(h + 1) * dv] = (jax.nn.sigmoid(og) * (hnT.T * hn_ref[:, h * dv:(h + 1) * dv])).astype(BF16)


def _mlstm_call(qt, k, vt, o, gcol, grow, hn_g, *, heads):
    B, S, dq = k.shape
    dm = o.shape[-1]
    dk, dv = dq // heads, dm // heads
    dvx_all = vt.shape[1]
    L = min(MLSTM_CHUNK, S)
    row = lambda b, c: (b, c, 0)
    col = lambda b, c: (b, 0, c)
    kern = functools.partial(_mlstm_kernel, heads=heads, dk=dk, dv=dv, L=L)
    return pl.pallas_call(
        kern,
        grid=(B, S // L),
        in_specs=[
            pl.BlockSpec((1, dq, L), col),
            pl.BlockSpec((1, L, dq), row),
            pl.BlockSpec((1, dvx_all, L), col),
            pl.BlockSpec((1, L, dm), row),
            pl.BlockSpec((1, L, LANES), row),
            pl.BlockSpec((1, 16, L), col),
            _resident((1, dm)),
        ],
        out_specs=pl.BlockSpec((1, L, dm), row),
        out_shape=jax.ShapeDtypeStruct((B, S, dm), BF16),
        scratch_shapes=[
            pltpu.VMEM((heads, dv + MLSTM_EXTRA_ROWS, dk), F32),
            pltpu.VMEM((heads, 8, LANES), F32),
        ],
        compiler_params=pltpu.CompilerParams(
            dimension_semantics=("arbitrary", "arbitrary"), vmem_limit_bytes=VMEM_LIMIT),
        name="mlstm_recurrence",
    )(qt, k, vt, o, gcol, grow, hn_g)


def _mix_ffn_kernel(y_ref, x_ref, wo_ref, pg_ref, g1_ref, fg_ref, sh2_ref, sc2_ref, g2_ref, fpg_ref,
                    wg_ref, wu_ref, wd_ref, out_ref, act_ref, *, dff):
    dot = functools.partial(jnp.dot, preferred_element_type=F32)
    tm = x_ref.shape[1]
    cw = FFN_COL_CHUNK
    nchunks = dff // cw
    rows = [slice(r * (tm // 2), (r + 1) * (tm // 2)) for r in range(2)]

    def mix(r):
        t = dot(y_ref[0, rows[r], :], wo_ref[...])
        x1 = x_ref[0, rows[r], :] + g1_ref[0] * (_rms(t) * pg_ref[...])
        return x1, (_rms(x1) * fg_ref[...] * (1.0 + sc2_ref[0]) + sh2_ref[0]).astype(BF16)

    def up(r, h2, j):
        g = dot(h2, wg_ref[:, j * cw:(j + 1) * cw])
        u = dot(h2, wu_ref[:, j * cw:(j + 1) * cw])
        act_ref[rows[r], j * cw:(j + 1) * cw] = (g * jax.nn.sigmoid(g) * u).astype(BF16)

    def down(r, x1):
        f = dot(act_ref[rows[r], :], wd_ref[...])
        out_ref[0, rows[r], :] = x1 + g2_ref[0] * (_rms(f) * fpg_ref[...])

    x1a, h2a = mix(0)
    up(0, h2a, 0)
    x1b, h2b = mix(1)
    for j in range(1, nchunks):
        up(0, h2a, j)
    for j in range(nchunks):
        up(1, h2b, j)
        if j == nchunks // 2:
            down(0, x1a)
    down(1, x1b)


def _mix_ffn_call(y, x, w_o, post_g, g1, ffn_pre_g, sh2, sc2, g2, ffn_post_g, w_gate, w_up, w_down):
    B, S, D = x.shape
    dff = w_down.shape[0]
    tm = min(ROW_TILE, S)
    row = lambda b, i: (b, i, 0)
    per_b = lambda b, i: (b, 0, 0)
    vec = pl.BlockSpec((1, 1, D), per_b)
    return pl.pallas_call(
        functools.partial(_mix_ffn_kernel, dff=dff),
        grid=(B, S // tm),
        in_specs=[
            pl.BlockSpec((1, tm, D), row),
            pl.BlockSpec((1, tm, D), row),
            _resident((D, D)),
            _resident((1, D)),
            vec,
            _resident((1, D)),
            vec, vec, vec,
            _resident((1, D)),
            _resident((D, dff)),
            _resident((D, dff)),
            _resident((dff, D)),
        ],
        out_specs=pl.BlockSpec((1, tm, D), row),
        out_shape=jax.ShapeDtypeStruct((B, S, D), F32),
        scratch_shapes=[pltpu.VMEM((tm, dff), BF16)],
        compiler_params=pltpu.CompilerParams(
            dimension_semantics=("arbitrary", "arbitrary"), vmem_limit_bytes=VMEM_LIMIT),
        name="mix_ffn",
    )(y, x, w_o, post_g, g1, ffn_pre_g, sh2, sc2, g2, ffn_post_g, w_gate, w_up, w_down)


def _qkv_kernel(x_ref, gq_ref, shq_ref, scq_ref, gkv_ref, shkv_ref, sckv_ref,
                wq_ref, wk_ref, wv_ref, wfc_ref, bfc_ref,
                q_ref, k_ref, v_ref, fcol_ref, frow_ref, ccol_ref, *, qscale):
    @pl.when(pl.program_id(1) == 0)
    def _():
        ccol_ref[...] = jnp.zeros_like(ccol_ref)

    dot = functools.partial(jnp.dot, preferred_element_type=F32)
    xn = _rms(x_ref[0])
    hq = (xn * gq_ref[...] * (1.0 + scq_ref[0]) + shq_ref[0]).astype(BF16)
    hkv = (xn * gkv_ref[...] * (1.0 + sckv_ref[0]) + shkv_ref[0]).astype(BF16)
    q_ref[0] = (dot(hq, wq_ref[...]) * qscale).astype(BF16)
    k_ref[0] = dot(hkv, wk_ref[...]).astype(BF16)
    v_ref[0] = dot(hkv, wv_ref[...]).astype(BF16)
    tm = x_ref.shape[1]
    lower, _ = _tri(tm)
    lf_col = _log_sigmoid(dot(hkv, wfc_ref[...]) + bfc_ref[...])
    f_col = _cumsum_rows(lower.astype(BF16), lf_col) + ccol_ref[0:1, :]
    ccol_ref[...] = jnp.broadcast_to(f_col[tm - 1:tm, :], ccol_ref.shape)
    f2 = f_col * LOG2E
    fcol_ref[0] = f2
    frow_ref[0] = f2.T[0:frow_ref.shape[1]]


def _qkv_call(x, gq, shq, scq, gkv, shkv, sckv, wq, wk, wv, wf_col, bf_col, *, heads):
    B, S, D = x.shape
    tm = min(ROW_TILE, S)
    row = lambda b, i: (b, i, 0)
    per_b = lambda b, i: (b, 0, 0)
    vec = pl.BlockSpec((1, 1, D), per_b)
    return pl.pallas_call(
        functools.partial(_qkv_kernel, qscale=(D // heads) ** -0.5 * LOG2E),
        grid=(B, S // tm),
        in_specs=[
            pl.BlockSpec((1, tm, D), row),
            _resident((1, D)), vec, vec,
            _resident((1, D)), vec, vec,
            _resident((D, D)), _resident((D, D)), _resident((D, D)),
            _resident((D, LANES)), _resident((1, LANES)),
        ],
        out_specs=[
            pl.BlockSpec((1, tm, D), row),
            pl.BlockSpec((1, tm, D), row),
            pl.BlockSpec((1, tm, D), row),
            pl.BlockSpec((1, tm, LANES), row),
            pl.BlockSpec((1, heads, tm), lambda b, i: (b, 0, i)),
        ],
        out_shape=[
            jax.ShapeDtypeStruct((B, S, D), BF16),
            jax.ShapeDtypeStruct((B, S, D), BF16),
            jax.ShapeDtypeStruct((B, S, D), BF16),
            jax.ShapeDtypeStruct((B, S, LANES), F32),
            jax.ShapeDtypeStruct((B, heads, S), F32),
        ],
        scratch_shapes=[pltpu.VMEM((8, LANES), F32)],
        compiler_params=pltpu.CompilerParams(
            dimension_semantics=("arbitrary", "arbitrary"), vmem_limit_bytes=VMEM_LIMIT),
        name="fox_qkv",
    )(x, gq, shq, scq, gkv, shkv, sckv, wq, wk, wv, wf_col, bf_col)


def _fox_kernel(q_ref, k_ref, v_ref, fcol_ref, frow_ref, o_ref, vt_ref, fkb_ref, acc_ref,
                a0_ref, a1_ref, p0_ref, p1_ref, *, hd, tq, tk):
    hp = pl.program_id(1)
    i = pl.program_id(2)
    S = k_ref.shape[1]
    lane = lax.broadcasted_iota(jnp.int32, (1, LANES), 1)
    a_ref = (a0_ref, a1_ref)
    p_ref = (p0_ref, p1_ref)

    @pl.when(i == 0)
    def _():
        for c in range(S // tk):
            rows = slice(c * tk, (c + 1) * tk)
            vT = v_ref[0, rows, :].astype(F32).T.astype(BF16)
            ones_row = (lax.broadcasted_iota(jnp.int32, (PV_EXTRA_ROWS, tk), 0) == 0).astype(BF16)
            fc = fcol_ref[0, rows, :]
            for h in range(2):
                vt_ref[h, 0:hd, rows] = vT[h * hd:(h + 1) * hd]
                vt_ref[h, hd:hd + PV_EXTRA_ROWS, rows] = ones_row
                col = jnp.sum(jnp.where(lane == 2 * hp + h, fc, 0.0), axis=1, keepdims=True)
                fkb_ref[h, rows, :] = jnp.broadcast_to(col, (tk, LANES))

    qT = q_ref[0].astype(F32).T
    sub = lax.broadcasted_iota(jnp.int32, (LANES, 1), 0)
    qstart = pl.multiple_of(i * tq, tq)
    qmT, fq = [], []
    for h in range(2):
        qmT.append(jnp.where((sub < hd) == (h == 0), qT, 0.0).astype(BF16))
        fq.append(frow_ref[0, 0, h:h + 1, pl.ds(qstart, tq)])
        acc_ref[h] = jnp.zeros(acc_ref.shape[1:], F32)

    def qk(j, h):
        start = pl.multiple_of(j * tk, tk)
        fk = fkb_ref[h, pl.ds(start, tk), :]
        a_ref[h][:, 0:tq] = (jnp.dot(k_ref[0, pl.ds(start, tk), :], qmT[h], preferred_element_type=F32)
                         - jnp.concatenate([fk] * (tq // LANES), axis=1))

    def softmax(h, m_prev, diagonal):
        a = a_ref[h][:, 0:tq]
        if diagonal:
            r = lax.broadcasted_iota(jnp.int32, (tk, tq), 0)
            c = lax.broadcasted_iota(jnp.int32, (tk, tq), 1)
            a = jnp.where(r <= c, a, -jnp.inf)
        m_new = jnp.maximum(m_prev, jnp.max(a, axis=0, keepdims=True) + fq[h])
        alpha = jnp.exp2(m_prev - m_new)
        p_ref[h][:, 0:tq] = jnp.exp2(a - (m_new - fq[h])).astype(BF16)
        return m_new, alpha

    def pv(j, h, alpha):
        start = pl.multiple_of(j * tk, tk)
        v2t = vt_ref[h, :, pl.ds(start, tk)]
        acc_ref[h] = alpha * acc_ref[h] + jnp.dot(v2t, p_ref[h][:, 0:tq], preferred_element_type=F32)

    def body(j, carry, diagonal):
        m0, m1, alpha1 = carry
        pv(jnp.maximum(j - 1, 0), 1, alpha1)
        m0, alpha0 = softmax(0, m0, diagonal)
        qk(j, 1)
        pv(j, 0, alpha0)
        if not diagonal:
            qk(j + 1, 0)
        m1, alpha1 = softmax(1, m1, diagonal)
        return m0, m1, alpha1

    qk(0, 0)
    p1_ref[:, 0:tq] = jnp.zeros((tk, tq), BF16)
    neg = jnp.full((1, tq), -jnp.inf, F32)
    carry = lax.fori_loop(0, i, lambda j, c: body(j, c, False), (neg, neg, jnp.ones((1, tq), F32)))
    _, _, alpha1 = body(i, carry, True)
    pv(i, 1, alpha1)
    ot = jnp.concatenate([acc_ref[h, 0:hd, :] / acc_ref[h, hd:hd + 1, :] for h in range(2)], axis=0)
    o_ref[0] = ot.T.astype(BF16)


def _fox_call(q, k, v, fcol, frow, *, heads):
    B, S, D = q.shape
    hd = D // heads
    assert 2 * hd == LANES
    t = min(ATTN_TILE, S)
    frow = frow.reshape(B, heads // 2, 2, S)
    return pl.pallas_call(
        functools.partial(_fox_kernel, hd=hd, tq=t, tk=t),
        grid=(B, heads // 2, S // t),
        in_specs=[
            pl.BlockSpec((1, t, LANES), lambda b, p, i: (b, i, p)),
            pl.BlockSpec((1, S, LANES), lambda b, p, i: (b, 0, p)),
            pl.BlockSpec((1, S, LANES), lambda b, p, i: (b, 0, p)),
            pl.BlockSpec((1, S, LANES), lambda b, p, i: (b, 0, 0)),
            pl.BlockSpec((1, 1, 2, S), lambda b, p, i: (b, p, 0, 0)),
        ],
        out_specs=pl.BlockSpec((1, t, LANES), lambda b, p, i: (b, i, p)),
        out_shape=jax.ShapeDtypeStruct((B, S, D), BF16),
        scratch_shapes=[
            pltpu.VMEM((2, hd + PV_EXTRA_ROWS, S), BF16),
            pltpu.VMEM((2, S, LANES), F32),
            pltpu.VMEM((2, hd + PV_EXTRA_ROWS, t), F32),
            pltpu.VMEM((t, t + LANES), F32), pltpu.VMEM((t, t + LANES), F32),
            pltpu.VMEM((t, t + LANES), BF16), pltpu.VMEM((t, t + LANES), BF16),
        ],
        compiler_params=pltpu.CompilerParams(
            dimension_semantics=("arbitrary", "arbitrary", "arbitrary"), vmem_limit_bytes=VMEM_LIMIT),
        name="fox_attention",
    )(q, k, v, fcol, frow)


def _pad_cols(w, n):
    return jnp.pad(w, ((0, 0), (0, n - w.shape[1])))


def kernel(x, c, mod_w, mod_b, mix_pre_g, mix_post_g, ffn_pre_g, ffn_post_g, ffn_w_in, ffn_w_out,
           a_w_in, a_gate_b, a_hnorm_g, a_w_out, kv_norm_g, kv_mod_w, kv_mod_b, kv_w, kv_fgate_b,
           b_w_q, b_w_out):
    B, S, D = x.shape
    depth = mod_w.shape[0]
    n_a = a_w_in.shape[0]
    m_heads = a_gate_b.shape[1] // 2
    f_heads = kv_fgate_b.shape[0]
    dff = ffn_w_out.shape[1]
    dq = (a_w_in.shape[2] - 2 * D - 2 * m_heads) // 2

    mods = _mod_call(c, mod_w, mod_b)
    kv_mods = _mod_call(c, kv_mod_w[None], kv_mod_b[None])[0]
    vecs = lambda m, n: [v.reshape(B, 1, D) for v in jnp.split(m, n, axis=-1)]
    row = lambda g: g.reshape(1, D)

    k_sh = v_sh = fcol = frow = None
    for l in range(depth):
        sh1, sc1, g1, sh2, sc2, g2 = vecs(mods[l], 6)
        if l < n_a:
            w = a_w_in[l]
            nmain = 2 * dq + 2 * D
            wg = w[:, nmain:]
            w_ko = jnp.concatenate([w[:, dq:2 * dq], w[:, 2 * dq + D:nmain]], axis=1)
            w_qv = jnp.concatenate([w[:, :dq], w[:, 2 * dq:2 * dq + D]], axis=1)
            qt, k, vt, o, gcol, grow = _inproj_call(
                x, sh1, sc1, row(mix_pre_g[l]),
                w_ko.astype(BF16), w_qv.T.astype(BF16), _pad_cols(wg, LANES).astype(BF16),
                _pad_cols(a_gate_b[l][None, :], LANES), dq=dq, dm=D, heads=m_heads)
            y = _mlstm_call(qt, k, vt, o, gcol, grow, row(a_hnorm_g[l]), heads=m_heads)
            w_o = a_w_out[l]
        else:
            if l == n_a:
                kv_sh, kv_sc = vecs(kv_mods, 2)
                wf = kv_w[:, 2 * D:]
                shared = dict(
                    gkv=row(kv_norm_g), shkv=kv_sh, sckv=kv_sc,
                    wk=kv_w[:, :D].astype(BF16), wv=kv_w[:, D:2 * D].astype(BF16),
                    wf_col=_pad_cols(wf, LANES).astype(BF16), bf_col=_pad_cols(kv_fgate_b[None, :], LANES))
            q, k_new, v_new, fcol_new, frow_new = _qkv_call(
                x, row(mix_pre_g[l]), sh1, sc1, shared["gkv"], shared["shkv"], shared["sckv"],
                b_w_q[l - n_a].astype(BF16), shared["wk"], shared["wv"],
                shared["wf_col"], shared["bf_col"], heads=f_heads)
            if l == n_a:
                k_sh, v_sh, fcol, frow = k_new, v_new, fcol_new, frow_new
            y = _fox_call(q, k_sh, v_sh, fcol, frow, heads=f_heads)
            w_o = b_w_out[l - n_a]
        w_in = ffn_w_in[l]
        x = _mix_ffn_call(
            y, x, w_o.astype(BF16), row(mix_post_g[l]), g1, row(ffn_pre_g[l]), sh2, sc2, g2,
            row(ffn_post_g[l]), w_in[:, :dff].astype(BF16), w_in[:, dff:].astype(BF16),
            ffn_w_out[l].astype(BF16))
    return x
```

```python
import functools

import jax
import jax.numpy as jnp
from jax import lax
from jax.experimental import pallas as pl
from jax.experimental.pallas import tpu as pltpu

F32 = jnp.float32
BF16 = jnp.bfloat16

EPS = 1e-6
LOG2E = 1.4426950408889634
GATE_CAP = 15.0
LANES = 128
MLSTM_CHUNK = 256
ROW_TILE = 512
ATTN_TILE = 512
FFN_COL_CHUNK = 256
PV_EXTRA_ROWS = 16
MLSTM_EXTRA_ROWS = 16
VMEM_LIMIT = 56 * 1024 * 1024


def _rms(x):
    return x * lax.rsqrt(jnp.mean(x * x, axis=-1, keepdims=True) + EPS)


def _log_sigmoid(x):
    return jnp.minimum(x, 0.0) - jnp.log(1.0 + jnp.exp(-jnp.abs(x)))


def _softcap(a):
    return GATE_CAP * jnp.tanh(a / GATE_CAP)


def _split3(x):
    hi = x.astype(BF16)
    r1 = x - hi.astype(F32)
    mid = r1.astype(BF16)
    lo = (r1 - mid.astype(F32)).astype(BF16)
    return hi, mid, lo


def _cumsum_rows(tri_lower, x):
    return sum(jnp.dot(tri_lower, p, preferred_element_type=F32) for p in _split3(x))


def _cumsum_lanes(x, tri_upper):
    return sum(jnp.dot(p, tri_upper, preferred_element_type=F32) for p in _split3(x))


def _tri(n):
    r = lax.broadcasted_iota(jnp.int32, (n, n), 0)
    c = lax.broadcasted_iota(jnp.int32, (n, n), 1)
    return r >= c, (r <= c)


def _resident(shape):
    nd = len(shape)
    return pl.BlockSpec(shape, lambda *_: (0,) * nd, pipeline_mode=pl.Buffered(1))


def _mod_kernel(c_ref, w_ref, b_ref, o_ref):
    c = c_ref[...]
    cs = (c * jax.nn.sigmoid(c)).astype(BF16)
    o_ref[0] = jnp.dot(cs, w_ref[0].astype(BF16), preferred_element_type=F32) + b_ref[0]


def _mod_call(c, w, b):
    G, D, N = w.shape
    B = c.shape[0]
    tn = 1024
    return pl.pallas_call(
        _mod_kernel,
        grid=(G, N // tn),
        in_specs=[
            pl.BlockSpec((B, D), lambda g, j: (0, 0)),
            pl.BlockSpec((1, D, tn), lambda g, j: (g, 0, j)),
            pl.BlockSpec((1, 1, tn), lambda g, j: (g, 0, j)),
        ],
        out_specs=pl.BlockSpec((1, B, tn), lambda g, j: (g, 0, j)),
        out_shape=jax.ShapeDtypeStruct((G, B, N), F32),
        name="adaln_mod",
    )(c, w, b.reshape(G, 1, N))


def _inproj_kernel(x_ref, sh_ref, sc_ref, g_ref, wko_ref, wqvt_ref, wgc_ref, bc_ref,
                   qt_ref, k_ref, vt_ref, o_ref, gcol_ref, grow_ref, *, dq, dm, heads, qscale):
    x = x_ref[0]
    tm = x.shape[0]
    h = (_rms(x) * g_ref[...] * (1.0 + sc_ref[0]) + sh_ref[0]).astype(BF16)
    nt = (((1,), (1,)), ((), ()))
    ko = jnp.dot(h, wko_ref[...], preferred_element_type=F32)
    k_ref[0] = ko[:, 0:dq].astype(BF16)
    o_ref[0] = ko[:, dq:dq + dm].astype(BF16)
    qvt = lax.dot_general(wqvt_ref[...], h, nt, preferred_element_type=F32)
    qt_ref[0] = (qvt[0:dq] * qscale).astype(BF16)
    dv = dm // heads
    dvx = dv + MLSTM_EXTRA_ROWS
    ones_row = (lax.broadcasted_iota(jnp.int32, (MLSTM_EXTRA_ROWS, tm), 0) == 0).astype(BF16)
    for hh in range(heads):
        vt_ref[0, hh * dvx:hh * dvx + dv, :] = qvt[dq + hh * dv:dq + (hh + 1) * dv].astype(BF16)
        vt_ref[0, hh * dvx + dv:(hh + 1) * dvx, :] = ones_row
    gc = _softcap(jnp.dot(h, wgc_ref[...], preferred_element_type=F32) + bc_ref[...])
    lane = lax.broadcasted_iota(jnp.int32, gc.shape, 1)
    g2 = jnp.where(lane < heads, gc, _log_sigmoid(gc)) * LOG2E
    gcol_ref[0] = g2
    grow_ref[0] = g2.T[0:grow_ref.shape[1]]


def _inproj_call(x, sh, sc, g, w_ko, w_qv_t, wg_col, b_col, *, dq, dm, heads):
    B, S, D = x.shape
    tm = min(ROW_TILE, S)
    dvx_all = dm + heads * MLSTM_EXTRA_ROWS
    row = lambda b, i: (b, i, 0)
    col = lambda b, i: (b, 0, i)
    per_b = lambda b, i: (b, 0, 0)
    kern = functools.partial(_inproj_kernel, dq=dq, dm=dm, heads=heads, qscale=(dq // heads) ** -0.5)
    return pl.pallas_call(
        kern,
        grid=(B, S // tm),
        in_specs=[
            pl.BlockSpec((1, tm, D), row),
            pl.BlockSpec((1, 1, D), per_b),
            pl.BlockSpec((1, 1, D), per_b),
            _resident((1, D)),
            _resident((D, dq + dm)),
            _resident((dq + dm, D)),
            _resident((D, LANES)),
            _resident((1, LANES)),
        ],
        out_specs=[
            pl.BlockSpec((1, dq, tm), col),
            pl.BlockSpec((1, tm, dq), row),
            pl.BlockSpec((1, dvx_all, tm), col),
            pl.BlockSpec((1, tm, dm), row),
            pl.BlockSpec((1, tm, LANES), row),
            pl.BlockSpec((1, 16, tm), col),
        ],
        out_shape=[
            jax.ShapeDtypeStruct((B, dq, S), BF16),
            jax.ShapeDtypeStruct((B, S, dq), BF16),
            jax.ShapeDtypeStruct((B, dvx_all, S), BF16),
            jax.ShapeDtypeStruct((B, S, dm), BF16),
            jax.ShapeDtypeStruct((B, S, LANES), F32),
            jax.ShapeDtypeStruct((B, 16, S), F32),
        ],
        compiler_params=pltpu.CompilerParams(
            dimension_semantics=("arbitrary", "arbitrary"), vmem_limit_bytes=VMEM_LIMIT),
        name="mlstm_inproj",
    )(x, sh, sc, g, w_ko, w_qv_t, wg_col, b_col)


def _mlstm_kernel(qt_ref, k_ref, vt_ref, o_ref, gcol_ref, grow_ref, hn_ref, y_ref,
                  st_ref, m_ref, *, heads, dk, dv, L):
    @pl.when(pl.program_id(1) == 0)
    def _():
        st_ref[...] = jnp.zeros_like(st_ref)
        m_ref[...] = jnp.zeros_like(m_ref)

    gcol = gcol_ref[0]
    grow = grow_ref[0]
    lower, upper = _tri(L)
    bcol = _cumsum_rows(lower.astype(BF16), gcol)
    brow = _cumsum_lanes(grow, upper.astype(BF16))
    dvx = dv + MLSTM_EXTRA_ROWS
    dot = functools.partial(jnp.dot, preferred_element_type=F32)

    hs = range(heads)
    i_row = [grow[h:h + 1, :] for h in hs]
    b_row = [brow[heads + h:heads + h + 1, :] for h in hs]
    i_col = [gcol[:, h:h + 1] for h in hs]
    b_col = [bcol[:, heads + h:heads + h + 1] for h in hs]
    bl = [b[:, L - 1:L] for b in b_row]
    m = [m_ref[h][0:1, 0:1] for h in hs]
    qt = [qt_ref[0, h * dk:(h + 1) * dk, :] for h in hs]
    kh = [k_ref[0, :, h * dk:(h + 1) * dk] for h in hs]
    vt = [vt_ref[0, h * dvx:(h + 1) * dvx, :] for h in hs]
    st = [st_ref[h] for h in hs]

    qk = [dot(kh[h], qt[h]) for h in hs]
    sq = [dot(st[h].astype(BF16), qt[h]) for h in hs]
    m_t, a, w = [], [], []
    for h in hs:
        dmat = jnp.where(upper, b_row[h] + (i_col[h] - b_col[h]), -jnp.inf)
        inter = b_row[h] + m[h]
        m_t.append(jnp.maximum(inter, jnp.max(dmat, axis=0, keepdims=True)))
        a.append(jnp.exp2(inter - m_t[h]))
        w.append(jnp.exp2(dmat - m_t[h]))
    na = [a[h] * sq[h] + dot(vt[h], (qk[h] * w[h]).astype(BF16)) for h in hs]
    for h in hs:
        ws = bl[h] - b_col[h] + i_col[h]
        m_new = jnp.maximum(bl[h] + m[h], jnp.max(ws, axis=0, keepdims=True))
        kw = (kh[h].astype(F32) * jnp.exp2(ws - m_new)).astype(BF16)
        st_ref[h] = jnp.exp2(bl[h] + m[h] - m_new) * st[h] + dot(vt[h], kw)
        m_ref[h] = jnp.broadcast_to(m_new, m_ref.shape[1:])

        hT = na[h][0:dv] * (1.0 / jnp.maximum(jnp.abs(na[h][dv:dv + 1]), jnp.exp2(-m_t[h])))
        hnT = hT * lax.rsqrt(jnp.mean(hT * hT, axis=0, keepdims=True) + EPS)
        og = o_ref[0, :, h * dv:(h + 1) * dv].astype(F32)
        y_ref[0, :, h * dv:(h + 1) * dv] = (jax.nn.sigmoid(og) * (hnT.T * hn_ref[:, h * dv:(h + 1) * dv])).astype(BF16)


def _mlstm_call(qt, k, vt, o, gcol, grow, hn_g, *, heads):
    B, S, dq = k.shape
    dm = o.shape[-1]
    dk, dv = dq // heads, dm // heads
    dvx_all = vt.shape[1]
    L = min(MLSTM_CHUNK, S)
    row = lambda b, c: (b, c, 0)
    col = lambda b, c: (b, 0, c)
    kern = functools.partial(_mlstm_kernel, heads=heads, dk=dk, dv=dv, L=L)
    return pl.pallas_call(
        kern,
        grid=(B, S // L),
        in_specs=[
            pl.BlockSpec((1, dq, L), col),
            pl.BlockSpec((1, L, dq), row),
            pl.BlockSpec((1, dvx_all, L), col),
            pl.BlockSpec((1, L, dm), row),
            pl.BlockSpec((1, L, LANES), row),
            pl.BlockSpec((1, 16, L), col),
            _resident((1, dm)),
        ],
        out_specs=pl.BlockSpec((1, L, dm), row),
        out_shape=jax.ShapeDtypeStruct((B, S, dm), BF16),
        scratch_shapes=[
            pltpu.VMEM((heads, dv + MLSTM_EXTRA_ROWS, dk), F32),
            pltpu.VMEM((heads, 8, LANES), F32),
        ],
        compiler_params=pltpu.CompilerParams(
            dimension_semantics=("arbitrary", "arbitrary"), vmem_limit_bytes=VMEM_LIMIT),
        name="mlstm_recurrence",
    )(qt, k, vt, o, gcol, grow, hn_g)


def _mix_ffn_kernel(y_ref, x_ref, wo_ref, pg_ref, g1_ref, fg_ref, sh2_ref, sc2_ref, g2_ref, fpg_ref,
                    wg_ref, wu_ref, wd_ref, out_ref, act_ref, *, dff):
    dot = functools.partial(jnp.dot, preferred_element_type=F32)
    tm = x_ref.shape[1]
    cw = FFN_COL_CHUNK
    nchunks = dff // cw
    rows = [slice(r * (tm // 2), (r + 1) * (tm // 2)) for r in range(2)]

    def mix(r):
        t = dot(y_ref[0, rows[r], :], wo_ref[...])
        x1 = x_ref[0, rows[r], :] + g1_ref[0] * (_rms(t) * pg_ref[...])
        return x1, (_rms(x1) * fg_ref[...] * (1.0 + sc2_ref[0]) + sh2_ref[0]).astype(BF16)

    def up(r, h2, j):
        g = dot(h2, wg_ref[:, j * cw:(j + 1) * cw])
        u = dot(h2, wu_ref[:, j * cw:(j + 1) * cw])
        act_ref[rows[r], j * cw:(j + 1) * cw] = (g * jax.nn.sigmoid(g) * u).astype(BF16)

    def down(r, x1):
        f = dot(act_ref[rows[r], :], wd_ref[...])
        out_ref[0, rows[r], :] = x1 + g2_ref[0] * (_rms(f) * fpg_ref[...])

    x1a, h2a = mix(0)
    up(0, h2a, 0)
    x1b, h2b = mix(1)
    for j in range(1, nchunks):
        up(0, h2a, j)
    for j in range(nchunks):
        up(1, h2b, j)
        if j == nchunks // 2:
            down(0, x1a)
    down(1, x1b)


def _mix_ffn_call(y, x, w_o, post_g, g1, ffn_pre_g, sh2, sc2, g2, ffn_post_g, w_gate, w_up, w_down):
    B, S, D = x.shape
    dff = w_down.shape[0]
    tm = min(ROW_TILE, S)
    row = lambda b, i: (b, i, 0)
    per_b = lambda b, i: (b, 0, 0)
    vec = pl.BlockSpec((1, 1, D), per_b)
    return pl.pallas_call(
        functools.partial(_mix_ffn_kernel, dff=dff),
        grid=(B, S // tm),
        in_specs=[
            pl.BlockSpec((1, tm, D), row),
            pl.BlockSpec((1, tm, D), row),
            _resident((D, D)),
            _resident((1, D)),
            vec,
            _resident((1, D)),
            vec, vec, vec,
            _resident((1, D)),
            _resident((D, dff)),
            _resident((D, dff)),
            _resident((dff, D)),
        ],
        out_specs=pl.BlockSpec((1, tm, D), row),
        out_shape=jax.ShapeDtypeStruct((B, S, D), F32),
        scratch_shapes=[pltpu.VMEM((tm, dff), BF16)],
        compiler_params=pltpu.CompilerParams(
            dimension_semantics=("arbitrary", "arbitrary"), vmem_limit_bytes=VMEM_LIMIT),
        name="mix_ffn",
    )(y, x, w_o, post_g, g1, ffn_pre_g, sh2, sc2, g2, ffn_post_g, w_gate, w_up, w_down)


def _qkv_kernel(x_ref, gq_ref, shq_ref, scq_ref, gkv_ref, shkv_ref, sckv_ref,
                wqt_ref, wk_ref, wv_ref, wfc_ref, bfc_ref,
                qt_ref, k_ref, v_ref, fcol_ref, frow_ref, ccol_ref, *, qscale):
    @pl.when(pl.program_id(1) == 0)
    def _():
        ccol_ref[...] = jnp.zeros_like(ccol_ref)

    dot = functools.partial(jnp.dot, preferred_element_type=F32)
    xn = _rms(x_ref[0])
    hq = (xn * gq_ref[...] * (1.0 + scq_ref[0]) + shq_ref[0]).astype(BF16)
    hkv = (xn * gkv_ref[...] * (1.0 + sckv_ref[0]) + shkv_ref[0]).astype(BF16)
    qt_ref[0] = (lax.dot_general(wqt_ref[...], hq, (((1,), (1,)), ((), ())), preferred_element_type=F32)
                 * qscale).astype(BF16)
    k_ref[0] = dot(hkv, wk_ref[...]).astype(BF16)
    v_ref[0] = dot(hkv, wv_ref[...]).astype(BF16)
    tm = x_ref.shape[1]
    lower, _ = _tri(tm)
    lf_col = _log_sigmoid(dot(hkv, wfc_ref[...]) + bfc_ref[...])
    f_col = _cumsum_rows(lower.astype(BF16), lf_col) + ccol_ref[0:1, :]
    ccol_ref[...] = jnp.broadcast_to(f_col[tm - 1:tm, :], ccol_ref.shape)
    f2 = f_col * LOG2E
    fcol_ref[0] = f2
    frow_ref[0] = f2.T[0:frow_ref.shape[1]]


def _qkv_call(x, gq, shq, scq, gkv, shkv, sckv, wq_t, wk, wv, wf_col, bf_col, *, heads):
    B, S, D = x.shape
    tm = min(ROW_TILE, S)
    row = lambda b, i: (b, i, 0)
    per_b = lambda b, i: (b, 0, 0)
    vec = pl.BlockSpec((1, 1, D), per_b)
    return pl.pallas_call(
        functools.partial(_qkv_kernel, qscale=(D // heads) ** -0.5 * LOG2E),
        grid=(B, S // tm),
        in_specs=[
            pl.BlockSpec((1, tm, D), row),
            _resident((1, D)), vec, vec,
            _resident((1, D)), vec, vec,
            _resident((D, D)), _resident((D, D)), _resident((D, D)),
            _resident((D, LANES)), _resident((1, LANES)),
        ],
        out_specs=[
            pl.BlockSpec((1, D, tm), lambda b, i: (b, 0, i)),
            pl.BlockSpec((1, tm, D), row),
            pl.BlockSpec((1, tm, D), row),
            pl.BlockSpec((1, tm, LANES), row),
            pl.BlockSpec((1, heads, tm), lambda b, i: (b, 0, i)),
        ],
        out_shape=[
            jax.ShapeDtypeStruct((B, D, S), BF16),
            jax.ShapeDtypeStruct((B, S, D), BF16),
            jax.ShapeDtypeStruct((B, S, D), BF16),
            jax.ShapeDtypeStruct((B, S, LANES), F32),
            jax.ShapeDtypeStruct((B, heads, S), F32),
        ],
        scratch_shapes=[pltpu.VMEM((8, LANES), F32)],
        compiler_params=pltpu.CompilerParams(
            dimension_semantics=("arbitrary", "arbitrary"), vmem_limit_bytes=VMEM_LIMIT),
        name="fox_qkv",
    )(x, gq, shq, scq, gkv, shkv, sckv, wq_t, wk, wv, wf_col, bf_col)


def _fox_kernel(qt_ref, k_ref, v_ref, fcol_ref, frow_ref, o_ref, vt_ref, fkb_ref, acc_ref, qm_ref,
                a0_ref, a1_ref, p0_ref, p1_ref, *, hd, tq, tk):
    hp = pl.program_id(1)
    S = k_ref.shape[1]
    nq = S // tq
    lane = lax.broadcasted_iota(jnp.int32, (1, LANES), 1)
    sub = lax.broadcasted_iota(jnp.int32, (LANES, 1), 0)
    a_ref = (a0_ref, a1_ref)
    p_ref = (p0_ref, p1_ref)

    for c in range(S // tk):
        rows = slice(c * tk, (c + 1) * tk)
        vT = v_ref[0, rows, :].astype(F32).T.astype(BF16)
        ones_row = (lax.broadcasted_iota(jnp.int32, (PV_EXTRA_ROWS, tk), 0) == 0).astype(BF16)
        fc = fcol_ref[0, rows, :]
        for h in range(2):
            vt_ref[h, 0:hd, rows] = vT[h * hd:(h + 1) * hd]
            vt_ref[h, hd:hd + PV_EXTRA_ROWS, rows] = ones_row
            col = jnp.sum(jnp.where(lane == 2 * hp + h, fc, 0.0), axis=1, keepdims=True)
            fkb_ref[h, rows, :] = jnp.broadcast_to(col, (tk, LANES))

    def masked_q(i, h):
        qT = qt_ref[0, :, pl.ds(pl.multiple_of(i * tq, tq), tq)].astype(F32)
        return jnp.where((sub < hd) == (h == 0), qT, 0.0).astype(BF16)

    def qk(j, h, qm=None):
        start = pl.multiple_of(j * tk, tk)
        fk = fkb_ref[h, pl.ds(start, tk), :]
        a_ref[h][:, 0:tq] = (jnp.dot(k_ref[0, pl.ds(start, tk), :], qm_ref[h] if qm is None else qm,
                                     preferred_element_type=F32)
                             - jnp.concatenate([fk] * (tq // LANES), axis=1))

    def softmax(h, m_prev, fq, diagonal):
        a = a_ref[h][:, 0:tq]
        if diagonal:
            r = lax.broadcasted_iota(jnp.int32, (tk, tq), 0)
            c = lax.broadcasted_iota(jnp.int32, (tk, tq), 1)
            a = jnp.where(r <= c, a, -jnp.inf)
        m_new = jnp.maximum(m_prev, jnp.max(a, axis=0, keepdims=True) + fq)
        alpha = jnp.exp2(m_prev - m_new)
        p_ref[h][:, 0:tq] = jnp.exp2(a - (m_new - fq)).astype(BF16)
        return m_new, alpha

    def pv(j, h, alpha):
        start = pl.multiple_of(j * tk, tk)
        v2t = vt_ref[h, :, pl.ds(start, tk)]
        acc_ref[h] = alpha * acc_ref[h] + jnp.dot(v2t, p_ref[h][:, 0:tq], preferred_element_type=F32)

    def body(j, carry, fq, diagonal):
        m0, m1, alpha1 = carry
        pv(jnp.maximum(j - 1, 0), 1, alpha1)
        m0, alpha0 = softmax(0, m0, fq[0], diagonal)
        qk(j, 1)
        pv(j, 0, alpha0)
        if not diagonal:
            qk(j + 1, 0)
        m1, alpha1 = softmax(1, m1, fq[1], diagonal)
        return m0, m1, alpha1

    def query_block(i, _):
        qstart = pl.multiple_of(i * tq, tq)
        for h in range(2):
            qm_ref[h] = masked_q(i, h)
        fq = [frow_ref[0, 0, h:h + 1, pl.ds(qstart, tq)] for h in range(2)]
        acc_ref[...] = jnp.zeros(acc_ref.shape, F32)
        p1_ref[:, 0:tq] = jnp.zeros((tk, tq), BF16)
        neg = jnp.full((1, tq), -jnp.inf, F32)
        carry = lax.fori_loop(0, i, lambda j, c: body(j, c, fq, False), (neg, neg, jnp.ones((1, tq), F32)))
        _, _, alpha1 = body(i, carry, fq, True)
        qk(0, 0, masked_q(jnp.minimum(i + 1, nq - 1), 0))
        pv(i, 1, alpha1)
        ot = jnp.concatenate([acc_ref[h, 0:hd, :] / acc_ref[h, hd:hd + 1, :] for h in range(2)], axis=0)
        o_ref[0, pl.ds(qstart, tq), :] = ot.T.astype(BF16)
        return 0

    qk(0, 0, masked_q(0, 0))
    lax.fori_loop(0, nq, query_block, 0)


def _fox_call(qt, k, v, fcol, frow, *, heads):
    B, S, D = k.shape
    hd = D // heads
    assert 2 * hd == LANES
    t = min(ATTN_TILE, S)
    frow = frow.reshape(B, heads // 2, 2, S)
    return pl.pallas_call(
        functools.partial(_fox_kernel, hd=hd, tq=t, tk=t),
        grid=(B, heads // 2),
        in_specs=[
            pl.BlockSpec((1, LANES, S), lambda b, p: (b, p, 0)),
            pl.BlockSpec((1, S, LANES), lambda b, p: (b, 0, p)),
            pl.BlockSpec((1, S, LANES), lambda b, p: (b, 0, p)),
            pl.BlockSpec((1, S, LANES), lambda b, p: (b, 0, 0)),
            pl.BlockSpec((1, 1, 2, S), lambda b, p: (b, p, 0, 0)),
        ],
        out_specs=pl.BlockSpec((1, S, LANES), lambda b, p: (b, 0, p)),
        out_shape=jax.ShapeDtypeStruct((B, S, D), BF16),
        scratch_shapes=[
            pltpu.VMEM((2, hd + PV_EXTRA_ROWS, S), BF16),
            pltpu.VMEM((2, S, LANES), F32),
            pltpu.VMEM((2, hd + PV_EXTRA_ROWS, t), F32),
            pltpu.VMEM((2, LANES, t), BF16),
            pltpu.VMEM((t, t + LANES), F32), pltpu.VMEM((t, t + LANES), F32),
            pltpu.VMEM((t, t + LANES), BF16), pltpu.VMEM((t, t + LANES), BF16),
        ],
        compiler_params=pltpu.CompilerParams(
            dimension_semantics=("arbitrary", "arbitrary"), vmem_limit_bytes=VMEM_LIMIT),
        name="fox_attention",
    )(qt, k, v, fcol, frow)


def _pad_cols(w, n):
    return jnp.pad(w, ((0, 0), (0, n - w.shape[1])))


def kernel(x, c, mod_w, mod_b, mix_pre_g, mix_post_g, ffn_pre_g, ffn_post_g, ffn_w_in, ffn_w_out,
           a_w_in, a_gate_b, a_hnorm_g, a_w_out, kv_norm_g, kv_mod_w, kv_mod_b, kv_w, kv_fgate_b,
           b_w_q, b_w_out):
    B, S, D = x.shape
    depth = mod_w.shape[0]
    n_a = a_w_in.shape[0]
    m_heads = a_gate_b.shape[1] // 2
    f_heads = kv_fgate_b.shape[0]
    dff = ffn_w_out.shape[1]
    dq = (a_w_in.shape[2] - 2 * D - 2 * m_heads) // 2

    mods = _mod_call(c, mod_w, mod_b)
    kv_mods = _mod_call(c, kv_mod_w[None], kv_mod_b[None])[0]
    vecs = lambda m, n: [v.reshape(B, 1, D) for v in jnp.split(m, n, axis=-1)]
    row = lambda g: g.reshape(1, D)

    k_sh = v_sh = fcol = frow = None
    for l in range(depth):
        sh1, sc1, g1, sh2, sc2, g2 = vecs(mods[l], 6)
        if l < n_a:
            w = a_w_in[l]
            nmain = 2 * dq + 2 * D
            wg = w[:, nmain:]
            w_ko = jnp.concatenate([w[:, dq:2 * dq], w[:, 2 * dq + D:nmain]], axis=1)
            w_qv = jnp.concatenate([w[:, :dq], w[:, 2 * dq:2 * dq + D]], axis=1)
            qt, k, vt, o, gcol, grow = _inproj_call(
                x, sh1, sc1, row(mix_pre_g[l]),
                w_ko.astype(BF16), w_qv.T.astype(BF16), _pad_cols(wg, LANES).astype(BF16),
                _pad_cols(a_gate_b[l][None, :], LANES), dq=dq, dm=D, heads=m_heads)
            y = _mlstm_call(qt, k, vt, o, gcol, grow, row(a_hnorm_g[l]), heads=m_heads)
            w_o = a_w_out[l]
        else:
            if l == n_a:
                kv_sh, kv_sc = vecs(kv_mods, 2)
                wf = kv_w[:, 2 * D:]
                shared = dict(
                    gkv=row(kv_norm_g), shkv=kv_sh, sckv=kv_sc,
                    wk=kv_w[:, :D].astype(BF16), wv=kv_w[:, D:2 * D].astype(BF16),
                    wf_col=_pad_cols(wf, LANES).astype(BF16), bf_col=_pad_cols(kv_fgate_b[None, :], LANES))
            qt, k_new, v_new, fcol_new, frow_new = _qkv_call(
                x, row(mix_pre_g[l]), sh1, sc1, shared["gkv"], shared["shkv"], shared["sckv"],
                b_w_q[l - n_a].T.astype(BF16), shared["wk"], shared["wv"],
                shared["wf_col"], shared["bf_col"], heads=f_heads)
            if l == n_a:
                k_sh, v_sh, fcol, frow = k_new, v_new, fcol_new, frow_new
            y = _fox_call(qt, k_sh, v_sh, fcol, frow, heads=f_heads)
            w_o = b_w_out[l - n_a]
        w_in = ffn_w_in[l]
        x = _mix_ffn_call(
            y, x, w_o.astype(BF16), row(mix_post_g[l]), g1, row(ffn_pre_g[l]), sh2, sc2, g2,
            row(ffn_post_g[l]), w_in[:, :dff].astype(BF16), w_in[:, dff:].astype(BF16),
            ffn_w_out[l].astype(BF16))
    return x
```

```python
import functools

import jax
import jax.numpy as jnp
from jax import lax
from jax.experimental import pallas as pl
from jax.experimental.pallas import tpu as pltpu

F32 = jnp.float32
BF16 = jnp.bfloat16

EPS = 1e-6
LOG2E = 1.4426950408889634
GATE_CAP = 15.0
LANES = 128
MLSTM_CHUNK = 256
ROW_TILE = 512
FFN_ROW_TILE = 512
ATTN_TILE = 512
FFN_COL_CHUNK = 256
PV_EXTRA_ROWS = 16
MLSTM_EXTRA_ROWS = 16
VMEM_LIMIT = 56 * 1024 * 1024


def _rms(x):
    return x * lax.rsqrt(jnp.mean(x * x, axis=-1, keepdims=True) + EPS)


def _log_sigmoid(x):
    return jnp.minimum(x, 0.0) - jnp.log(1.0 + jnp.exp(-jnp.abs(x)))


def _softcap(a):
    return GATE_CAP * jnp.tanh(a / GATE_CAP)


def _split3(x):
    hi = x.astype(BF16)
    r1 = x - hi.astype(F32)
    mid = r1.astype(BF16)
    lo = (r1 - mid.astype(F32)).astype(BF16)
    return hi, mid, lo


def _cumsum_rows(tri_lower, x):
    return sum(jnp.dot(tri_lower, p, preferred_element_type=F32) for p in _split3(x))


def _cumsum_lanes(x, tri_upper):
    return sum(jnp.dot(p, tri_upper, preferred_element_type=F32) for p in _split3(x))


def _tri(n):
    r = lax.broadcasted_iota(jnp.int32, (n, n), 0)
    c = lax.broadcasted_iota(jnp.int32, (n, n), 1)
    return r >= c, (r <= c)


def _resident(shape):
    nd = len(shape)
    return pl.BlockSpec(shape, lambda *_: (0,) * nd, pipeline_mode=pl.Buffered(1))


def _mod_kernel(c_ref, w_ref, b_ref, o_ref):
    c = c_ref[...]
    cs = (c * jax.nn.sigmoid(c)).astype(BF16)
    o_ref[0] = jnp.dot(cs, w_ref[0].astype(BF16), preferred_element_type=F32) + b_ref[0]


def _mod_call(c, w, b):
    G, D, N = w.shape
    B = c.shape[0]
    tn = 1024
    return pl.pallas_call(
        _mod_kernel,
        grid=(G, N // tn),
        in_specs=[
            pl.BlockSpec((B, D), lambda g, j: (0, 0)),
            pl.BlockSpec((1, D, tn), lambda g, j: (g, 0, j)),
            pl.BlockSpec((1, 1, tn), lambda g, j: (g, 0, j)),
        ],
        out_specs=pl.BlockSpec((1, B, tn), lambda g, j: (g, 0, j)),
        out_shape=jax.ShapeDtypeStruct((G, B, N), F32),
        name="adaln_mod",
    )(c, w, b.reshape(G, 1, N))


def _inproj_kernel(x_ref, sh_ref, sc_ref, g_ref, wko_ref, wqvt_ref, wgc_ref, bc_ref,
                   qt_ref, k_ref, vt_ref, o_ref, gcol_ref, grow_ref, *, dq, dm, heads, qscale):
    x = x_ref[0]
    tm = x.shape[0]
    h = (_rms(x) * g_ref[...] * (1.0 + sc_ref[0]) + sh_ref[0]).astype(BF16)
    nt = (((1,), (1,)), ((), ()))
    ko = jnp.dot(h, wko_ref[...], preferred_element_type=F32)
    k_ref[0] = ko[:, 0:dq].astype(BF16)
    o_ref[0] = ko[:, dq:dq + dm].astype(BF16)
    qvt = lax.dot_general(wqvt_ref[...], h, nt, preferred_element_type=F32)
    qt_ref[0] = (qvt[0:dq] * qscale).astype(BF16)
    dv = dm // heads
    dvx = dv + MLSTM_EXTRA_ROWS
    ones_row = (lax.broadcasted_iota(jnp.int32, (MLSTM_EXTRA_ROWS, tm), 0) == 0).astype(BF16)
    for hh in range(heads):
        vt_ref[0, hh * dvx:hh * dvx + dv, :] = qvt[dq + hh * dv:dq + (hh + 1) * dv].astype(BF16)
        vt_ref[0, hh * dvx + dv:(hh + 1) * dvx, :] = ones_row
    gc = _softcap(jnp.dot(h, wgc_ref[...], preferred_element_type=F32) + bc_ref[...])
    lane = lax.broadcasted_iota(jnp.int32, gc.shape, 1)
    g2 = jnp.where(lane < heads, gc, _log_sigmoid(gc)) * LOG2E
    gcol_ref[0] = g2
    grow_ref[0] = g2.T[0:grow_ref.shape[1]]


def _inproj_call(x, sh, sc, g, w_ko, w_qv_t, wg_col, b_col, *, dq, dm, heads):
    B, S, D = x.shape
    tm = min(ROW_TILE, S)
    dvx_all = dm + heads * MLSTM_EXTRA_ROWS
    row = lambda b, i: (b, i, 0)
    col = lambda b, i: (b, 0, i)
    per_b = lambda b, i: (b, 0, 0)
    kern = functools.partial(_inproj_kernel, dq=dq, dm=dm, heads=heads, qscale=(dq // heads) ** -0.5)
    return pl.pallas_call(
        kern,
        grid=(B, S // tm),
        in_specs=[
            pl.BlockSpec((1, tm, D), row),
            pl.BlockSpec((1, 1, D), per_b),
            pl.BlockSpec((1, 1, D), per_b),
            _resident((1, D)),
            _resident((D, dq + dm)),
            _resident((dq + dm, D)),
            _resident((D, LANES)),
            _resident((1, LANES)),
        ],
        out_specs=[
            pl.BlockSpec((1, dq, tm), col),
            pl.BlockSpec((1, tm, dq), row),
            pl.BlockSpec((1, dvx_all, tm), col),
            pl.BlockSpec((1, tm, dm), row),
            pl.BlockSpec((1, tm, LANES), row),
            pl.BlockSpec((1, 16, tm), col),
        ],
        out_shape=[
            jax.ShapeDtypeStruct((B, dq, S), BF16),
            jax.ShapeDtypeStruct((B, S, dq), BF16),
            jax.ShapeDtypeStruct((B, dvx_all, S), BF16),
            jax.ShapeDtypeStruct((B, S, dm), BF16),
            jax.ShapeDtypeStruct((B, S, LANES), F32),
            jax.ShapeDtypeStruct((B, 16, S), F32),
        ],
        compiler_params=pltpu.CompilerParams(
            dimension_semantics=("arbitrary", "arbitrary"), vmem_limit_bytes=VMEM_LIMIT),
        name="mlstm_inproj",
    )(x, sh, sc, g, w_ko, w_qv_t, wg_col, b_col)


def _mlstm_kernel(qt_ref, k_ref, vt_ref, o_ref, gcol_ref, grow_ref, hn_ref, y_ref,
                  st_ref, m_ref, *, heads, dk, dv, L):
    @pl.when(pl.program_id(1) == 0)
    def _():
        st_ref[...] = jnp.zeros_like(st_ref)
        m_ref[...] = jnp.zeros_like(m_ref)

    gcol = gcol_ref[0]
    grow = grow_ref[0]
    lower, upper = _tri(L)
    bcol = _cumsum_rows(lower.astype(BF16), gcol)
    brow = _cumsum_lanes(grow, upper.astype(BF16))
    dvx = dv + MLSTM_EXTRA_ROWS
    dot = functools.partial(jnp.dot, preferred_element_type=F32)

    hs = range(heads)
    i_row = [grow[h:h + 1, :] for h in hs]
    b_row = [brow[heads + h:heads + h + 1, :] for h in hs]
    i_col = [gcol[:, h:h + 1] for h in hs]
    b_col = [bcol[:, heads + h:heads + h + 1] for h in hs]
    bl = [b[:, L - 1:L] for b in b_row]
    m = [m_ref[h][0:1, 0:1] for h in hs]
    qt = [qt_ref[0, h * dk:(h + 1) * dk, :] for h in hs]
    kh = [k_ref[0, :, h * dk:(h + 1) * dk] for h in hs]
    vt = [vt_ref[0, h * dvx:(h + 1) * dvx, :] for h in hs]
    st = [st_ref[h] for h in hs]

    qk = [dot(kh[h], qt[h]) for h in hs]
    sq = [dot(st[h].astype(BF16), qt[h]) for h in hs]
    m_t, a, w = [], [], []
    for h in hs:
        dmat = jnp.where(upper, b_row[h] + (i_col[h] - b_col[h]), -jnp.inf)
        inter = b_row[h] + m[h]
        m_t.append(jnp.maximum(inter, jnp.max(dmat, axis=0, keepdims=True)))
        a.append(jnp.exp2(inter - m_t[h]))
        w.append(jnp.exp2(dmat - m_t[h]))
    na = [a[h] * sq[h] + dot(vt[h], (qk[h] * w[h]).astype(BF16)) for h in hs]
    for h in hs:
        ws = bl[h] - b_col[h] + i_col[h]
        m_new = jnp.maximum(bl[h] + m[h], jnp.max(ws, axis=0, keepdims=True))
        kw = (kh[h].astype(F32) * jnp.exp2(ws - m_new)).astype(BF16)
        st_ref[h] = jnp.exp2(bl[h] + m[h] - m_new) * st[h] + dot(vt[h], kw)
        m_ref[h] = jnp.broadcast_to(m_new, m_ref.shape[1:])

        hT = na[h][0:dv] * (1.0 / jnp.maximum(jnp.abs(na[h][dv:dv + 1]), jnp.exp2(-m_t[h])))
        hnT = hT * lax.rsqrt(jnp.mean(hT * hT, axis=0, keepdims=True) + EPS)
        og = o_ref[0, :, h * dv:(h + 1) * dv].astype(F32)
        y_ref[0, :, h * dv:(h + 1) * dv] = (jax.nn.sigmoid(og) * (hnT.T * hn_ref[:, h * dv:(h + 1) * dv])).astype(BF16)


def _mlstm_call(qt, k, vt, o, gcol, grow, hn_g, *, heads):
    B, S, dq = k.shape
    dm = o.shape[-1]
    dk, dv = dq // heads, dm // heads
    dvx_all = vt.shape[1]
    L = min(MLSTM_CHUNK, S)
    row = lambda b, c: (b, c, 0)
    col = lambda b, c: (b, 0, c)
    kern = functools.partial(_mlstm_kernel, heads=heads, dk=dk, dv=dv, L=L)
    return pl.pallas_call(
        kern,
        grid=(B, S // L),
        in_specs=[
            pl.BlockSpec((1, dq, L), col),
            pl.BlockSpec((1, L, dq), row),
            pl.BlockSpec((1, dvx_all, L), col),
            pl.BlockSpec((1, L, dm), row),
            pl.BlockSpec((1, L, LANES), row),
            pl.BlockSpec((1, 16, L), col),
            _resident((1, dm)),
        ],
        out_specs=pl.BlockSpec((1, L, dm), row),
        out_shape=jax.ShapeDtypeStruct((B, S, dm), BF16),
        scratch_shapes=[
            pltpu.VMEM((heads, dv + MLSTM_EXTRA_ROWS, dk), F32),
            pltpu.VMEM((heads, 8, LANES), F32),
        ],
        compiler_params=pltpu.CompilerParams(
            dimension_semantics=("arbitrary", "arbitrary"), vmem_limit_bytes=VMEM_LIMIT),
        name="mlstm_recurrence",
    )(qt, k, vt, o, gcol, grow, hn_g)


def _mix_ffn_kernel(y_ref, x_ref, wo_ref, pg_ref, g1_ref, fg_ref, sh2_ref, sc2_ref, g2_ref, fpg_ref,
                    wg_ref, wu_ref, wd_ref, out_ref, act_ref, *, dff):
    dot = functools.partial(jnp.dot, preferred_element_type=F32)
    tm = x_ref.shape[1]
    cw = FFN_COL_CHUNK
    nchunks = dff // cw
    rows = [slice(r * (tm // 2), (r + 1) * (tm // 2)) for r in range(2)]

    def mix(r):
        t = dot(y_ref[0, rows[r], :], wo_ref[...])
        x1 = x_ref[0, rows[r], :] + g1_ref[0] * (_rms(t) * pg_ref[...])
        return x1, (_rms(x1) * fg_ref[...] * (1.0 + sc2_ref[0]) + sh2_ref[0]).astype(BF16)

    def up(r, h2, j):
        g = dot(h2, wg_ref[:, j * cw:(j + 1) * cw])
        u = dot(h2, wu_ref[:, j * cw:(j + 1) * cw])
        act_ref[rows[r], j * cw:(j + 1) * cw] = (g * jax.nn.sigmoid(g) * u).astype(BF16)

    def down(r, x1):
        f = dot(act_ref[rows[r], :], wd_ref[...])
        out_ref[0, rows[r], :] = x1 + g2_ref[0] * (_rms(f) * fpg_ref[...])

    x1a, h2a = mix(0)
    up(0, h2a, 0)
    x1b, h2b = mix(1)
    for j in range(1, nchunks):
        up(0, h2a, j)
    for j in range(nchunks):
        up(1, h2b, j)
        if j == nchunks // 2:
            down(0, x1a)
    down(1, x1b)


def _mix_ffn_call(y, x, w_o, post_g, g1, ffn_pre_g, sh2, sc2, g2, ffn_post_g, w_gate, w_up, w_down):
    B, S, D = x.shape
    dff = w_down.shape[0]
    tm = min(FFN_ROW_TILE, S)
    row = lambda b, i: (b, i, 0)
    per_b = lambda b, i: (b, 0, 0)
    vec = pl.BlockSpec((1, 1, D), per_b)
    return pl.pallas_call(
        functools.partial(_mix_ffn_kernel, dff=dff),
        grid=(B, S // tm),
        in_specs=[
            pl.BlockSpec((1, tm, D), row),
            pl.BlockSpec((1, tm, D), row),
            _resident((D, D)),
            _resident((1, D)),
            vec,
            _resident((1, D)),
            vec, vec, vec,
            _resident((1, D)),
            _resident((D, dff)),
            _resident((D, dff)),
            _resident((dff, D)),
        ],
        out_specs=pl.BlockSpec((1, tm, D), row),
        out_shape=jax.ShapeDtypeStruct((B, S, D), F32),
        scratch_shapes=[pltpu.VMEM((tm, dff), BF16)],
        compiler_params=pltpu.CompilerParams(
            dimension_semantics=("arbitrary", "arbitrary"), vmem_limit_bytes=VMEM_LIMIT),
        name="mix_ffn",
    )(y, x, w_o, post_g, g1, ffn_pre_g, sh2, sc2, g2, ffn_post_g, w_gate, w_up, w_down)


def _qkv_kernel(x_ref, gq_ref, shq_ref, scq_ref, gkv_ref, shkv_ref, sckv_ref,
                wqt_ref, wk_ref, wvt_ref, wfc_ref, bfc_ref,
                qt_ref, k_ref, vt_ref, fcol_ref, frow_ref, ccol_ref, *, qscale, hd):
    @pl.when(pl.program_id(1) == 0)
    def _():
        ccol_ref[...] = jnp.zeros_like(ccol_ref)

    dot = functools.partial(jnp.dot, preferred_element_type=F32)
    xn = _rms(x_ref[0])
    hq = (xn * gq_ref[...] * (1.0 + scq_ref[0]) + shq_ref[0]).astype(BF16)
    hkv = (xn * gkv_ref[...] * (1.0 + sckv_ref[0]) + shkv_ref[0]).astype(BF16)
    qt_ref[0] = (lax.dot_general(wqt_ref[...], hq, (((1,), (1,)), ((), ())), preferred_element_type=F32)
                 * qscale).astype(BF16)
    k_ref[0] = dot(hkv, wk_ref[...]).astype(BF16)
    vT = lax.dot_general(wvt_ref[...], hkv, (((1,), (1,)), ((), ())), preferred_element_type=F32)
    tm = x_ref.shape[1]
    hdx = hd + PV_EXTRA_ROWS
    ones_row = (lax.broadcasted_iota(jnp.int32, (PV_EXTRA_ROWS, tm), 0) == 0).astype(BF16)
    for h in range(vT.shape[0] // hd):
        vt_ref[0, h * hdx:h * hdx + hd, :] = vT[h * hd:(h + 1) * hd].astype(BF16)
        vt_ref[0, h * hdx + hd:(h + 1) * hdx, :] = ones_row
    lower, _ = _tri(tm)
    lf_col = _log_sigmoid(dot(hkv, wfc_ref[...]) + bfc_ref[...])
    f_col = _cumsum_rows(lower.astype(BF16), lf_col) + ccol_ref[0:1, :]
    ccol_ref[...] = jnp.broadcast_to(f_col[tm - 1:tm, :], ccol_ref.shape)
    f2 = f_col * LOG2E
    fcol_ref[0] = f2
    frow_ref[0] = f2.T[0:frow_ref.shape[1]]


def _qkv_call(x, gq, shq, scq, gkv, shkv, sckv, wq_t, wk, wv_t, wf_col, bf_col, *, heads):
    B, S, D = x.shape
    tm = min(ROW_TILE, S)
    hd = D // heads
    vt_rows = heads * (hd + PV_EXTRA_ROWS)
    row = lambda b, i: (b, i, 0)
    col = lambda b, i: (b, 0, i)
    per_b = lambda b, i: (b, 0, 0)
    vec = pl.BlockSpec((1, 1, D), per_b)
    return pl.pallas_call(
        functools.partial(_qkv_kernel, qscale=hd ** -0.5 * LOG2E, hd=hd),
        grid=(B, S // tm),
        in_specs=[
            pl.BlockSpec((1, tm, D), row),
            _resident((1, D)), vec, vec,
            _resident((1, D)), vec, vec,
            _resident((D, D)), _resident((D, D)), _resident((D, D)),
            _resident((D, LANES)), _resident((1, LANES)),
        ],
        out_specs=[
            pl.BlockSpec((1, D, tm), col),
            pl.BlockSpec((1, tm, D), row),
            pl.BlockSpec((1, vt_rows, tm), col),
            pl.BlockSpec((1, tm, LANES), row),
            pl.BlockSpec((1, heads, tm), col),
        ],
        out_shape=[
            jax.ShapeDtypeStruct((B, D, S), BF16),
            jax.ShapeDtypeStruct((B, S, D), BF16),
            jax.ShapeDtypeStruct((B, vt_rows, S), BF16),
            jax.ShapeDtypeStruct((B, S, LANES), F32),
            jax.ShapeDtypeStruct((B, heads, S), F32),
        ],
        scratch_shapes=[pltpu.VMEM((8, LANES), F32)],
        compiler_params=pltpu.CompilerParams(
            dimension_semantics=("arbitrary", "arbitrary"), vmem_limit_bytes=VMEM_LIMIT),
        name="fox_qkv",
    )(x, gq, shq, scq, gkv, shkv, sckv, wq_t, wk, wv_t, wf_col, bf_col)


def _fox_kernel(qt_ref, k_ref, vt_ref, fcol_ref, frow_ref, o_ref, fkb_ref, acc_ref, qm_ref,
                a0_ref, a1_ref, p0_ref, p1_ref, *, hd, tq, tk):
    hp = pl.program_id(1)
    S = k_ref.shape[1]
    nq = S // tq
    hdx = hd + PV_EXTRA_ROWS
    lane = lax.broadcasted_iota(jnp.int32, (1, LANES), 1)
    sub = lax.broadcasted_iota(jnp.int32, (LANES, 1), 0)
    a_ref = (a0_ref, a1_ref)
    p_ref = (p0_ref, p1_ref)

    def spread_fk(c):
        rows = pl.ds(pl.multiple_of(c * tk, tk), tk)
        fc = fcol_ref[0, rows, :]
        for h in range(2):
            col = jnp.sum(jnp.where(lane == 2 * hp + h, fc, 0.0), axis=1, keepdims=True)
            fkb_ref[h, rows, :] = jnp.broadcast_to(col, (tk, LANES))

    def masked_q(i, h):
        qT = qt_ref[0, :, pl.ds(pl.multiple_of(i * tq, tq), tq)].astype(F32)
        return jnp.where((sub < hd) == (h == 0), qT, 0.0).astype(BF16)

    def qk(j, h, qm=None):
        start = pl.multiple_of(j * tk, tk)
        fk = fkb_ref[h, pl.ds(start, tk), :]
        a_ref[h][:, 0:tq] = (jnp.dot(k_ref[0, pl.ds(start, tk), :], qm_ref[h] if qm is None else qm,
                                     preferred_element_type=F32)
                             - jnp.concatenate([fk] * (tq // LANES), axis=1))

    def softmax(h, m_prev, fq, diagonal):
        a = a_ref[h][:, 0:tq]
        if diagonal:
            r = lax.broadcasted_iota(jnp.int32, (tk, tq), 0)
            c = lax.broadcasted_iota(jnp.int32, (tk, tq), 1)
            a = jnp.where(r <= c, a, -jnp.inf)
        m_new = jnp.maximum(m_prev, jnp.max(a, axis=0, keepdims=True) + fq)
        alpha = jnp.exp2(m_prev - m_new)
        p_ref[h][:, 0:tq] = jnp.exp2(a - (m_new - fq)).astype(BF16)
        return m_new, alpha

    def pv(j, h, alpha):
        start = pl.multiple_of(j * tk, tk)
        v2t = vt_ref[0, h * hdx:(h + 1) * hdx, pl.ds(start, tk)]
        acc_ref[h] = alpha * acc_ref[h] + jnp.dot(v2t, p_ref[h][:, 0:tq], preferred_element_type=F32)

    def body(j, carry, fq, diagonal):
        m0, m1, alpha1 = carry
        pv(jnp.maximum(j - 1, 0), 1, alpha1)
        m0, alpha0 = softmax(0, m0, fq[0], diagonal)
        qk(j, 1)
        pv(j, 0, alpha0)
        if not diagonal:
            qk(j + 1, 0)
        m1, alpha1 = softmax(1, m1, fq[1], diagonal)
        return m0, m1, alpha1

    def query_block(i, _):
        qstart = pl.multiple_of(i * tq, tq)
        for h in range(2):
            qm_ref[h] = masked_q(i, h)
        fq = [frow_ref[0, 0, h:h + 1, pl.ds(qstart, tq)] for h in range(2)]
        acc_ref[...] = jnp.zeros(acc_ref.shape, F32)
        p1_ref[:, 0:tq] = jnp.zeros((tk, tq), BF16)
        neg = jnp.full((1, tq), -jnp.inf, F32)
        carry = lax.fori_loop(0, i, lambda j, c: body(j, c, fq, False), (neg, neg, jnp.ones((1, tq), F32)))
        _, _, alpha1 = body(i, carry, fq, True)
        qk(0, 0, masked_q(jnp.minimum(i + 1, nq - 1), 0))
        spread_fk(jnp.minimum(i + 1, nq - 1))
        pv(i, 1, alpha1)
        ot = jnp.concatenate([acc_ref[h, 0:hd, :] / acc_ref[h, hd:hd + 1, :] for h in range(2)], axis=0)
        o_ref[0, pl.ds(qstart, tq), :] = ot.T.astype(BF16)
        return 0

    spread_fk(0)
    qk(0, 0, masked_q(0, 0))
    lax.fori_loop(0, nq, query_block, 0)


def _fox_call(qt, k, vt, fcol, frow, *, heads):
    B, S, D = k.shape
    hd = D // heads
    assert 2 * hd == LANES
    t = min(ATTN_TILE, S)
    frow = frow.reshape(B, heads // 2, 2, S)
    return pl.pallas_call(
        functools.partial(_fox_kernel, hd=hd, tq=t, tk=t),
        grid=(B, heads // 2),
        in_specs=[
            pl.BlockSpec((1, LANES, S), lambda b, p: (b, p, 0)),
            pl.BlockSpec((1, S, LANES), lambda b, p: (b, 0, p)),
            pl.BlockSpec((1, 2 * (hd + PV_EXTRA_ROWS), S), lambda b, p: (b, p, 0)),
            pl.BlockSpec((1, S, LANES), lambda b, p: (b, 0, 0)),
            pl.BlockSpec((1, 1, 2, S), lambda b, p: (b, p, 0, 0)),
        ],
        out_specs=pl.BlockSpec((1, S, LANES), lambda b, p: (b, 0, p)),
        out_shape=jax.ShapeDtypeStruct((B, S, D), BF16),
        scratch_shapes=[
            pltpu.VMEM((2, S, LANES), F32),
            pltpu.VMEM((2, hd + PV_EXTRA_ROWS, t), F32),
            pltpu.VMEM((2, LANES, t), BF16),
            pltpu.VMEM((t, t + LANES), F32), pltpu.VMEM((t, t + LANES), F32),
            pltpu.VMEM((t, t + LANES), BF16), pltpu.VMEM((t, t + LANES), BF16),
        ],
        compiler_params=pltpu.CompilerParams(
            dimension_semantics=("arbitrary", "arbitrary"), vmem_limit_bytes=VMEM_LIMIT),
        name="fox_attention",
    )(qt, k, vt, fcol, frow)


def _pad_cols(w, n):
    return jnp.pad(w, ((0, 0), (0, n - w.shape[1])))


def kernel(x, c, mod_w, mod_b, mix_pre_g, mix_post_g, ffn_pre_g, ffn_post_g, ffn_w_in, ffn_w_out,
           a_w_in, a_gate_b, a_hnorm_g, a_w_out, kv_norm_g, kv_mod_w, kv_mod_b, kv_w, kv_fgate_b,
           b_w_q, b_w_out):
    B, S, D = x.shape
    depth = mod_w.shape[0]
    n_a = a_w_in.shape[0]
    m_heads = a_gate_b.shape[1] // 2
    f_heads = kv_fgate_b.shape[0]
    dff = ffn_w_out.shape[1]
    dq = (a_w_in.shape[2] - 2 * D - 2 * m_heads) // 2

    mods = _mod_call(c, mod_w, mod_b)
    kv_mods = _mod_call(c, kv_mod_w[None], kv_mod_b[None])[0]
    vecs = lambda m, n: [v.reshape(B, 1, D) for v in jnp.split(m, n, axis=-1)]
    row = lambda g: g.reshape(1, D)

    k_sh = v_sh = fcol = frow = None
    for l in range(depth):
        sh1, sc1, g1, sh2, sc2, g2 = vecs(mods[l], 6)
        if l < n_a:
            w = a_w_in[l]
            nmain = 2 * dq + 2 * D
            wg = w[:, nmain:]
            w_ko = jnp.concatenate([w[:, dq:2 * dq], w[:, 2 * dq + D:nmain]], axis=1)
            w_qv = jnp.concatenate([w[:, :dq], w[:, 2 * dq:2 * dq + D]], axis=1)
            qt, k, vt, o, gcol, grow = _inproj_call(
                x, sh1, sc1, row(mix_pre_g[l]),
                w_ko.astype(BF16), w_qv.T.astype(BF16), _pad_cols(wg, LANES).astype(BF16),
                _pad_cols(a_gate_b[l][None, :], LANES), dq=dq, dm=D, heads=m_heads)
            y = _mlstm_call(qt, k, vt, o, gcol, grow, row(a_hnorm_g[l]), heads=m_heads)
            w_o = a_w_out[l]
        else:
            if l == n_a:
                kv_sh, kv_sc = vecs(kv_mods, 2)
                wf = kv_w[:, 2 * D:]
                shared = dict(
                    gkv=row(kv_norm_g), shkv=kv_sh, sckv=kv_sc,
                    wk=kv_w[:, :D].astype(BF16), wv=kv_w[:, D:2 * D].T.astype(BF16),
                    wf_col=_pad_cols(wf, LANES).astype(BF16), bf_col=_pad_cols(kv_fgate_b[None, :], LANES))
            qt, k_new, v_new, fcol_new, frow_new = _qkv_call(
                x, row(mix_pre_g[l]), sh1, sc1, shared["gkv"], shared["shkv"], shared["sckv"],
                b_w_q[l - n_a].T.astype(BF16), shared["wk"], shared["wv"],
                shared["wf_col"], shared["bf_col"], heads=f_heads)
            if l == n_a:
                k_sh, v_sh, fcol, frow = k_new, v_new, fcol_new, frow_new
            y = _fox_call(qt, k_sh, v_sh, fcol, frow, heads=f_heads)
            w_o = b_w_out[l - n_a]
        w_in = ffn_w_in[l]
        x = _mix_ffn_call(
            y, x, w_o.astype(BF16), row(mix_post_g[l]), g1, row(ffn_pre_g[l]), sh2, sc2, g2,
            row(ffn_post_g[l]), w_in[:, :dff].astype(BF16), w_in[:, dff:].astype(BF16),
            ffn_w_out[l].astype(BF16))
    return x
```

```python
import functools

import jax
import jax.numpy as jnp
from jax import lax
from jax.experimental import pallas as pl
from jax.experimental.pallas import tpu as pltpu

F32 = jnp.float32
BF16 = jnp.bfloat16

EPS = 1e-6
LOG2E = 1.4426950408889634
GATE_CAP = 15.0
LANES = 128
MLSTM_CHUNK = 256
ROW_TILE = 512
FFN_ROW_TILE = 512
ATTN_TILE = 512
FFN_COL_CHUNK = 256
PV_EXTRA_ROWS = 16
MLSTM_EXTRA_ROWS = 16
VMEM_LIMIT = 56 * 1024 * 1024


def _rms(x):
    return x * lax.rsqrt(jnp.mean(x * x, axis=-1, keepdims=True) + EPS)


def _log_sigmoid(x):
    return jnp.minimum(x, 0.0) - jnp.log(1.0 + jnp.exp(-jnp.abs(x)))


def _softcap(a):
    return GATE_CAP * jnp.tanh(a / GATE_CAP)


def _split3(x):
    hi = x.astype(BF16)
    r1 = x - hi.astype(F32)
    mid = r1.astype(BF16)
    lo = (r1 - mid.astype(F32)).astype(BF16)
    return hi, mid, lo


def _cumsum_rows(tri_lower, x):
    return sum(jnp.dot(tri_lower, p, preferred_element_type=F32) for p in _split3(x))


def _cumsum_lanes(x, tri_upper):
    return sum(jnp.dot(p, tri_upper, preferred_element_type=F32) for p in _split3(x))


def _tri(n):
    r = lax.broadcasted_iota(jnp.int32, (n, n), 0)
    c = lax.broadcasted_iota(jnp.int32, (n, n), 1)
    return r >= c, (r <= c)


def _resident(shape):
    nd = len(shape)
    return pl.BlockSpec(shape, lambda *_: (0,) * nd, pipeline_mode=pl.Buffered(1))


def _mod_kernel(c_ref, w_ref, b_ref, o_ref):
    c = c_ref[...]
    cs = (c * jax.nn.sigmoid(c)).astype(BF16)
    o_ref[0] = jnp.dot(cs, w_ref[0].astype(BF16), preferred_element_type=F32) + b_ref[0]


def _mod_call(c, w, b):
    G, D, N = w.shape
    B = c.shape[0]
    tn = 1024
    return pl.pallas_call(
        _mod_kernel,
        grid=(G, N // tn),
        in_specs=[
            pl.BlockSpec((B, D), lambda g, j: (0, 0)),
            pl.BlockSpec((1, D, tn), lambda g, j: (g, 0, j)),
            pl.BlockSpec((1, 1, tn), lambda g, j: (g, 0, j)),
        ],
        out_specs=pl.BlockSpec((1, B, tn), lambda g, j: (g, 0, j)),
        out_shape=jax.ShapeDtypeStruct((G, B, N), F32),
        name="adaln_mod",
    )(c, w, b.reshape(G, 1, N))


def _inproj_kernel(x_ref, sh_ref, sc_ref, g_ref, wko_ref, wqvt_ref, wgc_ref, bc_ref,
                   qt_ref, k_ref, vt_ref, o_ref, gcol_ref, grow_ref, *, dq, dm, heads, qscale):
    x = x_ref[0]
    tm = x.shape[0]
    h = (_rms(x) * g_ref[...] * (1.0 + sc_ref[0]) + sh_ref[0]).astype(BF16)
    nt = (((1,), (1,)), ((), ()))
    ko = jnp.dot(h, wko_ref[...], preferred_element_type=F32)
    k_ref[0] = ko[:, 0:dq].astype(BF16)
    o_ref[0] = ko[:, dq:dq + dm].astype(BF16)
    qvt = lax.dot_general(wqvt_ref[...], h, nt, preferred_element_type=F32)
    qt_ref[0] = (qvt[0:dq] * qscale).astype(BF16)
    dv = dm // heads
    dvx = dv + MLSTM_EXTRA_ROWS
    ones_row = (lax.broadcasted_iota(jnp.int32, (MLSTM_EXTRA_ROWS, tm), 0) == 0).astype(BF16)
    for hh in range(heads):
        vt_ref[0, hh * dvx:hh * dvx + dv, :] = qvt[dq + hh * dv:dq + (hh + 1) * dv].astype(BF16)
        vt_ref[0, hh * dvx + dv:(hh + 1) * dvx, :] = ones_row
    gc = _softcap(jnp.dot(h, wgc_ref[...], preferred_element_type=F32) + bc_ref[...])
    lane = lax.broadcasted_iota(jnp.int32, gc.shape, 1)
    g2 = jnp.where(lane < heads, gc, _log_sigmoid(gc)) * LOG2E
    gcol_ref[0] = g2
    grow_ref[0] = g2.T[0:grow_ref.shape[1]]


def _inproj_call(x, sh, sc, g, w_ko, w_qv_t, wg_col, b_col, *, dq, dm, heads):
    B, S, D = x.shape
    tm = min(ROW_TILE, S)
    dvx_all = dm + heads * MLSTM_EXTRA_ROWS
    row = lambda b, i: (b, i, 0)
    col = lambda b, i: (b, 0, i)
    per_b = lambda b, i: (b, 0, 0)
    kern = functools.partial(_inproj_kernel, dq=dq, dm=dm, heads=heads, qscale=(dq // heads) ** -0.5)
    return pl.pallas_call(
        kern,
        grid=(B, S // tm),
        in_specs=[
            pl.BlockSpec((1, tm, D), row),
            pl.BlockSpec((1, 1, D), per_b),
            pl.BlockSpec((1, 1, D), per_b),
            _resident((1, D)),
            _resident((D, dq + dm)),
            _resident((dq + dm, D)),
            _resident((D, LANES)),
            _resident((1, LANES)),
        ],
        out_specs=[
            pl.BlockSpec((1, dq, tm), col),
            pl.BlockSpec((1, tm, dq), row),
            pl.BlockSpec((1, dvx_all, tm), col),
            pl.BlockSpec((1, tm, dm), row),
            pl.BlockSpec((1, tm, LANES), row),
            pl.BlockSpec((1, 16, tm), col),
        ],
        out_shape=[
            jax.ShapeDtypeStruct((B, dq, S), BF16),
            jax.ShapeDtypeStruct((B, S, dq), BF16),
            jax.ShapeDtypeStruct((B, dvx_all, S), BF16),
            jax.ShapeDtypeStruct((B, S, dm), BF16),
            jax.ShapeDtypeStruct((B, S, LANES), F32),
            jax.ShapeDtypeStruct((B, 16, S), F32),
        ],
        compiler_params=pltpu.CompilerParams(
            dimension_semantics=("arbitrary", "arbitrary"), vmem_limit_bytes=VMEM_LIMIT),
        name="mlstm_inproj",
    )(x, sh, sc, g, w_ko, w_qv_t, wg_col, b_col)


def _mlstm_kernel(qt_ref, k_ref, vt_ref, o_ref, gcol_ref, grow_ref, hn_ref, y_ref,
                  st_ref, m_ref, *, heads, dk, dv, L):
    @pl.when(pl.program_id(1) == 0)
    def _():
        st_ref[...] = jnp.zeros_like(st_ref)
        m_ref[...] = jnp.zeros_like(m_ref)

    gcol = gcol_ref[0]
    grow = grow_ref[0]
    lower, upper = _tri(L)
    bcol = _cumsum_rows(lower.astype(BF16), gcol)
    brow = _cumsum_lanes(grow, upper.astype(BF16))
    dvx = dv + MLSTM_EXTRA_ROWS
    dot = functools.partial(jnp.dot, preferred_element_type=F32)

    hs = range(heads)
    i_row = [grow[h:h + 1, :] for h in hs]
    b_row = [brow[heads + h:heads + h + 1, :] for h in hs]
    i_col = [gcol[:, h:h + 1] for h in hs]
    b_col = [bcol[:, heads + h:heads + h + 1] for h in hs]
    bl = [b[:, L - 1:L] for b in b_row]
    m = [m_ref[h][0:1, 0:1] for h in hs]
    qt = [qt_ref[0, h * dk:(h + 1) * dk, :] for h in hs]
    kh = [k_ref[0, :, h * dk:(h + 1) * dk] for h in hs]
    vt = [vt_ref[0, h * dvx:(h + 1) * dvx, :] for h in hs]
    st = [st_ref[h] for h in hs]

    qk = [dot(kh[h], qt[h]) for h in hs]
    sq = [dot(st[h].astype(BF16), qt[h]) for h in hs]
    m_t, a, w = [], [], []
    for h in hs:
        dmat = jnp.where(upper, b_row[h] + (i_col[h] - b_col[h]), -jnp.inf)
        inter = b_row[h] + m[h]
        m_t.append(jnp.maximum(inter, jnp.max(dmat, axis=0, keepdims=True)))
        a.append(jnp.exp2(inter - m_t[h]))
        w.append(jnp.exp2(dmat - m_t[h]))
    na = [a[h] * sq[h] + dot(vt[h], (qk[h] * w[h]).astype(BF16)) for h in hs]
    for h in hs:
        ws = bl[h] - b_col[h] + i_col[h]
        m_new = jnp.maximum(bl[h] + m[h], jnp.max(ws, axis=0, keepdims=True))
        kw = (kh[h].astype(F32) * jnp.exp2(ws - m_new)).astype(BF16)
        st_ref[h] = jnp.exp2(bl[h] + m[h] - m_new) * st[h] + dot(vt[h], kw)
        m_ref[h] = jnp.broadcast_to(m_new, m_ref.shape[1:])

        hT = na[h][0:dv] * (1.0 / jnp.maximum(jnp.abs(na[h][dv:dv + 1]), jnp.exp2(-m_t[h])))
        hnT = hT * lax.rsqrt(jnp.mean(hT * hT, axis=0, keepdims=True) + EPS)
        og = o_ref[0, :, h * dv:(h + 1) * dv].astype(F32)
        y_ref[0, :, h * dv:(h + 1) * dv] = (jax.nn.sigmoid(og) * (hnT.T * hn_ref[:, h * dv:(h + 1) * dv])).astype(BF16)


def _mlstm_call(qt, k, vt, o, gcol, grow, hn_g, *, heads):
    B, S, dq = k.shape
    dm = o.shape[-1]
    dk, dv = dq // heads, dm // heads
    dvx_all = vt.shape[1]
    L = min(MLSTM_CHUNK, S)
    row = lambda b, c: (b, c, 0)
    col = lambda b, c: (b, 0, c)
    kern = functools.partial(_mlstm_kernel, heads=heads, dk=dk, dv=dv, L=L)
    return pl.pallas_call(
        kern,
        grid=(B, S // L),
        in_specs=[
            pl.BlockSpec((1, dq, L), col),
            pl.BlockSpec((1, L, dq), row),
            pl.BlockSpec((1, dvx_all, L), col),
            pl.BlockSpec((1, L, dm), row),
            pl.BlockSpec((1, L, LANES), row),
            pl.BlockSpec((1, 16, L), col),
            _resident((1, dm)),
        ],
        out_specs=pl.BlockSpec((1, L, dm), row),
        out_shape=jax.ShapeDtypeStruct((B, S, dm), BF16),
        scratch_shapes=[
            pltpu.VMEM((heads, dv + MLSTM_EXTRA_ROWS, dk), F32),
            pltpu.VMEM((heads, 8, LANES), F32),
        ],
        compiler_params=pltpu.CompilerParams(
            dimension_semantics=("arbitrary", "arbitrary"), vmem_limit_bytes=VMEM_LIMIT),
        name="mlstm_recurrence",
    )(qt, k, vt, o, gcol, grow, hn_g)


def _mix_ffn_kernel(y_ref, x_ref, wo_ref, pg_ref, g1_ref, fg_ref, sh2_ref, sc2_ref, g2_ref, fpg_ref,
                    wg_ref, wu_ref, wd_ref, out_ref, act_ref, *, dff):
    dot = functools.partial(jnp.dot, preferred_element_type=F32)
    tm = x_ref.shape[1]
    cw = FFN_COL_CHUNK
    nchunks = dff // cw
    rows = [slice(r * (tm // 2), (r + 1) * (tm // 2)) for r in range(2)]

    def mix(r):
        t = dot(y_ref[0, rows[r], :], wo_ref[...])
        x1 = x_ref[0, rows[r], :] + g1_ref[0] * (_rms(t) * pg_ref[...])
        return x1, (_rms(x1) * fg_ref[...] * (1.0 + sc2_ref[0]) + sh2_ref[0]).astype(BF16)

    def up(r, h2, j):
        g = dot(h2, wg_ref[:, j * cw:(j + 1) * cw])
        u = dot(h2, wu_ref[:, j * cw:(j + 1) * cw])
        act_ref[rows[r], j * cw:(j + 1) * cw] = (g * jax.nn.sigmoid(g) * u).astype(BF16)

    def down(r, x1):
        f = dot(act_ref[rows[r], :], wd_ref[...])
        out_ref[0, rows[r], :] = x1 + g2_ref[0] * (_rms(f) * fpg_ref[...])

    x1a, h2a = mix(0)
    up(0, h2a, 0)
    x1b, h2b = mix(1)
    for j in range(1, nchunks):
        up(0, h2a, j)
    for j in range(nchunks):
        up(1, h2b, j)
        if j == nchunks // 2:
            down(0, x1a)
    down(1, x1b)


def _mix_ffn_call(y, x, w_o, post_g, g1, ffn_pre_g, sh2, sc2, g2, ffn_post_g, w_gate, w_up, w_down):
    B, S, D = x.shape
    dff = w_down.shape[0]
    tm = min(FFN_ROW_TILE, S)
    row = lambda b, i: (b, i, 0)
    per_b = lambda b, i: (b, 0, 0)
    vec = pl.BlockSpec((1, 1, D), per_b)
    return pl.pallas_call(
        functools.partial(_mix_ffn_kernel, dff=dff),
        grid=(B, S // tm),
        in_specs=[
            pl.BlockSpec((1, tm, D), row),
            pl.BlockSpec((1, tm, D), row),
            _resident((D, D)),
            _resident((1, D)),
            vec,
            _resident((1, D)),
            vec, vec, vec,
            _resident((1, D)),
            _resident((D, dff)),
            _resident((D, dff)),
            _resident((dff, D)),
        ],
        out_specs=pl.BlockSpec((1, tm, D), row),
        out_shape=jax.ShapeDtypeStruct((B, S, D), F32),
        scratch_shapes=[pltpu.VMEM((tm, dff), BF16)],
        compiler_params=pltpu.CompilerParams(
            dimension_semantics=("arbitrary", "arbitrary"), vmem_limit_bytes=VMEM_LIMIT),
        name="mix_ffn",
    )(y, x, w_o, post_g, g1, ffn_pre_g, sh2, sc2, g2, ffn_post_g, w_gate, w_up, w_down)


def _qkv_kernel(x_ref, gq_ref, shq_ref, scq_ref, gkv_ref, shkv_ref, sckv_ref,
                wqt_ref, wk_ref, wvt_ref, wfc_ref, bfc_ref,
                qt_ref, k_ref, vt_ref, fcol_ref, frow_ref, ccol_ref, *, qscale, hd):
    @pl.when(pl.program_id(1) == 0)
    def _():
        ccol_ref[...] = jnp.zeros_like(ccol_ref)

    dot = functools.partial(jnp.dot, preferred_element_type=F32)
    xn = _rms(x_ref[0])
    hq = (xn * gq_ref[...] * (1.0 + scq_ref[0]) + shq_ref[0]).astype(BF16)
    hkv = (xn * gkv_ref[...] * (1.0 + sckv_ref[0]) + shkv_ref[0]).astype(BF16)
    qt_ref[0] = (lax.dot_general(wqt_ref[...], hq, (((1,), (1,)), ((), ())), preferred_element_type=F32)
                 * qscale).astype(BF16)
    k_ref[0] = dot(hkv, wk_ref[...]).astype(BF16)
    vT = lax.dot_general(wvt_ref[...], hkv, (((1,), (1,)), ((), ())), preferred_element_type=F32)
    tm = x_ref.shape[1]
    hdx = hd + PV_EXTRA_ROWS
    ones_row = (lax.broadcasted_iota(jnp.int32, (PV_EXTRA_ROWS, tm), 0) == 0).astype(BF16)
    for h in range(vT.shape[0] // hd):
        vt_ref[0, h * hdx:h * hdx + hd, :] = vT[h * hd:(h + 1) * hd].astype(BF16)
        vt_ref[0, h * hdx + hd:(h + 1) * hdx, :] = ones_row
    lower, _ = _tri(tm)
    lf_col = _log_sigmoid(dot(hkv, wfc_ref[...]) + bfc_ref[...])
    f_col = _cumsum_rows(lower.astype(BF16), lf_col) + ccol_ref[0:1, :]
    ccol_ref[...] = jnp.broadcast_to(f_col[tm - 1:tm, :], ccol_ref.shape)
    f2 = f_col * LOG2E
    fcol_ref[0] = f2
    frow_ref[0] = f2.T[0:frow_ref.shape[1]]


def _qkv_call(x, gq, shq, scq, gkv, shkv, sckv, wq_t, wk, wv_t, wf_col, bf_col, *, heads):
    B, S, D = x.shape
    tm = min(ROW_TILE, S)
    hd = D // heads
    vt_rows = heads * (hd + PV_EXTRA_ROWS)
    row = lambda b, i: (b, i, 0)
    col = lambda b, i: (b, 0, i)
    per_b = lambda b, i: (b, 0, 0)
    vec = pl.BlockSpec((1, 1, D), per_b)
    return pl.pallas_call(
        functools.partial(_qkv_kernel, qscale=hd ** -0.5 * LOG2E, hd=hd),
        grid=(B, S // tm),
        in_specs=[
            pl.BlockSpec((1, tm, D), row),
            _resident((1, D)), vec, vec,
            _resident((1, D)), vec, vec,
            _resident((D, D)), _resident((D, D)), _resident((D, D)),
            _resident((D, LANES)), _resident((1, LANES)),
        ],
        out_specs=[
            pl.BlockSpec((1, D, tm), col),
            pl.BlockSpec((1, tm, D), row),
            pl.BlockSpec((1, vt_rows, tm), col),
            pl.BlockSpec((1, tm, LANES), row),
            pl.BlockSpec((1, heads, tm), col),
        ],
        out_shape=[
            jax.ShapeDtypeStruct((B, D, S), BF16),
            jax.ShapeDtypeStruct((B, S, D), BF16),
            jax.ShapeDtypeStruct((B, vt_rows, S), BF16),
            jax.ShapeDtypeStruct((B, S, LANES), F32),
            jax.ShapeDtypeStruct((B, heads, S), F32),
        ],
        scratch_shapes=[pltpu.VMEM((8, LANES), F32)],
        compiler_params=pltpu.CompilerParams(
            dimension_semantics=("arbitrary", "arbitrary"), vmem_limit_bytes=VMEM_LIMIT),
        name="fox_qkv",
    )(x, gq, shq, scq, gkv, shkv, sckv, wq_t, wk, wv_t, wf_col, bf_col)


def _fox_kernel(qt_ref, k_ref, vt_ref, fcol_ref, frow_ref, o_ref, fkb_ref, acc_ref, qm_ref,
                a0_ref, a1_ref, p0_ref, p1_ref, *, hd, tq, tk):
    hp = pl.program_id(1)
    S = k_ref.shape[1]
    nq = S // tq
    hdx = hd + PV_EXTRA_ROWS
    lane = lax.broadcasted_iota(jnp.int32, (1, LANES), 1)
    sub = lax.broadcasted_iota(jnp.int32, (LANES, 1), 0)
    a_ref = (a0_ref, a1_ref)
    p_ref = (p0_ref, p1_ref)

    def spread_fk(c):
        rows = pl.ds(pl.multiple_of(c * tk, tk), tk)
        fc = fcol_ref[0, rows, :]
        for h in range(2):
            col = jnp.sum(jnp.where(lane == 2 * hp + h, fc, 0.0), axis=1, keepdims=True)
            fkb_ref[h, rows, :] = jnp.broadcast_to(col, (tk, LANES))

    def masked_q(i, h):
        qT = qt_ref[0, :, pl.ds(pl.multiple_of(i * tq, tq), tq)].astype(F32)
        return jnp.where((sub < hd) == (h == 0), qT, 0.0).astype(BF16)

    def qk(j, h, qm=None):
        start = pl.multiple_of(j * tk, tk)
        fk = fkb_ref[h, pl.ds(start, tk), :]
        a_ref[h][:, 0:tq] = (jnp.dot(k_ref[0, pl.ds(start, tk), :], qm_ref[h] if qm is None else qm,
                                     preferred_element_type=F32)
                             - jnp.concatenate([fk] * (tq // LANES), axis=1))

    def softmax(h, m_prev, fq, diagonal):
        a = a_ref[h][:, 0:tq]
        if diagonal:
            r = lax.broadcasted_iota(jnp.int32, (tk, tq), 0)
            c = lax.broadcasted_iota(jnp.int32, (tk, tq), 1)
            a = jnp.where(r <= c, a, -jnp.inf)
        m_new = jnp.maximum(m_prev, jnp.max(a, axis=0, keepdims=True) + fq)
        alpha = jnp.exp2(m_prev - m_new)
        p_ref[h][:, 0:tq] = jnp.exp2(a - (m_new - fq)).astype(BF16)
        return m_new, alpha

    def pv(j, h, alpha):
        start = pl.multiple_of(j * tk, tk)
        v2t = vt_ref[0, h * hdx:(h + 1) * hdx, pl.ds(start, tk)]
        acc_ref[h] = alpha * acc_ref[h] + jnp.dot(v2t, p_ref[h][:, 0:tq], preferred_element_type=F32)

    def body(j, carry, fq, diagonal, deferred=True):
        m0, m1, alpha1 = carry
        if deferred:
            pv(j - 1, 1, alpha1)
        m0, alpha0 = softmax(0, m0, fq[0], diagonal)
        qk(j, 1)
        pv(j, 0, alpha0)
        if not diagonal:
            qk(j + 1, 0)
        m1, alpha1 = softmax(1, m1, fq[1], diagonal)
        return m0, m1, alpha1

    def query_block(i, first):
        qstart = pl.multiple_of(i * tq, tq)
        for h in range(2):
            qm_ref[h] = masked_q(i, h)
        fq = [frow_ref[0, 0, h:h + 1, pl.ds(qstart, tq)] for h in range(2)]
        acc_ref[...] = jnp.zeros(acc_ref.shape, F32)
        neg = jnp.full((1, tq), -jnp.inf, F32)
        carry = (neg, neg, jnp.ones((1, tq), F32))
        if first:
            _, _, alpha1 = body(i, carry, fq, True, deferred=False)
        else:
            carry = body(0, carry, fq, False, deferred=False)
            carry = lax.fori_loop(1, i, lambda j, c: body(j, c, fq, False), carry)
            _, _, alpha1 = body(i, carry, fq, True)
        qk(0, 0, masked_q(jnp.minimum(i + 1, nq - 1), 0))
        spread_fk(jnp.minimum(i + 1, nq - 1))
        pv(i, 1, alpha1)
        ot = jnp.concatenate([acc_ref[h, 0:hd, :] / acc_ref[h, hd:hd + 1, :] for h in range(2)], axis=0)
        o_ref[0, pl.ds(qstart, tq), :] = ot.T.astype(BF16)

    spread_fk(0)
    qk(0, 0, masked_q(0, 0))
    query_block(jnp.int32(0), True)

    def later_block(i, carry):
        query_block(i, False)
        return carry

    lax.fori_loop(1, nq, later_block, 0)


def _fox_call(qt, k, vt, fcol, frow, *, heads):
    B, S, D = k.shape
    hd = D // heads
    assert 2 * hd == LANES
    t = min(ATTN_TILE, S)
    frow = frow.reshape(B, heads // 2, 2, S)
    return pl.pallas_call(
        functools.partial(_fox_kernel, hd=hd, tq=t, tk=t),
        grid=(B, heads // 2),
        in_specs=[
            pl.BlockSpec((1, LANES, S), lambda b, p: (b, p, 0)),
            pl.BlockSpec((1, S, LANES), lambda b, p: (b, 0, p)),
            pl.BlockSpec((1, 2 * (hd + PV_EXTRA_ROWS), S), lambda b, p: (b, p, 0)),
            pl.BlockSpec((1, S, LANES), lambda b, p: (b, 0, 0)),
            pl.BlockSpec((1, 1, 2, S), lambda b, p: (b, p, 0, 0)),
        ],
        out_specs=pl.BlockSpec((1, S, LANES), lambda b, p: (b, 0, p)),
        out_shape=jax.ShapeDtypeStruct((B, S, D), BF16),
        scratch_shapes=[
            pltpu.VMEM((2, S, LANES), F32),
            pltpu.VMEM((2, hd + PV_EXTRA_ROWS, t), F32),
            pltpu.VMEM((2, LANES, t), BF16),
            pltpu.VMEM((t, t + LANES), F32), pltpu.VMEM((t, t + LANES), F32),
            pltpu.VMEM((t, t + LANES), BF16), pltpu.VMEM((t, t + LANES), BF16),
        ],
        compiler_params=pltpu.CompilerParams(
            dimension_semantics=("arbitrary", "arbitrary"), vmem_limit_bytes=VMEM_LIMIT),
        name="fox_attention",
    )(qt, k, vt, fcol, frow)


def _pad_cols(w, n):
    return jnp.pad(w, ((0, 0), (0, n - w.shape[1])))


def kernel(x, c, mod_w, mod_b, mix_pre_g, mix_post_g, ffn_pre_g, ffn_post_g, ffn_w_in, ffn_w_out,
           a_w_in, a_gate_b, a_hnorm_g, a_w_out, kv_norm_g, kv_mod_w, kv_mod_b, kv_w, kv_fgate_b,
           b_w_q, b_w_out):
    B, S, D = x.shape
    depth = mod_w.shape[0]
    n_a = a_w_in.shape[0]
    m_heads = a_gate_b.shape[1] // 2
    f_heads = kv_fgate_b.shape[0]
    dff = ffn_w_out.shape[1]
    dq = (a_w_in.shape[2] - 2 * D - 2 * m_heads) // 2

    mods = _mod_call(c, mod_w, mod_b)
    kv_mods = _mod_call(c, kv_mod_w[None], kv_mod_b[None])[0]
    vecs = lambda m, n: [v.reshape(B, 1, D) for v in jnp.split(m, n, axis=-1)]
    row = lambda g: g.reshape(1, D)

    k_sh = v_sh = fcol = frow = None
    for l in range(depth):
        sh1, sc1, g1, sh2, sc2, g2 = vecs(mods[l], 6)
        if l < n_a:
            w = a_w_in[l]
            nmain = 2 * dq + 2 * D
            wg = w[:, nmain:]
            w_ko = jnp.concatenate([w[:, dq:2 * dq], w[:, 2 * dq + D:nmain]], axis=1)
            w_qv = jnp.concatenate([w[:, :dq], w[:, 2 * dq:2 * dq + D]], axis=1)
            qt, k, vt, o, gcol, grow = _inproj_call(
                x, sh1, sc1, row(mix_pre_g[l]),
                w_ko.astype(BF16), w_qv.T.astype(BF16), _pad_cols(wg, LANES).astype(BF16),
                _pad_cols(a_gate_b[l][None, :], LANES), dq=dq, dm=D, heads=m_heads)
            y = _mlstm_call(qt, k, vt, o, gcol, grow, row(a_hnorm_g[l]), heads=m_heads)
            w_o = a_w_out[l]
        else:
            if l == n_a:
                kv_sh, kv_sc = vecs(kv_mods, 2)
                wf = kv_w[:, 2 * D:]
                shared = dict(
                    gkv=row(kv_norm_g), shkv=kv_sh, sckv=kv_sc,
                    wk=kv_w[:, :D].astype(BF16), wv=kv_w[:, D:2 * D].T.astype(BF16),
                    wf_col=_pad_cols(wf, LANES).astype(BF16), bf_col=_pad_cols(kv_fgate_b[None, :], LANES))
            qt, k_new, v_new, fcol_new, frow_new = _qkv_call(
                x, row(mix_pre_g[l]), sh1, sc1, shared["gkv"], shared["shkv"], shared["sckv"],
                b_w_q[l - n_a].T.astype(BF16), shared["wk"], shared["wv"],
                shared["wf_col"], shared["bf_col"], heads=f_heads)
            if l == n_a:
                k_sh, v_sh, fcol, frow = k_new, v_new, fcol_new, frow_new
            y = _fox_call(qt, k_sh, v_sh, fcol, frow, heads=f_heads)
            w_o = b_w_out[l - n_a]
        w_in = ffn_w_in[l]
        x = _mix_ffn_call(
            y, x, w_o.astype(BF16), row(mix_post_g[l]), g1, row(ffn_pre_g[l]), sh2, sc2, g2,
            row(ffn_post_g[l]), w_in[:, :dff].astype(BF16), w_in[:, dff:].astype(BF16),
            ffn_w_out[l].astype(BF16))
    return x
```

```python
import functools

import jax
import jax.numpy as jnp
from jax import lax
from jax.experimental import pallas as pl
from jax.experimental.pallas import tpu as pltpu

F32 = jnp.float32
BF16 = jnp.bfloat16

EPS = 1e-6
LOG2E = 1.4426950408889634
GATE_CAP = 15.0
LANES = 128
MLSTM_CHUNK = 256
ROW_TILE = 512
FFN_ROW_TILE = 512
ATTN_TILE = 512
FFN_COL_CHUNK = 256
PV_EXTRA_ROWS = 16
MLSTM_EXTRA_ROWS = 16
VMEM_LIMIT = 56 * 1024 * 1024


def _rms(x):
    return x * lax.rsqrt(jnp.mean(x * x, axis=-1, keepdims=True) + EPS)


def _log_sigmoid(x):
    return jnp.minimum(x, 0.0) - jnp.log(1.0 + jnp.exp(-jnp.abs(x)))


def _softcap(a):
    return GATE_CAP * jnp.tanh(a / GATE_CAP)


def _split3(x):
    hi = x.astype(BF16)
    r1 = x - hi.astype(F32)
    mid = r1.astype(BF16)
    lo = (r1 - mid.astype(F32)).astype(BF16)
    return hi, mid, lo


def _cumsum_rows(tri_lower, x):
    return sum(jnp.dot(tri_lower, p, preferred_element_type=F32) for p in _split3(x))


def _cumsum_lanes(x, tri_upper):
    return sum(jnp.dot(p, tri_upper, preferred_element_type=F32) for p in _split3(x))


def _tri(n):
    r = lax.broadcasted_iota(jnp.int32, (n, n), 0)
    c = lax.broadcasted_iota(jnp.int32, (n, n), 1)
    return r >= c, (r <= c)


def _chunk_spec(chunk, width):
    k = chunk[1]
    return pl.BlockSpec((1, 1, width), lambda b, i: (b, 0, k))


def _resident(shape):
    nd = len(shape)
    return pl.BlockSpec(shape, lambda *_: (0,) * nd, pipeline_mode=pl.Buffered(1))


def _mod_kernel(c_ref, w_ref, b_ref, o_ref):
    c = c_ref[...]
    cs = (c * jax.nn.sigmoid(c)).astype(BF16)
    o_ref[0] = jnp.dot(cs, w_ref[0].astype(BF16), preferred_element_type=F32) + b_ref[0]


def _mod_call(c, w, b):
    G, D, N = w.shape
    B = c.shape[0]
    tn = 1024
    return pl.pallas_call(
        _mod_kernel,
        grid=(G, N // tn),
        in_specs=[
            pl.BlockSpec((B, D), lambda g, j: (0, 0)),
            pl.BlockSpec((1, D, tn), lambda g, j: (g, 0, j)),
            pl.BlockSpec((1, 1, tn), lambda g, j: (g, 0, j)),
        ],
        out_specs=pl.BlockSpec((1, B, tn), lambda g, j: (g, 0, j)),
        out_shape=jax.ShapeDtypeStruct((G, B, N), F32),
        name="adaln_mod",
    )(c, w, b.reshape(G, 1, N))


def _inproj_kernel(x_ref, sh_ref, sc_ref, g_ref, wko_ref, wqvt_ref, wgc_ref, bc_ref,
                   qt_ref, k_ref, vt_ref, o_ref, gcol_ref, grow_ref, *, dq, dm, heads, qscale):
    x = x_ref[0]
    tm = x.shape[0]
    h = (_rms(x) * g_ref[...] * (1.0 + sc_ref[0]) + sh_ref[0]).astype(BF16)
    nt = (((1,), (1,)), ((), ()))
    ko = jnp.dot(h, wko_ref[...], preferred_element_type=F32)
    k_ref[0] = ko[:, 0:dq].astype(BF16)
    o_ref[0] = ko[:, dq:dq + dm].astype(BF16)
    qvt = lax.dot_general(wqvt_ref[...], h, nt, preferred_element_type=F32)
    qt_ref[0] = (qvt[0:dq] * qscale).astype(BF16)
    dv = dm // heads
    dvx = dv + MLSTM_EXTRA_ROWS
    ones_row = (lax.broadcasted_iota(jnp.int32, (MLSTM_EXTRA_ROWS, tm), 0) == 0).astype(BF16)
    for hh in range(heads):
        vt_ref[0, hh * dvx:hh * dvx + dv, :] = qvt[dq + hh * dv:dq + (hh + 1) * dv].astype(BF16)
        vt_ref[0, hh * dvx + dv:(hh + 1) * dvx, :] = ones_row
    gc = _softcap(jnp.dot(h, wgc_ref[...], preferred_element_type=F32) + bc_ref[...])
    lane = lax.broadcasted_iota(jnp.int32, gc.shape, 1)
    g2 = jnp.where(lane < heads, gc, _log_sigmoid(gc)) * LOG2E
    gcol_ref[0] = g2
    grow_ref[0] = g2.T[0:grow_ref.shape[1]]


def _inproj_call(x, sh, sc, g, w_ko, w_qv_t, wg_col, b_col, *, dq, dm, heads):
    B, S, D = x.shape
    tm = min(ROW_TILE, S)
    dvx_all = dm + heads * MLSTM_EXTRA_ROWS
    row = lambda b, i: (b, i, 0)
    col = lambda b, i: (b, 0, i)
    kern = functools.partial(_inproj_kernel, dq=dq, dm=dm, heads=heads, qscale=(dq // heads) ** -0.5)
    return pl.pallas_call(
        kern,
        grid=(B, S // tm),
        in_specs=[
            pl.BlockSpec((1, tm, D), row),
            _chunk_spec(sh, D),
            _chunk_spec(sc, D),
            _resident((1, D)),
            _resident((D, dq + dm)),
            _resident((dq + dm, D)),
            _resident((D, LANES)),
            _resident((1, LANES)),
        ],
        out_specs=[
            pl.BlockSpec((1, dq, tm), col),
            pl.BlockSpec((1, tm, dq), row),
            pl.BlockSpec((1, dvx_all, tm), col),
            pl.BlockSpec((1, tm, dm), row),
            pl.BlockSpec((1, tm, LANES), row),
            pl.BlockSpec((1, 16, tm), col),
        ],
        out_shape=[
            jax.ShapeDtypeStruct((B, dq, S), BF16),
            jax.ShapeDtypeStruct((B, S, dq), BF16),
            jax.ShapeDtypeStruct((B, dvx_all, S), BF16),
            jax.ShapeDtypeStruct((B, S, dm), BF16),
            jax.ShapeDtypeStruct((B, S, LANES), F32),
            jax.ShapeDtypeStruct((B, 16, S), F32),
        ],
        compiler_params=pltpu.CompilerParams(
            dimension_semantics=("arbitrary", "arbitrary"), vmem_limit_bytes=VMEM_LIMIT),
        name="mlstm_inproj",
    )(x, sh[0], sc[0], g, w_ko, w_qv_t, wg_col, b_col)


def _mlstm_kernel(qt_ref, k_ref, vt_ref, o_ref, gcol_ref, grow_ref, hn_ref, y_ref,
                  st_ref, m_ref, *, heads, dk, dv, L):
    @pl.when(pl.program_id(1) == 0)
    def _():
        st_ref[...] = jnp.zeros_like(st_ref)
        m_ref[...] = jnp.zeros_like(m_ref)

    gcol = gcol_ref[0]
    grow = grow_ref[0]
    lower, upper = _tri(L)
    bcol = _cumsum_rows(lower.astype(BF16), gcol)
    brow = _cumsum_lanes(grow, upper.astype(BF16))
    dvx = dv + MLSTM_EXTRA_ROWS
    dot = functools.partial(jnp.dot, preferred_element_type=F32)

    hs = range(heads)
    i_row = [grow[h:h + 1, :] for h in hs]
    b_row = [brow[heads + h:heads + h + 1, :] for h in hs]
    i_col = [gcol[:, h:h + 1] for h in hs]
    b_col = [bcol[:, heads + h:heads + h + 1] for h in hs]
    bl = [b[:, L - 1:L] for b in b_row]
    m = [m_ref[h][0:1, 0:1] for h in hs]
    qt = [qt_ref[0, h * dk:(h + 1) * dk, :] for h in hs]
    kh = [k_ref[0, :, h * dk:(h + 1) * dk] for h in hs]
    vt = [vt_ref[0, h * dvx:(h + 1) * dvx, :] for h in hs]
    st = [st_ref[h] for h in hs]

    qk = [dot(kh[h], qt[h]) for h in hs]
    sq = [dot(st[h].astype(BF16), qt[h]) for h in hs]
    m_t, a, w = [], [], []
    for h in hs:
        dmat = jnp.where(upper, b_row[h] + (i_col[h] - b_col[h]), -jnp.inf)
        inter = b_row[h] + m[h]
        m_t.append(jnp.maximum(inter, jnp.max(dmat, axis=0, keepdims=True)))
        a.append(jnp.exp2(inter - m_t[h]))
        w.append(jnp.exp2(dmat - m_t[h]))
    na = [a[h] * sq[h] + dot(vt[h], (qk[h] * w[h]).astype(BF16)) for h in hs]
    for h in hs:
        ws = bl[h] - b_col[h] + i_col[h]
        m_new = jnp.maximum(bl[h] + m[h], jnp.max(ws, axis=0, keepdims=True))
        kw = (kh[h].astype(F32) * jnp.exp2(ws - m_new)).astype(BF16)
        st_ref[h] = jnp.exp2(bl[h] + m[h] - m_new) * st[h] + dot(vt[h], kw)
        m_ref[h] = jnp.broadcast_to(m_new, m_ref.shape[1:])

        hT = na[h][0:dv] * (1.0 / jnp.maximum(jnp.abs(na[h][dv:dv + 1]), jnp.exp2(-m_t[h])))
        hnT = hT * lax.rsqrt(jnp.mean(hT * hT, axis=0, keepdims=True) + EPS)
        og = o_ref[0, :, h * dv:(h + 1) * dv].astype(F32)
        y_ref[0, :, h * dv:(h + 1) * dv] = (jax.nn.sigmoid(og) * (hnT.T * hn_ref[:, h * dv:(h + 1) * dv])).astype(BF16)


def _mlstm_call(qt, k, vt, o, gcol, grow, hn_g, *, heads):
    B, S, dq = k.shape
    dm = o.shape[-1]
    dk, dv = dq // heads, dm // heads
    dvx_all = vt.shape[1]
    L = min(MLSTM_CHUNK, S)
    row = lambda b, c: (b, c, 0)
    col = lambda b, c: (b, 0, c)
    kern = functools.partial(_mlstm_kernel, heads=heads, dk=dk, dv=dv, L=L)
    return pl.pallas_call(
        kern,
        grid=(B, S // L),
        in_specs=[
            pl.BlockSpec((1, dq, L), col),
            pl.BlockSpec((1, L, dq), row),
            pl.BlockSpec((1, dvx_all, L), col),
            pl.BlockSpec((1, L, dm), row),
            pl.BlockSpec((1, L, LANES), row),
            pl.BlockSpec((1, 16, L), col),
            _resident((1, dm)),
        ],
        out_specs=pl.BlockSpec((1, L, dm), row),
        out_shape=jax.ShapeDtypeStruct((B, S, dm), BF16),
        scratch_shapes=[
            pltpu.VMEM((heads, dv + MLSTM_EXTRA_ROWS, dk), F32),
            pltpu.VMEM((heads, 8, LANES), F32),
        ],
        compiler_params=pltpu.CompilerParams(
            dimension_semantics=("arbitrary", "arbitrary"), vmem_limit_bytes=VMEM_LIMIT),
        name="mlstm_recurrence",
    )(qt, k, vt, o, gcol, grow, hn_g)


def _mix_ffn_kernel(y_ref, x_ref, wo_ref, pg_ref, g1_ref, fg_ref, sh2_ref, sc2_ref, g2_ref, fpg_ref,
                    wg_ref, wu_ref, wd_ref, out_ref, act_ref, *, dff):
    dot = functools.partial(jnp.dot, preferred_element_type=F32)
    tm = x_ref.shape[1]
    cw = FFN_COL_CHUNK
    nchunks = dff // cw
    rows = [slice(r * (tm // 2), (r + 1) * (tm // 2)) for r in range(2)]

    def mix(r):
        t = dot(y_ref[0, rows[r], :], wo_ref[...])
        x1 = x_ref[0, rows[r], :] + g1_ref[0] * (_rms(t) * pg_ref[...])
        return x1, (_rms(x1) * fg_ref[...] * (1.0 + sc2_ref[0]) + sh2_ref[0]).astype(BF16)

    def up(r, h2, j):
        g = dot(h2, wg_ref[:, j * cw:(j + 1) * cw])
        u = dot(h2, wu_ref[:, j * cw:(j + 1) * cw])
        act_ref[rows[r], j * cw:(j + 1) * cw] = (g * jax.nn.sigmoid(g) * u).astype(BF16)

    def down(r, x1):
        f = dot(act_ref[rows[r], :], wd_ref[...])
        out_ref[0, rows[r], :] = x1 + g2_ref[0] * (_rms(f) * fpg_ref[...])

    x1a, h2a = mix(0)
    up(0, h2a, 0)
    x1b, h2b = mix(1)
    for j in range(1, nchunks):
        up(0, h2a, j)
    for j in range(nchunks):
        up(1, h2b, j)
        if j == nchunks // 2:
            down(0, x1a)
    down(1, x1b)


def _mix_ffn_call(y, x, w_o, post_g, g1, ffn_pre_g, sh2, sc2, g2, ffn_post_g, w_gate, w_up, w_down):
    B, S, D = x.shape
    dff = w_down.shape[0]
    tm = min(FFN_ROW_TILE, S)
    row = lambda b, i: (b, i, 0)
    return pl.pallas_call(
        functools.partial(_mix_ffn_kernel, dff=dff),
        grid=(B, S // tm),
        in_specs=[
            pl.BlockSpec((1, tm, D), row),
            pl.BlockSpec((1, tm, D), row),
            _resident((D, D)),
            _resident((1, D)),
            _chunk_spec(g1, D),
            _resident((1, D)),
            _chunk_spec(sh2, D), _chunk_spec(sc2, D), _chunk_spec(g2, D),
            _resident((1, D)),
            _resident((D, dff)),
            _resident((D, dff)),
            _resident((dff, D)),
        ],
        out_specs=pl.BlockSpec((1, tm, D), row),
        out_shape=jax.ShapeDtypeStruct((B, S, D), F32),
        scratch_shapes=[pltpu.VMEM((tm, dff), BF16)],
        compiler_params=pltpu.CompilerParams(
            dimension_semantics=("arbitrary", "arbitrary"), vmem_limit_bytes=VMEM_LIMIT),
        name="mix_ffn",
    )(y, x, w_o, post_g, g1[0], ffn_pre_g, sh2[0], sc2[0], g2[0], ffn_post_g, w_gate, w_up, w_down)


def _qkv_kernel(x_ref, gq_ref, shq_ref, scq_ref, gkv_ref, shkv_ref, sckv_ref,
                wqt_ref, wk_ref, wvt_ref, wfc_ref, bfc_ref,
                qt_ref, k_ref, vt_ref, fcol_ref, frow_ref, ccol_ref, *, qscale, hd):
    @pl.when(pl.program_id(1) == 0)
    def _():
        ccol_ref[...] = jnp.zeros_like(ccol_ref)

    dot = functools.partial(jnp.dot, preferred_element_type=F32)
    xn = _rms(x_ref[0])
    hq = (xn * gq_ref[...] * (1.0 + scq_ref[0]) + shq_ref[0]).astype(BF16)
    qt_ref[0] = (lax.dot_general(wqt_ref[...], hq, (((1,), (1,)), ((), ())), preferred_element_type=F32)
                 * qscale).astype(BF16)
    hkv = (xn * gkv_ref[...] * (1.0 + sckv_ref[0]) + shkv_ref[0]).astype(BF16)
    k_ref[0] = dot(hkv, wk_ref[...]).astype(BF16)
    vT = lax.dot_general(wvt_ref[...], hkv, (((1,), (1,)), ((), ())), preferred_element_type=F32)
    tm = x_ref.shape[1]
    hdx = hd + PV_EXTRA_ROWS
    ones_row = (lax.broadcasted_iota(jnp.int32, (PV_EXTRA_ROWS, tm), 0) == 0).astype(BF16)
    for h in range(vT.shape[0] // hd):
        vt_ref[0, h * hdx:h * hdx + hd, :] = vT[h * hd:(h + 1) * hd].astype(BF16)
        vt_ref[0, h * hdx + hd:(h + 1) * hdx, :] = ones_row
    lower, _ = _tri(tm)
    lf_col = _log_sigmoid(dot(hkv, wfc_ref[...]) + bfc_ref[...])
    f_col = _cumsum_rows(lower.astype(BF16), lf_col) + ccol_ref[0:1, :]
    ccol_ref[...] = jnp.broadcast_to(f_col[tm - 1:tm, :], ccol_ref.shape)
    f2 = f_col * LOG2E
    fcol_ref[0] = f2
    frow_ref[0] = f2.T[0:frow_ref.shape[1]]


def _qkv_call(x, gq, shq, scq, gkv, shkv, sckv, wq_t, wk, wv_t, wf_col, bf_col, *, heads):
    B, S, D = x.shape
    tm = min(ROW_TILE, S)
    hd = D // heads
    vt_rows = heads * (hd + PV_EXTRA_ROWS)
    row = lambda b, i: (b, i, 0)
    col = lambda b, i: (b, 0, i)
    return pl.pallas_call(
        functools.partial(_qkv_kernel, qscale=hd ** -0.5 * LOG2E, hd=hd),
        grid=(B, S // tm),
        in_specs=[
            pl.BlockSpec((1, tm, D), row),
            _resident((1, D)), _chunk_spec(shq, D), _chunk_spec(scq, D),
            _resident((1, D)), _chunk_spec(shkv, D), _chunk_spec(sckv, D),
            _resident((D, D)), _resident((D, D)), _resident((D, D)),
            _resident((D, LANES)), _resident((1, LANES)),
        ],
        out_specs=[
            pl.BlockSpec((1, D, tm), col),
            pl.BlockSpec((1, tm, D), row),
            pl.BlockSpec((1, vt_rows, tm), col),
            pl.BlockSpec((1, tm, LANES), row),
            pl.BlockSpec((1, heads, tm), col),
        ],
        out_shape=[
            jax.ShapeDtypeStruct((B, D, S), BF16),
            jax.ShapeDtypeStruct((B, S, D), BF16),
            jax.ShapeDtypeStruct((B, vt_rows, S), BF16),
            jax.ShapeDtypeStruct((B, S, LANES), F32),
            jax.ShapeDtypeStruct((B, heads, S), F32),
        ],
        scratch_shapes=[pltpu.VMEM((8, LANES), F32)],
        compiler_params=pltpu.CompilerParams(
            dimension_semantics=("arbitrary", "arbitrary"), vmem_limit_bytes=VMEM_LIMIT),
        name="fox_qkv",
    )(x, gq, shq[0], scq[0], gkv, shkv[0], sckv[0], wq_t, wk, wv_t, wf_col, bf_col)


def _fox_kernel(qt_ref, k_ref, vt_ref, fcol_ref, frow_ref, o_ref, fkb_ref, acc_ref, qm_ref,
                a0_ref, a1_ref, p0_ref, p1_ref, *, hd, tq, tk):
    hp = pl.program_id(1)
    S = k_ref.shape[1]
    nq = S // tq
    hdx = hd + PV_EXTRA_ROWS
    lane = lax.broadcasted_iota(jnp.int32, (1, LANES), 1)
    sub = lax.broadcasted_iota(jnp.int32, (LANES, 1), 0)
    a_ref = (a0_ref, a1_ref)
    p_ref = (p0_ref, p1_ref)

    def spread_fk(c):
        rows = pl.ds(pl.multiple_of(c * tk, tk), tk)
        fc = fcol_ref[0, rows, :]
        for h in range(2):
            col = jnp.sum(jnp.where(lane == 2 * hp + h, fc, 0.0), axis=1, keepdims=True)
            fkb_ref[h, rows, :] = jnp.broadcast_to(col, (tk, LANES))

    def masked_q(i, h):
        qT = qt_ref[0, :, pl.ds(pl.multiple_of(i * tq, tq), tq)].astype(F32)
        return jnp.where((sub < hd) == (h == 0), qT, 0.0).astype(BF16)

    def qk(j, h, qm=None):
        start = pl.multiple_of(j * tk, tk)
        fk = fkb_ref[h, pl.ds(start, tk), :]
        a_ref[h][:, 0:tq] = (jnp.dot(k_ref[0, pl.ds(start, tk), :], qm_ref[h] if qm is None else qm,
                                     preferred_element_type=F32)
                             - jnp.concatenate([fk] * (tq // LANES), axis=1))

    def softmax(h, m_prev, fq, diagonal):
        a = a_ref[h][:, 0:tq]
        if diagonal:
            r = lax.broadcasted_iota(jnp.int32, (tk, tq), 0)
            c = lax.broadcasted_iota(jnp.int32, (tk, tq), 1)
            a = jnp.where(r <= c, a, -jnp.inf)
        m_new = jnp.maximum(m_prev, jnp.max(a, axis=0, keepdims=True) + fq)
        alpha = jnp.exp2(m_prev - m_new)
        p_ref[h][:, 0:tq] = jnp.exp2(a - (m_new - fq)).astype(BF16)
        return m_new, alpha

    def pv(j, h, alpha):
        start = pl.multiple_of(j * tk, tk)
        v2t = vt_ref[0, h * hdx:(h + 1) * hdx, pl.ds(start, tk)]
        acc_ref[h] = alpha * acc_ref[h] + jnp.dot(v2t, p_ref[h][:, 0:tq], preferred_element_type=F32)

    def body(j, carry, fq, diagonal, deferred=True):
        m0, m1, alpha1 = carry
        if deferred:
            pv(j - 1, 1, alpha1)
        m0, alpha0 = softmax(0, m0, fq[0], diagonal)
        qk(j, 1)
        pv(j, 0, alpha0)
        if not diagonal:
            qk(j + 1, 0)
        m1, alpha1 = softmax(1, m1, fq[1], diagonal)
        return m0, m1, alpha1

    def query_block(i, first):
        qstart = pl.multiple_of(i * tq, tq)
        for h in range(2):
            qm_ref[h] = masked_q(i, h)
        fq = [frow_ref[0, 0, h:h + 1, pl.ds(qstart, tq)] for h in range(2)]
        acc_ref[...] = jnp.zeros(acc_ref.shape, F32)
        neg = jnp.full((1, tq), -jnp.inf, F32)
        carry = (neg, neg, jnp.ones((1, tq), F32))
        if first:
            _, _, alpha1 = body(i, carry, fq, True, deferred=False)
        else:
            carry = body(0, carry, fq, False, deferred=False)
            carry = lax.fori_loop(1, i, lambda j, c: body(j, c, fq, False), carry)
            _, _, alpha1 = body(i, carry, fq, True)
        qk(0, 0, masked_q(jnp.minimum(i + 1, nq - 1), 0))
        spread_fk(jnp.minimum(i + 1, nq - 1))
        pv(i, 1, alpha1)
        ot = jnp.concatenate([acc_ref[h, 0:hd, :] / acc_ref[h, hd:hd + 1, :] for h in range(2)], axis=0)
        o_ref[0, pl.ds(qstart, tq), :] = ot.T.astype(BF16)

    spread_fk(0)
    qk(0, 0, masked_q(0, 0))
    query_block(jnp.int32(0), True)

    def later_block(i, carry):
        query_block(i, False)
        return carry

    lax.fori_loop(1, nq, later_block, 0)


def _fox_call(qt, k, vt, fcol, frow, *, heads):
    B, S, D = k.shape
    hd = D // heads
    assert 2 * hd == LANES
    t = min(ATTN_TILE, S)
    frow = frow.reshape(B, heads // 2, 2, S)
    return pl.pallas_call(
        functools.partial(_fox_kernel, hd=hd, tq=t, tk=t),
        grid=(B, heads // 2),
        in_specs=[
            pl.BlockSpec((1, LANES, S), lambda b, p: (b, p, 0)),
            pl.BlockSpec((1, S, LANES), lambda b, p: (b, 0, p)),
            pl.BlockSpec((1, 2 * (hd + PV_EXTRA_ROWS), S), lambda b, p: (b, p, 0)),
            pl.BlockSpec((1, S, LANES), lambda b, p: (b, 0, 0)),
            pl.BlockSpec((1, 1, 2, S), lambda b, p: (b, p, 0, 0)),
        ],
        out_specs=pl.BlockSpec((1, S, LANES), lambda b, p: (b, 0, p)),
        out_shape=jax.ShapeDtypeStruct((B, S, D), BF16),
        scratch_shapes=[
            pltpu.VMEM((2, S, LANES), F32),
            pltpu.VMEM((2, hd + PV_EXTRA_ROWS, t), F32),
            pltpu.VMEM((2, LANES, t), BF16),
            pltpu.VMEM((t, t + LANES), F32), pltpu.VMEM((t, t + LANES), F32),
            pltpu.VMEM((t, t + LANES), BF16), pltpu.VMEM((t, t + LANES), BF16),
        ],
        compiler_params=pltpu.CompilerParams(
            dimension_semantics=("arbitrary", "arbitrary"), vmem_limit_bytes=VMEM_LIMIT),
        name="fox_attention",
    )(qt, k, vt, fcol, frow)


def _pad_cols(w, n):
    return jnp.pad(w, ((0, 0), (0, n - w.shape[1])))


def kernel(x, c, mod_w, mod_b, mix_pre_g, mix_post_g, ffn_pre_g, ffn_post_g, ffn_w_in, ffn_w_out,
           a_w_in, a_gate_b, a_hnorm_g, a_w_out, kv_norm_g, kv_mod_w, kv_mod_b, kv_w, kv_fgate_b,
           b_w_q, b_w_out):
    B, S, D = x.shape
    depth = mod_w.shape[0]
    n_a = a_w_in.shape[0]
    m_heads = a_gate_b.shape[1] // 2
    f_heads = kv_fgate_b.shape[0]
    dff = ffn_w_out.shape[1]
    dq = (a_w_in.shape[2] - 2 * D - 2 * m_heads) // 2

    mods = _mod_call(c, mod_w, mod_b)
    kv_mods = _mod_call(c, kv_mod_w[None], kv_mod_b[None])[0]
    vecs = lambda m, n: [(m.reshape(B, 1, n * D), k) for k in range(n)]
    row = lambda g: g.reshape(1, D)

    k_sh = v_sh = fcol = frow = None
    for l in range(depth):
        sh1, sc1, g1, sh2, sc2, g2 = vecs(mods[l], 6)
        if l < n_a:
            w = a_w_in[l]
            nmain = 2 * dq + 2 * D
            wg = w[:, nmain:]
            w_ko = jnp.concatenate([w[:, dq:2 * dq], w[:, 2 * dq + D:nmain]], axis=1)
            w_qv = jnp.concatenate([w[:, :dq], w[:, 2 * dq:2 * dq + D]], axis=1)
            qt, k, vt, o, gcol, grow = _inproj_call(
                x, sh1, sc1, row(mix_pre_g[l]),
                w_ko.astype(BF16), w_qv.T.astype(BF16), _pad_cols(wg, LANES).astype(BF16),
                _pad_cols(a_gate_b[l][None, :], LANES), dq=dq, dm=D, heads=m_heads)
            y = _mlstm_call(qt, k, vt, o, gcol, grow, row(a_hnorm_g[l]), heads=m_heads)
            w_o = a_w_out[l]
        else:
            if l == n_a:
                kv_sh, kv_sc = vecs(kv_mods, 2)
                wf = kv_w[:, 2 * D:]
                shared = dict(
                    gkv=row(kv_norm_g), shkv=kv_sh, sckv=kv_sc,
                    wk=kv_w[:, :D].astype(BF16), wv=kv_w[:, D:2 * D].T.astype(BF16),
                    wf_col=_pad_cols(wf, LANES).astype(BF16), bf_col=_pad_cols(kv_fgate_b[None, :], LANES))
            qt, k_new, v_new, fcol_new, frow_new = _qkv_call(
                x, row(mix_pre_g[l]), sh1, sc1, shared["gkv"], shared["shkv"], shared["sckv"],
                b_w_q[l - n_a].T.astype(BF16), shared["wk"], shared["wv"],
                shared["wf_col"], shared["bf_col"], heads=f_heads)
            if l == n_a:
                k_sh, v_sh, fcol, frow = k_new, v_new, fcol_new, frow_new
            y = _fox_call(qt, k_sh, v_sh, fcol, frow, heads=f_heads)
            w_o = b_w_out[l - n_a]
        w_in = ffn_w_in[l]
        x = _mix_ffn_call(
            y, x, w_o.astype(BF16), row(mix_post_g[l]), g1, row(ffn_pre_g[l]), sh2, sc2, g2,
            row(ffn_post_g[l]), w_in[:, :dff].astype(BF16), w_in[:, dff:].astype(BF16),
            ffn_w_out[l].astype(BF16))
    return x
```

```python
import functools

import jax
import jax.numpy as jnp
from jax import lax
from jax.experimental import pallas as pl
from jax.experimental.pallas import tpu as pltpu

F32 = jnp.float32
BF16 = jnp.bfloat16

EPS = 1e-6
LOG2E = 1.4426950408889634
GATE_CAP = 15.0
LANES = 128
MLSTM_CHUNK = 256
ROW_TILE = 512
FFN_ROW_TILE = 512
ATTN_TILE = 512
FFN_COL_CHUNK = 256
PV_EXTRA_ROWS = 16
MLSTM_EXTRA_ROWS = 16
VMEM_LIMIT = 56 * 1024 * 1024


def _rms(x):
    return x * lax.rsqrt(jnp.mean(x * x, axis=-1, keepdims=True) + EPS)


def _log_sigmoid(x):
    return jnp.minimum(x, 0.0) - jnp.log(1.0 + jnp.exp(-jnp.abs(x)))


def _softcap(a):
    return GATE_CAP * jnp.tanh(a / GATE_CAP)


def _split3(x):
    hi = x.astype(BF16)
    r1 = x - hi.astype(F32)
    mid = r1.astype(BF16)
    lo = (r1 - mid.astype(F32)).astype(BF16)
    return hi, mid, lo


def _cumsum_rows(tri_lower, x):
    return sum(jnp.dot(tri_lower, p, preferred_element_type=F32) for p in _split3(x))


def _cumsum_lanes(x, tri_upper):
    return sum(jnp.dot(p, tri_upper, preferred_element_type=F32) for p in _split3(x))


def _tri(n):
    r = lax.broadcasted_iota(jnp.int32, (n, n), 0)
    c = lax.broadcasted_iota(jnp.int32, (n, n), 1)
    return r >= c, (r <= c)


def _chunk_spec(chunk, width):
    k = chunk[1]
    return pl.BlockSpec((1, 1, width), lambda b, i: (b, 0, k))


def _resident(shape):
    nd = len(shape)
    return pl.BlockSpec(shape, lambda *_: (0,) * nd, pipeline_mode=pl.Buffered(1))


def _mod_kernel(c_ref, w_ref, b_ref, o_ref):
    c = c_ref[...]
    cs = (c * jax.nn.sigmoid(c)).astype(BF16)
    o_ref[0] = jnp.dot(cs, w_ref[0].astype(BF16), preferred_element_type=F32) + b_ref[0]


def _mod_call(c, w, b):
    G, D, N = w.shape
    B = c.shape[0]
    tn = 1024
    return pl.pallas_call(
        _mod_kernel,
        grid=(G, N // tn),
        in_specs=[
            pl.BlockSpec((B, D), lambda g, j: (0, 0)),
            pl.BlockSpec((1, D, tn), lambda g, j: (g, 0, j)),
            pl.BlockSpec((1, 1, tn), lambda g, j: (g, 0, j)),
        ],
        out_specs=pl.BlockSpec((1, B, tn), lambda g, j: (g, 0, j)),
        out_shape=jax.ShapeDtypeStruct((G, B, N), F32),
        name="adaln_mod",
    )(c, w, b.reshape(G, 1, N))


def _inproj_kernel(x_ref, sh_ref, sc_ref, g_ref, wko_ref, wqvt_ref, wgc_ref, bc_ref,
                   qt_ref, k_ref, vt_ref, o_ref, gcol_ref, grow_ref, *, dq, dm, heads, qscale):
    x = x_ref[0]
    tm = x.shape[0]
    h = (_rms(x) * g_ref[...] * (1.0 + sc_ref[0]) + sh_ref[0]).astype(BF16)
    nt = (((1,), (1,)), ((), ()))
    ko = jnp.dot(h, wko_ref[...], preferred_element_type=F32)
    k_ref[0] = ko[:, 0:dq].astype(BF16)
    o_ref[0] = ko[:, dq:dq + dm].astype(BF16)
    qvt = lax.dot_general(wqvt_ref[...], h, nt, preferred_element_type=F32)
    qt_ref[0] = (qvt[0:dq] * qscale).astype(BF16)
    dv = dm // heads
    dvx = dv + MLSTM_EXTRA_ROWS
    ones_row = (lax.broadcasted_iota(jnp.int32, (MLSTM_EXTRA_ROWS, tm), 0) == 0).astype(BF16)
    for hh in range(heads):
        vt_ref[0, hh * dvx:hh * dvx + dv, :] = qvt[dq + hh * dv:dq + (hh + 1) * dv].astype(BF16)
        vt_ref[0, hh * dvx + dv:(hh + 1) * dvx, :] = ones_row
    gc = _softcap(jnp.dot(h, wgc_ref[...], preferred_element_type=F32) + bc_ref[...])
    lane = lax.broadcasted_iota(jnp.int32, gc.shape, 1)
    g2 = jnp.where(lane < heads, gc, _log_sigmoid(gc)) * LOG2E
    gcol_ref[0] = g2
    grow_ref[0] = g2.T[0:grow_ref.shape[1]]


def _inproj_call(x, sh, sc, g, w_ko, w_qv_t, wg_col, b_col, *, dq, dm, heads):
    B, S, D = x.shape
    tm = min(ROW_TILE, S)
    dvx_all = dm + heads * MLSTM_EXTRA_ROWS
    row = lambda b, i: (b, i, 0)
    col = lambda b, i: (b, 0, i)
    kern = functools.partial(_inproj_kernel, dq=dq, dm=dm, heads=heads, qscale=(dq // heads) ** -0.5)
    return pl.pallas_call(
        kern,
        grid=(B, S // tm),
        in_specs=[
            pl.BlockSpec((1, tm, D), row),
            _chunk_spec(sh, D),
            _chunk_spec(sc, D),
            _resident((1, D)),
            _resident((D, dq + dm)),
            _resident((dq + dm, D)),
            _resident((D, LANES)),
            _resident((1, LANES)),
        ],
        out_specs=[
            pl.BlockSpec((1, dq, tm), col),
            pl.BlockSpec((1, tm, dq), row),
            pl.BlockSpec((1, dvx_all, tm), col),
            pl.BlockSpec((1, tm, dm), row),
            pl.BlockSpec((1, tm, LANES), row),
            pl.BlockSpec((1, 16, tm), col),
        ],
        out_shape=[
            jax.ShapeDtypeStruct((B, dq, S), BF16),
            jax.ShapeDtypeStruct((B, S, dq), BF16),
            jax.ShapeDtypeStruct((B, dvx_all, S), BF16),
            jax.ShapeDtypeStruct((B, S, dm), BF16),
            jax.ShapeDtypeStruct((B, S, LANES), F32),
            jax.ShapeDtypeStruct((B, 16, S), F32),
        ],
        compiler_params=pltpu.CompilerParams(
            dimension_semantics=("arbitrary", "arbitrary"), vmem_limit_bytes=VMEM_LIMIT),
        name="mlstm_inproj",
    )(x, sh[0], sc[0], g, w_ko, w_qv_t, wg_col, b_col)


def _mlstm_kernel(qt_ref, k_ref, vt_ref, o_ref, gcol_ref, grow_ref, hn_ref, y_ref,
                  st_ref, m_ref, *, heads, dk, dv, L):
    @pl.when(pl.program_id(1) == 0)
    def _():
        st_ref[...] = jnp.zeros_like(st_ref)
        m_ref[...] = jnp.zeros_like(m_ref)

    gcol = gcol_ref[0]
    grow = grow_ref[0]
    lower, upper = _tri(L)
    bcol = _cumsum_rows(lower.astype(BF16), gcol)
    brow = _cumsum_lanes(grow, upper.astype(BF16))
    dvx = dv + MLSTM_EXTRA_ROWS
    dot = functools.partial(jnp.dot, preferred_element_type=F32)

    hs = range(heads)
    i_row = [grow[h:h + 1, :] for h in hs]
    b_row = [brow[heads + h:heads + h + 1, :] for h in hs]
    i_col = [gcol[:, h:h + 1] for h in hs]
    b_col = [bcol[:, heads + h:heads + h + 1] for h in hs]
    bl = [b[:, L - 1:L] for b in b_row]
    m = [m_ref[h][0:1, 0:1] for h in hs]
    qt = [qt_ref[0, h * dk:(h + 1) * dk, :] for h in hs]
    kh = [k_ref[0, :, h * dk:(h + 1) * dk] for h in hs]
    vt = [vt_ref[0, h * dvx:(h + 1) * dvx, :] for h in hs]
    st = [st_ref[h] for h in hs]

    qk = [dot(kh[h], qt[h]) for h in hs]
    sq = [dot(st[h].astype(BF16), qt[h]) for h in hs]
    m_t, a, w = [], [], []
    for h in hs:
        dmat = jnp.where(upper, b_row[h] + (i_col[h] - b_col[h]), -jnp.inf)
        inter = b_row[h] + m[h]
        m_t.append(jnp.maximum(inter, jnp.max(dmat, axis=0, keepdims=True)))
        a.append(jnp.exp2(inter - m_t[h]))
        w.append(jnp.exp2(dmat - m_t[h]))
    na = [a[h] * sq[h] + dot(vt[h], (qk[h] * w[h]).astype(BF16)) for h in hs]
    for h in hs:
        ws = bl[h] - b_col[h] + i_col[h]
        m_new = jnp.maximum(bl[h] + m[h], jnp.max(ws, axis=0, keepdims=True))
        kw = (kh[h].astype(F32) * jnp.exp2(ws - m_new)).astype(BF16)
        st_ref[h] = jnp.exp2(bl[h] + m[h] - m_new) * st[h] + dot(vt[h], kw)
        m_ref[h] = jnp.broadcast_to(m_new, m_ref.shape[1:])

        hT = na[h][0:dv] * (1.0 / jnp.maximum(jnp.abs(na[h][dv:dv + 1]), jnp.exp2(-m_t[h])))
        hnT = hT * lax.rsqrt(jnp.mean(hT * hT, axis=0, keepdims=True) + EPS)
        og = o_ref[0, :, h * dv:(h + 1) * dv].astype(F32)
        y_ref[0, :, h * dv:(h + 1) * dv] = (jax.nn.sigmoid(og) * (hnT.T * hn_ref[:, h * dv:(h + 1) * dv])).astype(BF16)


def _mlstm_call(qt, k, vt, o, gcol, grow, hn_g, *, heads):
    B, S, dq = k.shape
    dm = o.shape[-1]
    dk, dv = dq // heads, dm // heads
    dvx_all = vt.shape[1]
    L = min(MLSTM_CHUNK, S)
    row = lambda b, c: (b, c, 0)
    col = lambda b, c: (b, 0, c)
    kern = functools.partial(_mlstm_kernel, heads=heads, dk=dk, dv=dv, L=L)
    return pl.pallas_call(
        kern,
        grid=(B, S // L),
        in_specs=[
            pl.BlockSpec((1, dq, L), col),
            pl.BlockSpec((1, L, dq), row),
            pl.BlockSpec((1, dvx_all, L), col),
            pl.BlockSpec((1, L, dm), row),
            pl.BlockSpec((1, L, LANES), row),
            pl.BlockSpec((1, 16, L), col),
            _resident((1, dm)),
        ],
        out_specs=pl.BlockSpec((1, L, dm), row),
        out_shape=jax.ShapeDtypeStruct((B, S, dm), BF16),
        scratch_shapes=[
            pltpu.VMEM((heads, dv + MLSTM_EXTRA_ROWS, dk), F32),
            pltpu.VMEM((heads, 8, LANES), F32),
        ],
        compiler_params=pltpu.CompilerParams(
            dimension_semantics=("arbitrary", "arbitrary"), vmem_limit_bytes=VMEM_LIMIT),
        name="mlstm_recurrence",
    )(qt, k, vt, o, gcol, grow, hn_g)


def _mix_ffn_kernel(y_ref, x_ref, wo_ref, pg_ref, g1_ref, fg_ref, sh2_ref, sc2_ref, g2_ref, fpg_ref,
                    wg_ref, wu_ref, wd_ref, out_ref, act_ref, *, dff, y_transposed):
    dot = functools.partial(jnp.dot, preferred_element_type=F32)
    tm = x_ref.shape[1]
    cw = FFN_COL_CHUNK
    nchunks = dff // cw
    rows = [slice(r * (tm // 2), (r + 1) * (tm // 2)) for r in range(2)]

    def mix(r):
        if y_transposed:
            t = lax.dot_general(y_ref[0, :, rows[r]], wo_ref[...], (((0,), (0,)), ((), ())),
                                preferred_element_type=F32)
        else:
            t = dot(y_ref[0, rows[r], :], wo_ref[...])
        x1 = x_ref[0, rows[r], :] + g1_ref[0] * (_rms(t) * pg_ref[...])
        return x1, (_rms(x1) * fg_ref[...] * (1.0 + sc2_ref[0]) + sh2_ref[0]).astype(BF16)

    def up(r, h2, j):
        g = dot(h2, wg_ref[:, j * cw:(j + 1) * cw])
        u = dot(h2, wu_ref[:, j * cw:(j + 1) * cw])
        act_ref[rows[r], j * cw:(j + 1) * cw] = (g * jax.nn.sigmoid(g) * u).astype(BF16)

    def down(r, x1):
        f = dot(act_ref[rows[r], :], wd_ref[...])
        out_ref[0, rows[r], :] = x1 + g2_ref[0] * (_rms(f) * fpg_ref[...])

    x1a, h2a = mix(0)
    up(0, h2a, 0)
    x1b, h2b = mix(1)
    for j in range(1, nchunks):
        up(0, h2a, j)
    for j in range(nchunks):
        up(1, h2b, j)
        if j == nchunks // 2:
            down(0, x1a)
    down(1, x1b)


def _mix_ffn_call(y, x, w_o, post_g, g1, ffn_pre_g, sh2, sc2, g2, ffn_post_g, w_gate, w_up, w_down,
                  *, y_transposed=False):
    B, S, D = x.shape
    dff = w_down.shape[0]
    tm = min(FFN_ROW_TILE, S)
    row = lambda b, i: (b, i, 0)
    return pl.pallas_call(
        functools.partial(_mix_ffn_kernel, dff=dff, y_transposed=y_transposed),
        grid=(B, S // tm),
        in_specs=[
            pl.BlockSpec((1, D, tm), lambda b, i: (b, 0, i)) if y_transposed else pl.BlockSpec((1, tm, D), row),
            pl.BlockSpec((1, tm, D), row),
            _resident((D, D)),
            _resident((1, D)),
            _chunk_spec(g1, D),
            _resident((1, D)),
            _chunk_spec(sh2, D), _chunk_spec(sc2, D), _chunk_spec(g2, D),
            _resident((1, D)),
            _resident((D, dff)),
            _resident((D, dff)),
            _resident((dff, D)),
        ],
        out_specs=pl.BlockSpec((1, tm, D), row),
        out_shape=jax.ShapeDtypeStruct((B, S, D), F32),
        scratch_shapes=[pltpu.VMEM((tm, dff), BF16)],
        compiler_params=pltpu.CompilerParams(
            dimension_semantics=("arbitrary", "arbitrary"), vmem_limit_bytes=VMEM_LIMIT),
        name="mix_ffn",
    )(y, x, w_o, post_g, g1[0], ffn_pre_g, sh2[0], sc2[0], g2[0], ffn_post_g, w_gate, w_up, w_down)


def _qkv_kernel(x_ref, gq_ref, shq_ref, scq_ref, gkv_ref, shkv_ref, sckv_ref,
                wqt_ref, wk_ref, wvt_ref, wfc_ref, bfc_ref,
                qt_ref, k_ref, vt_ref, fcol_ref, frow_ref, ccol_ref, *, qscale, hd):
    @pl.when(pl.program_id(1) == 0)
    def _():
        ccol_ref[...] = jnp.zeros_like(ccol_ref)

    dot = functools.partial(jnp.dot, preferred_element_type=F32)
    xn = _rms(x_ref[0])
    hq = (xn * gq_ref[...] * (1.0 + scq_ref[0]) + shq_ref[0]).astype(BF16)
    qt_ref[0] = (lax.dot_general(wqt_ref[...], hq, (((1,), (1,)), ((), ())), preferred_element_type=F32)
                 * qscale).astype(BF16)
    hkv = (xn * gkv_ref[...] * (1.0 + sckv_ref[0]) + shkv_ref[0]).astype(BF16)
    k_ref[0] = dot(hkv, wk_ref[...]).astype(BF16)
    vT = lax.dot_general(wvt_ref[...], hkv, (((1,), (1,)), ((), ())), preferred_element_type=F32)
    tm = x_ref.shape[1]
    hdx = hd + PV_EXTRA_ROWS
    ones_row = (lax.broadcasted_iota(jnp.int32, (PV_EXTRA_ROWS, tm), 0) == 0).astype(BF16)
    for h in range(vT.shape[0] // hd):
        vt_ref[0, h * hdx:h * hdx + hd, :] = vT[h * hd:(h + 1) * hd].astype(BF16)
        vt_ref[0, h * hdx + hd:(h + 1) * hdx, :] = ones_row
    lower, _ = _tri(tm)
    lf_col = _log_sigmoid(dot(hkv, wfc_ref[...]) + bfc_ref[...])
    f_col = _cumsum_rows(lower.astype(BF16), lf_col) + ccol_ref[0:1, :]
    ccol_ref[...] = jnp.broadcast_to(f_col[tm - 1:tm, :], ccol_ref.shape)
    f2 = f_col * LOG2E
    fcol_ref[0] = f2
    frow_ref[0] = f2.T[0:frow_ref.shape[1]]


def _qkv_call(x, gq, shq, scq, gkv, shkv, sckv, wq_t, wk, wv_t, wf_col, bf_col, *, heads):
    B, S, D = x.shape
    tm = min(ROW_TILE, S)
    hd = D // heads
    vt_rows = heads * (hd + PV_EXTRA_ROWS)
    row = lambda b, i: (b, i, 0)
    col = lambda b, i: (b, 0, i)
    return pl.pallas_call(
        functools.partial(_qkv_kernel, qscale=hd ** -0.5 * LOG2E, hd=hd),
        grid=(B, S // tm),
        in_specs=[
            pl.BlockSpec((1, tm, D), row),
            _resident((1, D)), _chunk_spec(shq, D), _chunk_spec(scq, D),
            _resident((1, D)), _chunk_spec(shkv, D), _chunk_spec(sckv, D),
            _resident((D, D)), _resident((D, D)), _resident((D, D)),
            _resident((D, LANES)), _resident((1, LANES)),
        ],
        out_specs=[
            pl.BlockSpec((1, D, tm), col),
            pl.BlockSpec((1, tm, D), row),
            pl.BlockSpec((1, vt_rows, tm), col),
            pl.BlockSpec((1, tm, LANES), row),
            pl.BlockSpec((1, heads, tm), col),
        ],
        out_shape=[
            jax.ShapeDtypeStruct((B, D, S), BF16),
            jax.ShapeDtypeStruct((B, S, D), BF16),
            jax.ShapeDtypeStruct((B, vt_rows, S), BF16),
            jax.ShapeDtypeStruct((B, S, LANES), F32),
            jax.ShapeDtypeStruct((B, heads, S), F32),
        ],
        scratch_shapes=[pltpu.VMEM((8, LANES), F32)],
        compiler_params=pltpu.CompilerParams(
            dimension_semantics=("arbitrary", "arbitrary"), vmem_limit_bytes=VMEM_LIMIT),
        name="fox_qkv",
    )(x, gq, shq[0], scq[0], gkv, shkv[0], sckv[0], wq_t, wk, wv_t, wf_col, bf_col)


def _fox_kernel(qt_ref, k_ref, vt_ref, fcol_ref, frow_ref, o_ref, fkb_ref, acc_ref, qm_ref,
                a0_ref, a1_ref, p0_ref, p1_ref, *, hd, tq, tk):
    hp = pl.program_id(1)
    S = k_ref.shape[1]
    nq = S // tq
    hdx = hd + PV_EXTRA_ROWS
    lane = lax.broadcasted_iota(jnp.int32, (1, LANES), 1)
    sub = lax.broadcasted_iota(jnp.int32, (LANES, 1), 0)
    a_ref = (a0_ref, a1_ref)
    p_ref = (p0_ref, p1_ref)

    def spread_fk(c):
        rows = pl.ds(pl.multiple_of(c * tk, tk), tk)
        fc = fcol_ref[0, rows, :]
        for h in range(2):
            col = jnp.sum(jnp.where(lane == 2 * hp + h, fc, 0.0), axis=1, keepdims=True)
            fkb_ref[h, rows, :] = jnp.broadcast_to(col, (tk, LANES))

    def masked_q(i, h):
        qT = qt_ref[0, :, pl.ds(pl.multiple_of(i * tq, tq), tq)].astype(F32)
        return jnp.where((sub < hd) == (h == 0), qT, 0.0).astype(BF16)

    def qk(j, h, qm=None):
        start = pl.multiple_of(j * tk, tk)
        fk = fkb_ref[h, pl.ds(start, tk), :]
        a_ref[h][:, 0:tq] = (jnp.dot(k_ref[0, pl.ds(start, tk), :], qm_ref[h] if qm is None else qm,
                                     preferred_element_type=F32)
                             - jnp.concatenate([fk] * (tq // LANES), axis=1))

    def softmax(h, m_prev, fq, diagonal):
        a = a_ref[h][:, 0:tq]
        if diagonal:
            r = lax.broadcasted_iota(jnp.int32, (tk, tq), 0)
            c = lax.broadcasted_iota(jnp.int32, (tk, tq), 1)
            a = jnp.where(r <= c, a, -jnp.inf)
        m_new = jnp.maximum(m_prev, jnp.max(a, axis=0, keepdims=True) + fq)
        alpha = jnp.exp2(m_prev - m_new)
        p_ref[h][:, 0:tq] = jnp.exp2(a - (m_new - fq)).astype(BF16)
        return m_new, alpha

    def pv(j, h, alpha):
        start = pl.multiple_of(j * tk, tk)
        v2t = vt_ref[0, h * hdx:(h + 1) * hdx, pl.ds(start, tk)]
        acc_ref[h] = alpha * acc_ref[h] + jnp.dot(v2t, p_ref[h][:, 0:tq], preferred_element_type=F32)

    def body(j, carry, fq, diagonal, deferred=True):
        m0, m1, alpha1 = carry
        if deferred:
            pv(j - 1, 1, alpha1)
        m0, alpha0 = softmax(0, m0, fq[0], diagonal)
        qk(j, 1)
        pv(j, 0, alpha0)
        if not diagonal:
            qk(j + 1, 0)
        m1, alpha1 = softmax(1, m1, fq[1], diagonal)
        return m0, m1, alpha1

    def query_block(i, first):
        qstart = pl.multiple_of(i * tq, tq)
        for h in range(2):
            qm_ref[h] = masked_q(i, h)
        fq = [frow_ref[0, 0, h:h + 1, pl.ds(qstart, tq)] for h in range(2)]
        acc_ref[...] = jnp.zeros(acc_ref.shape, F32)
        neg = jnp.full((1, tq), -jnp.inf, F32)
        carry = (neg, neg, jnp.ones((1, tq), F32))
        if first:
            _, _, alpha1 = body(i, carry, fq, True, deferred=False)
        else:
            carry = body(0, carry, fq, False, deferred=False)
            carry = lax.fori_loop(1, i, lambda j, c: body(j, c, fq, False), carry)
            _, _, alpha1 = body(i, carry, fq, True)
        qk(0, 0, masked_q(jnp.minimum(i + 1, nq - 1), 0))
        spread_fk(jnp.minimum(i + 1, nq - 1))
        pv(i, 1, alpha1)
        ot = jnp.concatenate([acc_ref[h, 0:hd, :] / acc_ref[h, hd:hd + 1, :] for h in range(2)], axis=0)
        o_ref[0, :, pl.ds(qstart, tq)] = ot.astype(BF16)

    spread_fk(0)
    qk(0, 0, masked_q(0, 0))
    query_block(jnp.int32(0), True)

    def later_block(i, carry):
        query_block(i, False)
        return carry

    lax.fori_loop(1, nq, later_block, 0)


def _fox_call(qt, k, vt, fcol, frow, *, heads):
    B, S, D = k.shape
    hd = D // heads
    assert 2 * hd == LANES
    t = min(ATTN_TILE, S)
    frow = frow.reshape(B, heads // 2, 2, S)
    return pl.pallas_call(
        functools.partial(_fox_kernel, hd=hd, tq=t, tk=t),
        grid=(B, heads // 2),
        in_specs=[
            pl.BlockSpec((1, LANES, S), lambda b, p: (b, p, 0)),
            pl.BlockSpec((1, S, LANES), lambda b, p: (b, 0, p)),
            pl.BlockSpec((1, 2 * (hd + PV_EXTRA_ROWS), S), lambda b, p: (b, p, 0)),
            pl.BlockSpec((1, S, LANES), lambda b, p: (b, 0, 0)),
            pl.BlockSpec((1, 1, 2, S), lambda b, p: (b, p, 0, 0)),
        ],
        out_specs=pl.BlockSpec((1, LANES, S), lambda b, p: (b, p, 0)),
        out_shape=jax.ShapeDtypeStruct((B, D, S), BF16),
        scratch_shapes=[
            pltpu.VMEM((2, S, LANES), F32),
            pltpu.VMEM((2, hd + PV_EXTRA_ROWS, t), F32),
            pltpu.VMEM((2, LANES, t), BF16),
            pltpu.VMEM((t, t + LANES), F32), pltpu.VMEM((t, t + LANES), F32),
            pltpu.VMEM((t, t + LANES), BF16), pltpu.VMEM((t, t + LANES), BF16),
        ],
        compiler_params=pltpu.CompilerParams(
            dimension_semantics=("arbitrary", "arbitrary"), vmem_limit_bytes=VMEM_LIMIT),
        name="fox_attention",
    )(qt, k, vt, fcol, frow)


def _pad_cols(w, n):
    return jnp.pad(w, ((0, 0), (0, n - w.shape[1])))


def kernel(x, c, mod_w, mod_b, mix_pre_g, mix_post_g, ffn_pre_g, ffn_post_g, ffn_w_in, ffn_w_out,
           a_w_in, a_gate_b, a_hnorm_g, a_w_out, kv_norm_g, kv_mod_w, kv_mod_b, kv_w, kv_fgate_b,
           b_w_q, b_w_out):
    B, S, D = x.shape
    depth = mod_w.shape[0]
    n_a = a_w_in.shape[0]
    m_heads = a_gate_b.shape[1] // 2
    f_heads = kv_fgate_b.shape[0]
    dff = ffn_w_out.shape[1]
    dq = (a_w_in.shape[2] - 2 * D - 2 * m_heads) // 2

    mods = _mod_call(c, mod_w, mod_b)
    kv_mods = _mod_call(c, kv_mod_w[None], kv_mod_b[None])[0]
    vecs = lambda m, n: [(m.reshape(B, 1, n * D), k) for k in range(n)]
    row = lambda g: g.reshape(1, D)

    k_sh = v_sh = fcol = frow = None
    for l in range(depth):
        sh1, sc1, g1, sh2, sc2, g2 = vecs(mods[l], 6)
        if l < n_a:
            w = a_w_in[l]
            nmain = 2 * dq + 2 * D
            wg = w[:, nmain:]
            w_ko = jnp.concatenate([w[:, dq:2 * dq], w[:, 2 * dq + D:nmain]], axis=1)
            w_qv = jnp.concatenate([w[:, :dq], w[:, 2 * dq:2 * dq + D]], axis=1)
            qt, k, vt, o, gcol, grow = _inproj_call(
                x, sh1, sc1, row(mix_pre_g[l]),
                w_ko.astype(BF16), w_qv.T.astype(BF16), _pad_cols(wg, LANES).astype(BF16),
                _pad_cols(a_gate_b[l][None, :], LANES), dq=dq, dm=D, heads=m_heads)
            y = _mlstm_call(qt, k, vt, o, gcol, grow, row(a_hnorm_g[l]), heads=m_heads)
            w_o = a_w_out[l]
        else:
            if l == n_a:
                kv_sh, kv_sc = vecs(kv_mods, 2)
                wf = kv_w[:, 2 * D:]
                shared = dict(
                    gkv=row(kv_norm_g), shkv=kv_sh, sckv=kv_sc,
                    wk=kv_w[:, :D].astype(BF16), wv=kv_w[:, D:2 * D].T.astype(BF16),
                    wf_col=_pad_cols(wf, LANES).astype(BF16), bf_col=_pad_cols(kv_fgate_b[None, :], LANES))
            qt, k_new, v_new, fcol_new, frow_new = _qkv_call(
                x, row(mix_pre_g[l]), sh1, sc1, shared["gkv"], shared["shkv"], shared["sckv"],
                b_w_q[l - n_a].T.astype(BF16), shared["wk"], shared["wv"],
                shared["wf_col"], shared["bf_col"], heads=f_heads)
            if l == n_a:
                k_sh, v_sh, fcol, frow = k_new, v_new, fcol_new, frow_new
            y = _fox_call(qt, k_sh, v_sh, fcol, frow, heads=f_heads)
            w_o = b_w_out[l - n_a]
        w_in = ffn_w_in[l]
        x = _mix_ffn_call(
            y, x, w_o.astype(BF16), row(mix_post_g[l]), g1, row(ffn_pre_g[l]), sh2, sc2, g2,
            row(ffn_post_g[l]), w_in[:, :dff].astype(BF16), w_in[:, dff:].astype(BF16),
            ffn_w_out[l].astype(BF16), y_transposed=l >= n_a)
    return x
```

```python
import functools

import jax
import jax.numpy as jnp
from jax import lax
from jax.experimental import pallas as pl
from jax.experimental.pallas import tpu as pltpu

F32 = jnp.float32
BF16 = jnp.bfloat16

EPS = 1e-6
LOG2E = 1.4426950408889634
GATE_CAP = 15.0
LANES = 128
MLSTM_CHUNK = 256
ROW_TILE = 512
FFN_ROW_TILE = 512
ATTN_TILE = 512
FFN_COL_CHUNK = 256
PV_EXTRA_ROWS = 16
MLSTM_EXTRA_ROWS = 16
VMEM_LIMIT = 56 * 1024 * 1024


def _rms(x):
    return x * lax.rsqrt(jnp.mean(x * x, axis=-1, keepdims=True) + EPS)


def _log_sigmoid(x):
    return jnp.minimum(x, 0.0) - jnp.log(1.0 + jnp.exp(-jnp.abs(x)))


def _softcap(a):
    return GATE_CAP * jnp.tanh(a / GATE_CAP)


def _split3(x):
    hi = x.astype(BF16)
    r1 = x - hi.astype(F32)
    mid = r1.astype(BF16)
    lo = (r1 - mid.astype(F32)).astype(BF16)
    return hi, mid, lo


def _cumsum_rows(tri_lower, x):
    return sum(jnp.dot(tri_lower, p, preferred_element_type=F32) for p in _split3(x))


def _cumsum_lanes(x, tri_upper):
    return sum(jnp.dot(p, tri_upper, preferred_element_type=F32) for p in _split3(x))


def _tri(n):
    r = lax.broadcasted_iota(jnp.int32, (n, n), 0)
    c = lax.broadcasted_iota(jnp.int32, (n, n), 1)
    return r >= c, (r <= c)


def _chunk_spec(chunk, width):
    k = chunk[1]
    return pl.BlockSpec((1, 1, width), lambda b, i: (b, 0, k))


def _resident(shape):
    nd = len(shape)
    return pl.BlockSpec(shape, lambda *_: (0,) * nd, pipeline_mode=pl.Buffered(1))


def _mod_kernel(c_ref, w_ref, b_ref, o_ref):
    c = c_ref[...]
    cs = (c * jax.nn.sigmoid(c)).astype(BF16)
    o_ref[0] = jnp.dot(cs, w_ref[0].astype(BF16), preferred_element_type=F32) + b_ref[0]


def _mod_call(c, w, b):
    G, D, N = w.shape
    B = c.shape[0]
    tn = 1024
    return pl.pallas_call(
        _mod_kernel,
        grid=(G, N // tn),
        in_specs=[
            pl.BlockSpec((B, D), lambda g, j: (0, 0)),
            pl.BlockSpec((1, D, tn), lambda g, j: (g, 0, j)),
            pl.BlockSpec((1, 1, tn), lambda g, j: (g, 0, j)),
        ],
        out_specs=pl.BlockSpec((1, B, tn), lambda g, j: (g, 0, j)),
        out_shape=jax.ShapeDtypeStruct((G, B, N), F32),
        name="adaln_mod",
    )(c, w, b.reshape(G, 1, N))


def _inproj_kernel(x_ref, sh_ref, sc_ref, g_ref, wko_ref, wqvt_ref, wgc_ref, bc_ref,
                   qt_ref, k_ref, vt_ref, o_ref, gcol_ref, grow_ref, *, dq, dm, heads, qscale):
    x = x_ref[0]
    tm = x.shape[0]
    h = (_rms(x) * g_ref[...] * (1.0 + sc_ref[0]) + sh_ref[0]).astype(BF16)
    nt = (((1,), (1,)), ((), ()))
    ko = jnp.dot(h, wko_ref[...], preferred_element_type=F32)
    k_ref[0] = ko[:, 0:dq].astype(BF16)
    o_ref[0] = ko[:, dq:dq + dm].astype(BF16)
    qvt = lax.dot_general(wqvt_ref[...], h, nt, preferred_element_type=F32)
    qt_ref[0] = (qvt[0:dq] * qscale).astype(BF16)
    dv = dm // heads
    dvx = dv + MLSTM_EXTRA_ROWS
    ones_row = (lax.broadcasted_iota(jnp.int32, (MLSTM_EXTRA_ROWS, tm), 0) == 0).astype(BF16)
    for hh in range(heads):
        vt_ref[0, hh * dvx:hh * dvx + dv, :] = qvt[dq + hh * dv:dq + (hh + 1) * dv].astype(BF16)
        vt_ref[0, hh * dvx + dv:(hh + 1) * dvx, :] = ones_row
    gc = _softcap(jnp.dot(h, wgc_ref[...], preferred_element_type=F32) + bc_ref[...])
    lane = lax.broadcasted_iota(jnp.int32, gc.shape, 1)
    g2 = jnp.where(lane < heads, gc, _log_sigmoid(gc)) * LOG2E
    gcol_ref[0] = g2
    grow_ref[0] = g2.T[0:grow_ref.shape[1]]


def _inproj_call(x, sh, sc, g, w_ko, w_qv_t, wg_col, b_col, *, dq, dm, heads):
    B, S, D = x.shape
    tm = min(ROW_TILE, S)
    dvx_all = dm + heads * MLSTM_EXTRA_ROWS
    row = lambda b, i: (b, i, 0)
    col = lambda b, i: (b, 0, i)
    kern = functools.partial(_inproj_kernel, dq=dq, dm=dm, heads=heads, qscale=(dq // heads) ** -0.5)
    return pl.pallas_call(
        kern,
        grid=(B, S // tm),
        in_specs=[
            pl.BlockSpec((1, tm, D), row),
            _chunk_spec(sh, D),
            _chunk_spec(sc, D),
            _resident((1, D)),
            _resident((D, dq + dm)),
            _resident((dq + dm, D)),
            _resident((D, LANES)),
            _resident((1, LANES)),
        ],
        out_specs=[
            pl.BlockSpec((1, dq, tm), col),
            pl.BlockSpec((1, tm, dq), row),
            pl.BlockSpec((1, dvx_all, tm), col),
            pl.BlockSpec((1, tm, dm), row),
            pl.BlockSpec((1, tm, LANES), row),
            pl.BlockSpec((1, 16, tm), col),
        ],
        out_shape=[
            jax.ShapeDtypeStruct((B, dq, S), BF16),
            jax.ShapeDtypeStruct((B, S, dq), BF16),
            jax.ShapeDtypeStruct((B, dvx_all, S), BF16),
            jax.ShapeDtypeStruct((B, S, dm), BF16),
            jax.ShapeDtypeStruct((B, S, LANES), F32),
            jax.ShapeDtypeStruct((B, 16, S), F32),
        ],
        compiler_params=pltpu.CompilerParams(
            dimension_semantics=("arbitrary", "arbitrary"), vmem_limit_bytes=VMEM_LIMIT),
        name="mlstm_inproj",
    )(x, sh[0], sc[0], g, w_ko, w_qv_t, wg_col, b_col)


def _mlstm_kernel(qt_ref, k_ref, vt_ref, o_ref, gcol_ref, grow_ref, hn_ref, y_ref,
                  st_ref, m_ref, *, heads, dk, dv, L):
    @pl.when(pl.program_id(1) == 0)
    def _():
        st_ref[...] = jnp.zeros_like(st_ref)
        m_ref[...] = jnp.zeros_like(m_ref)

    gcol = gcol_ref[0]
    grow = grow_ref[0]
    lower, upper = _tri(L)
    bcol = _cumsum_rows(lower.astype(BF16), gcol)
    brow = _cumsum_lanes(grow, upper.astype(BF16))
    dvx = dv + MLSTM_EXTRA_ROWS
    dot = functools.partial(jnp.dot, preferred_element_type=F32)

    hs = range(heads)
    i_row = [grow[h:h + 1, :] for h in hs]
    b_row = [brow[heads + h:heads + h + 1, :] for h in hs]
    i_col = [gcol[:, h:h + 1] for h in hs]
    b_col = [bcol[:, heads + h:heads + h + 1] for h in hs]
    bl = [b[:, L - 1:L] for b in b_row]
    m = [m_ref[h][0:1, 0:1] for h in hs]
    qt = [qt_ref[0, h * dk:(h + 1) * dk, :] for h in hs]
    kh = [k_ref[0, :, h * dk:(h + 1) * dk] for h in hs]
    vt = [vt_ref[0, h * dvx:(h + 1) * dvx, :] for h in hs]
    st = [st_ref[h] for h in hs]

    qk = [dot(kh[h], qt[h]) for h in hs]
    sq = [dot(st[h].astype(BF16), qt[h]) for h in hs]
    m_t, a, w = [], [], []
    for h in hs:
        dmat = jnp.where(upper, b_row[h] + (i_col[h] - b_col[h]), -jnp.inf)
        inter = b_row[h] + m[h]
        m_t.append(jnp.maximum(inter, jnp.max(dmat, axis=0, keepdims=True)))
        a.append(jnp.exp2(inter - m_t[h]))
        w.append(jnp.exp2(dmat - m_t[h]))
    na = [a[h] * sq[h] + dot(vt[h], (qk[h] * w[h]).astype(BF16)) for h in hs]
    for h in hs:
        ws = bl[h] - b_col[h] + i_col[h]
        m_new = jnp.maximum(bl[h] + m[h], jnp.max(ws, axis=0, keepdims=True))
        kw = (kh[h].astype(F32) * jnp.exp2(ws - m_new)).astype(BF16)
        st_ref[h] = jnp.exp2(bl[h] + m[h] - m_new) * st[h] + dot(vt[h], kw)
        m_ref[h] = jnp.broadcast_to(m_new, m_ref.shape[1:])

        hT = na[h][0:dv] * (1.0 / jnp.maximum(jnp.abs(na[h][dv:dv + 1]), jnp.exp2(-m_t[h])))
        hnT = hT * lax.rsqrt(jnp.mean(hT * hT, axis=0, keepdims=True) + EPS)
        og = o_ref[0, :, h * dv:(h + 1) * dv].astype(F32)
        y_ref[0, :, h * dv:(h + 1) * dv] = (jax.nn.sigmoid(og) * (hnT.T * hn_ref[:, h * dv:(h + 1) * dv])).astype(BF16)


def _mlstm_call(qt, k, vt, o, gcol, grow, hn_g, *, heads):
    B, S, dq = k.shape
    dm = o.shape[-1]
    dk, dv = dq // heads, dm // heads
    dvx_all = vt.shape[1]
    L = min(MLSTM_CHUNK, S)
    row = lambda b, c: (b, c, 0)
    col = lambda b, c: (b, 0, c)
    kern = functools.partial(_mlstm_kernel, heads=heads, dk=dk, dv=dv, L=L)
    return pl.pallas_call(
        kern,
        grid=(B, S // L),
        in_specs=[
            pl.BlockSpec((1, dq, L), col),
            pl.BlockSpec((1, L, dq), row),
            pl.BlockSpec((1, dvx_all, L), col),
            pl.BlockSpec((1, L, dm), row),
            pl.BlockSpec((1, L, LANES), row),
            pl.BlockSpec((1, 16, L), col),
            _resident((1, dm)),
        ],
        out_specs=pl.BlockSpec((1, L, dm), row),
        out_shape=jax.ShapeDtypeStruct((B, S, dm), BF16),
        scratch_shapes=[
            pltpu.VMEM((heads, dv + MLSTM_EXTRA_ROWS, dk), F32),
            pltpu.VMEM((heads, 8, LANES), F32),
        ],
        compiler_params=pltpu.CompilerParams(
            dimension_semantics=("arbitrary", "arbitrary"), vmem_limit_bytes=VMEM_LIMIT),
        name="mlstm_recurrence",
    )(qt, k, vt, o, gcol, grow, hn_g)


def _mix_ffn_kernel(y_ref, x_ref, wo_ref, pg_ref, g1_ref, fg_ref, sh2_ref, sc2_ref, g2_ref, fpg_ref,
                    wg_ref, wu_ref, wd_ref, out_ref, act_ref, *, dff, y_transposed):
    dot = functools.partial(jnp.dot, preferred_element_type=F32)
    if y_transposed:
        t = lax.dot_general(y_ref[0], wo_ref[...], (((0,), (0,)), ((), ())), preferred_element_type=F32)
    else:
        t = dot(y_ref[0], wo_ref[...])
    x1 = x_ref[0] + g1_ref[0] * (_rms(t) * pg_ref[...])
    h2 = (_rms(x1) * fg_ref[...] * (1.0 + sc2_ref[0]) + sh2_ref[0]).astype(BF16)
    cw = FFN_COL_CHUNK
    for j in range(dff // cw):
        g = dot(h2, wg_ref[:, j * cw:(j + 1) * cw])
        u = dot(h2, wu_ref[:, j * cw:(j + 1) * cw])
        act_ref[:, j * cw:(j + 1) * cw] = (g * jax.nn.sigmoid(g) * u).astype(BF16)
    f = dot(act_ref[...], wd_ref[...])
    out_ref[0] = x1 + g2_ref[0] * (_rms(f) * fpg_ref[...])


def _mix_ffn_call(y, x, w_o, post_g, g1, ffn_pre_g, sh2, sc2, g2, ffn_post_g, w_gate, w_up, w_down,
                  *, y_transposed=False):
    B, S, D = x.shape
    dff = w_down.shape[0]
    tm = min(FFN_ROW_TILE, S)
    row = lambda b, i: (b, i, 0)
    return pl.pallas_call(
        functools.partial(_mix_ffn_kernel, dff=dff, y_transposed=y_transposed),
        grid=(B, S // tm),
        in_specs=[
            pl.BlockSpec((1, D, tm), lambda b, i: (b, 0, i)) if y_transposed else pl.BlockSpec((1, tm, D), row),
            pl.BlockSpec((1, tm, D), row),
            _resident((D, D)),
            _resident((1, D)),
            _chunk_spec(g1, D),
            _resident((1, D)),
            _chunk_spec(sh2, D), _chunk_spec(sc2, D), _chunk_spec(g2, D),
            _resident((1, D)),
            _resident((D, dff)),
            _resident((D, dff)),
            _resident((dff, D)),
        ],
        out_specs=pl.BlockSpec((1, tm, D), row),
        out_shape=jax.ShapeDtypeStruct((B, S, D), F32),
        scratch_shapes=[pltpu.VMEM((tm, dff), BF16)],
        compiler_params=pltpu.CompilerParams(
            dimension_semantics=("arbitrary", "arbitrary"), vmem_limit_bytes=VMEM_LIMIT),
        name="mix_ffn",
    )(y, x, w_o, post_g, g1[0], ffn_pre_g, sh2[0], sc2[0], g2[0], ffn_post_g, w_gate, w_up, w_down)


def _qkv_kernel(x_ref, gq_ref, shq_ref, scq_ref, gkv_ref, shkv_ref, sckv_ref,
                wqt_ref, wk_ref, wvt_ref, wfc_ref, bfc_ref,
                qt_ref, k_ref, vt_ref, fcol_ref, frow_ref, ccol_ref, *, qscale, hd):
    @pl.when(pl.program_id(1) == 0)
    def _():
        ccol_ref[...] = jnp.zeros_like(ccol_ref)

    dot = functools.partial(jnp.dot, preferred_element_type=F32)
    xn = _rms(x_ref[0])
    hq = (xn * gq_ref[...] * (1.0 + scq_ref[0]) + shq_ref[0]).astype(BF16)
    qt_ref[0] = (lax.dot_general(wqt_ref[...], hq, (((1,), (1,)), ((), ())), preferred_element_type=F32)
                 * qscale).astype(BF16)
    hkv = (xn * gkv_ref[...] * (1.0 + sckv_ref[0]) + shkv_ref[0]).astype(BF16)
    k_ref[0] = dot(hkv, wk_ref[...]).astype(BF16)
    vT = lax.dot_general(wvt_ref[...], hkv, (((1,), (1,)), ((), ())), preferred_element_type=F32)
    tm = x_ref.shape[1]
    hdx = hd + PV_EXTRA_ROWS
    ones_row = (lax.broadcasted_iota(jnp.int32, (PV_EXTRA_ROWS, tm), 0) == 0).astype(BF16)
    for h in range(vT.shape[0] // hd):
        vt_ref[0, h * hdx:h * hdx + hd, :] = vT[h * hd:(h + 1) * hd].astype(BF16)
        vt_ref[0, h * hdx + hd:(h + 1) * hdx, :] = ones_row
    lower, _ = _tri(tm)
    lf_col = _log_sigmoid(dot(hkv, wfc_ref[...]) + bfc_ref[...])
    f_col = _cumsum_rows(lower.astype(BF16), lf_col) + ccol_ref[0:1, :]
    ccol_ref[...] = jnp.broadcast_to(f_col[tm - 1:tm, :], ccol_ref.shape)
    f2 = f_col * LOG2E
    fcol_ref[0] = f2
    frow_ref[0] = f2.T[0:frow_ref.shape[1]]


def _qkv_call(x, gq, shq, scq, gkv, shkv, sckv, wq_t, wk, wv_t, wf_col, bf_col, *, heads):
    B, S, D = x.shape
    tm = min(ROW_TILE, S)
    hd = D // heads
    vt_rows = heads * (hd + PV_EXTRA_ROWS)
    row = lambda b, i: (b, i, 0)
    col = lambda b, i: (b, 0, i)
    return pl.pallas_call(
        functools.partial(_qkv_kernel, qscale=hd ** -0.5 * LOG2E, hd=hd),
        grid=(B, S // tm),
        in_specs=[
            pl.BlockSpec((1, tm, D), row),
            _resident((1, D)), _chunk_spec(shq, D), _chunk_spec(scq, D),
            _resident((1, D)), _chunk_spec(shkv, D), _chunk_spec(sckv, D),
            _resident((D, D)), _resident((D, D)), _resident((D, D)),
            _resident((D, LANES)), _resident((1, LANES)),
        ],
        out_specs=[
            pl.BlockSpec((1, D, tm), col),
            pl.BlockSpec((1, tm, D), row),
            pl.BlockSpec((1, vt_rows, tm), col),
            pl.BlockSpec((1, tm, LANES), row),
            pl.BlockSpec((1, heads, tm), col),
        ],
        out_shape=[
            jax.ShapeDtypeStruct((B, D, S), BF16),
            jax.ShapeDtypeStruct((B, S, D), BF16),
            jax.ShapeDtypeStruct((B, vt_rows, S), BF16),
            jax.ShapeDtypeStruct((B, S, LANES), F32),
            jax.ShapeDtypeStruct((B, heads, S), F32),
        ],
        scratch_shapes=[pltpu.VMEM((8, LANES), F32)],
        compiler_params=pltpu.CompilerParams(
            dimension_semantics=("arbitrary", "arbitrary"), vmem_limit_bytes=VMEM_LIMIT),
        name="fox_qkv",
    )(x, gq, shq[0], scq[0], gkv, shkv[0], sckv[0], wq_t, wk, wv_t, wf_col, bf_col)


def _fox_kernel(qt_ref, k_ref, vt_ref, fcol_ref, frow_ref, o_ref, fkb_ref, acc_ref, qm_ref,
                a0_ref, a1_ref, p0_ref, p1_ref, *, hd, tq, tk):
    hp = pl.program_id(1)
    S = k_ref.shape[1]
    nq = S // tq
    hdx = hd + PV_EXTRA_ROWS
    lane = lax.broadcasted_iota(jnp.int32, (1, LANES), 1)
    sub = lax.broadcasted_iota(jnp.int32, (LANES, 1), 0)
    a_ref = (a0_ref, a1_ref)
    p_ref = (p0_ref, p1_ref)

    def spread_fk(c):
        rows = pl.ds(pl.multiple_of(c * tk, tk), tk)
        fc = fcol_ref[0, rows, :]
        for h in range(2):
            col = jnp.sum(jnp.where(lane == 2 * hp + h, fc, 0.0), axis=1, keepdims=True)
            fkb_ref[h, rows, :] = jnp.broadcast_to(col, (tk, LANES))

    def masked_q(i, h):
        qT = qt_ref[0, :, pl.ds(pl.multiple_of(i * tq, tq), tq)].astype(F32)
        return jnp.where((sub < hd) == (h == 0), qT, 0.0).astype(BF16)

    def qk(j, h, qm=None):
        start = pl.multiple_of(j * tk, tk)
        fk = fkb_ref[h, pl.ds(start, tk), :]
        a_ref[h][:, 0:tq] = (jnp.dot(k_ref[0, pl.ds(start, tk), :], qm_ref[h] if qm is None else qm,
                                     preferred_element_type=F32)
                             - jnp.concatenate([fk] * (tq // LANES), axis=1))

    def softmax(h, m_prev, fq, diagonal):
        a = a_ref[h][:, 0:tq]
        if diagonal:
            r = lax.broadcasted_iota(jnp.int32, (tk, tq), 0)
            c = lax.broadcasted_iota(jnp.int32, (tk, tq), 1)
            a = jnp.where(r <= c, a, -jnp.inf)
        m_new = jnp.maximum(m_prev, jnp.max(a, axis=0, keepdims=True) + fq)
        alpha = jnp.exp2(m_prev - m_new)
        p_ref[h][:, 0:tq] = jnp.exp2(a - (m_new - fq)).astype(BF16)
        return m_new, alpha

    def pv(j, h, alpha):
        start = pl.multiple_of(j * tk, tk)
        v2t = vt_ref[0, h * hdx:(h + 1) * hdx, pl.ds(start, tk)]
        acc_ref[h] = alpha * acc_ref[h] + jnp.dot(v2t, p_ref[h][:, 0:tq], preferred_element_type=F32)

    def body(j, carry, fq, diagonal, deferred=True):
        m0, m1, alpha1 = carry
        if deferred:
            pv(j - 1, 1, alpha1)
        m0, alpha0 = softmax(0, m0, fq[0], diagonal)
        qk(j, 1)
        pv(j, 0, alpha0)
        if not diagonal:
            qk(j + 1, 0)
        m1, alpha1 = softmax(1, m1, fq[1], diagonal)
        return m0, m1, alpha1

    def query_block(i, first):
        qstart = pl.multiple_of(i * tq, tq)
        for h in range(2):
            qm_ref[h] = masked_q(i, h)
        fq = [frow_ref[0, 0, h:h + 1, pl.ds(qstart, tq)] for h in range(2)]
        acc_ref[...] = jnp.zeros(acc_ref.shape, F32)
        neg = jnp.full((1, tq), -jnp.inf, F32)
        carry = (neg, neg, jnp.ones((1, tq), F32))
        if first:
            _, _, alpha1 = body(i, carry, fq, True, deferred=False)
        else:
            carry = body(0, carry, fq, False, deferred=False)
            carry = lax.fori_loop(1, i, lambda j, c: body(j, c, fq, False), carry)
            _, _, alpha1 = body(i, carry, fq, True)
        qk(0, 0, masked_q(jnp.minimum(i + 1, nq - 1), 0))
        spread_fk(jnp.minimum(i + 1, nq - 1))
        pv(i, 1, alpha1)
        ot = jnp.concatenate([acc_ref[h, 0:hd, :] / acc_ref[h, hd:hd + 1, :] for h in range(2)], axis=0)
        o_ref[0, :, pl.ds(qstart, tq)] = ot.astype(BF16)

    spread_fk(0)
    qk(0, 0, masked_q(0, 0))
    query_block(jnp.int32(0), True)

    def later_block(i, carry):
        query_block(i, False)
        return carry

    lax.fori_loop(1, nq, later_block, 0)


def _fox_call(qt, k, vt, fcol, frow, *, heads):
    B, S, D = k.shape
    hd = D // heads
    assert 2 * hd == LANES
    t = min(ATTN_TILE, S)
    frow = frow.reshape(B, heads // 2, 2, S)
    return pl.pallas_call(
        functools.partial(_fox_kernel, hd=hd, tq=t, tk=t),
        grid=(B, heads // 2),
        in_specs=[
            pl.BlockSpec((1, LANES, S), lambda b, p: (b, p, 0)),
            pl.BlockSpec((1, S, LANES), lambda b, p: (b, 0, p)),
            pl.BlockSpec((1, 2 * (hd + PV_EXTRA_ROWS), S), lambda b, p: (b, p, 0)),
            pl.BlockSpec((1, S, LANES), lambda b, p: (b, 0, 0)),
            pl.BlockSpec((1, 1, 2, S), lambda b, p: (b, p, 0, 0)),
        ],
        out_specs=pl.BlockSpec((1, LANES, S), lambda b, p: (b, p, 0)),
        out_shape=jax.ShapeDtypeStruct((B, D, S), BF16),
        scratch_shapes=[
            pltpu.VMEM((2, S, LANES), F32),
            pltpu.VMEM((2, hd + PV_EXTRA_ROWS, t), F32),
            pltpu.VMEM((2, LANES, t), BF16),
            pltpu.VMEM((t, t + LANES), F32), pltpu.VMEM((t, t + LANES), F32),
            pltpu.VMEM((t, t + LANES), BF16), pltpu.VMEM((t, t + LANES), BF16),
        ],
        compiler_params=pltpu.CompilerParams(
            dimension_semantics=("arbitrary", "arbitrary"), vmem_limit_bytes=VMEM_LIMIT),
        name="fox_attention",
    )(qt, k, vt, fcol, frow)


def _pad_cols(w, n):
    return jnp.pad(w, ((0, 0), (0, n - w.shape[1])))


def kernel(x, c, mod_w, mod_b, mix_pre_g, mix_post_g, ffn_pre_g, ffn_post_g, ffn_w_in, ffn_w_out,
           a_w_in, a_gate_b, a_hnorm_g, a_w_out, kv_norm_g, kv_mod_w, kv_mod_b, kv_w, kv_fgate_b,
           b_w_q, b_w_out):
    B, S, D = x.shape
    depth = mod_w.shape[0]
    n_a = a_w_in.shape[0]
    m_heads = a_gate_b.shape[1] // 2
    f_heads = kv_fgate_b.shape[0]
    dff = ffn_w_out.shape[1]
    dq = (a_w_in.shape[2] - 2 * D - 2 * m_heads) // 2

    mods = _mod_call(c, mod_w, mod_b)
    kv_mods = _mod_call(c, kv_mod_w[None], kv_mod_b[None])[0]
    vecs = lambda m, n: [(m.reshape(B, 1, n * D), k) for k in range(n)]
    row = lambda g: g.reshape(1, D)

    k_sh = v_sh = fcol = frow = None
    for l in range(depth):
        sh1, sc1, g1, sh2, sc2, g2 = vecs(mods[l], 6)
        if l < n_a:
            w = a_w_in[l]
            nmain = 2 * dq + 2 * D
            wg = w[:, nmain:]
            w_ko = jnp.concatenate([w[:, dq:2 * dq], w[:, 2 * dq + D:nmain]], axis=1)
            w_qv = jnp.concatenate([w[:, :dq], w[:, 2 * dq:2 * dq + D]], axis=1)
            qt, k, vt, o, gcol, grow = _inproj_call(
                x, sh1, sc1, row(mix_pre_g[l]),
                w_ko.astype(BF16), w_qv.T.astype(BF16), _pad_cols(wg, LANES).astype(BF16),
                _pad_cols(a_gate_b[l][None, :], LANES), dq=dq, dm=D, heads=m_heads)
            y = _mlstm_call(qt, k, vt, o, gcol, grow, row(a_hnorm_g[l]), heads=m_heads)
            w_o = a_w_out[l]
        else:
            if l == n_a:
                kv_sh, kv_sc = vecs(kv_mods, 2)
                wf = kv_w[:, 2 * D:]
                shared = dict(
                    gkv=row(kv_norm_g), shkv=kv_sh, sckv=kv_sc,
                    wk=kv_w[:, :D].astype(BF16), wv=kv_w[:, D:2 * D].T.astype(BF16),
                    wf_col=_pad_cols(wf, LANES).astype(BF16), bf_col=_pad_cols(kv_fgate_b[None, :], LANES))
            qt, k_new, v_new, fcol_new, frow_new = _qkv_call(
                x, row(mix_pre_g[l]), sh1, sc1, shared["gkv"], shared["shkv"], shared["sckv"],
                b_w_q[l - n_a].T.astype(BF16), shared["wk"], shared["wv"],
                shared["wf_col"], shared["bf_col"], heads=f_heads)
            if l == n_a:
                k_sh, v_sh, fcol, frow = k_new, v_new, fcol_new, frow_new
            y = _fox_call(qt, k_sh, v_sh, fcol, frow, heads=f_heads)
            w_o = b_w_out[l - n_a]
        w_in = ffn_w_in[l]
        x = _mix_ffn_call(
            y, x, w_o.astype(BF16), row(mix_post_g[l]), g1, row(ffn_pre_g[l]), sh2, sc2, g2,
            row(ffn_post_g[l]), w_in[:, :dff].astype(BF16), w_in[:, dff:].astype(BF16),
            ffn_w_out[l].astype(BF16), y_transposed=l >= n_a)
    return x
```

```python
import functools

import jax
import jax.numpy as jnp
from jax import lax
from jax.experimental import pallas as pl
from jax.experimental.pallas import tpu as pltpu

F32 = jnp.float32
BF16 = jnp.bfloat16

EPS = 1e-6
LOG2E = 1.4426950408889634
GATE_CAP = 15.0
LANES = 128
MLSTM_CHUNK = 256
ROW_TILE = 512
FFN_ROW_TILE = 512
ATTN_TILE = 512
FFN_COL_CHUNK = 256
PV_EXTRA_ROWS = 16
MLSTM_EXTRA_ROWS = 16
VMEM_LIMIT = 56 * 1024 * 1024


def _rms(x):
    return x * lax.rsqrt(jnp.mean(x * x, axis=-1, keepdims=True) + EPS)


def _log_sigmoid(x):
    return jnp.minimum(x, 0.0) - jnp.log(1.0 + jnp.exp(-jnp.abs(x)))


def _softcap(a):
    return GATE_CAP * jnp.tanh(a / GATE_CAP)


def _split3(x):
    hi = x.astype(BF16)
    r1 = x - hi.astype(F32)
    mid = r1.astype(BF16)
    lo = (r1 - mid.astype(F32)).astype(BF16)
    return hi, mid, lo


def _cumsum_rows(tri_lower, x):
    return sum(jnp.dot(tri_lower, p, preferred_element_type=F32) for p in _split3(x))


def _cumsum_lanes(x, tri_upper):
    return sum(jnp.dot(p, tri_upper, preferred_element_type=F32) for p in _split3(x))


def _tri(n):
    r = lax.broadcasted_iota(jnp.int32, (n, n), 0)
    c = lax.broadcasted_iota(jnp.int32, (n, n), 1)
    return r >= c, (r <= c)


def _chunk_spec(chunk, width):
    k = chunk[1]
    return pl.BlockSpec((1, 1, width), lambda b, i: (b, 0, k))


def _resident(shape):
    nd = len(shape)
    return pl.BlockSpec(shape, lambda *_: (0,) * nd, pipeline_mode=pl.Buffered(1))


def _mod_kernel(c_ref, w_ref, b_ref, o_ref):
    c = c_ref[...]
    cs = (c * jax.nn.sigmoid(c)).astype(BF16)
    o_ref[0] = jnp.dot(cs, w_ref[0].astype(BF16), preferred_element_type=F32) + b_ref[0]


def _mod_call(c, w, b):
    G, D, N = w.shape
    B = c.shape[0]
    tn = 1024
    return pl.pallas_call(
        _mod_kernel,
        grid=(G, N // tn),
        in_specs=[
            pl.BlockSpec((B, D), lambda g, j: (0, 0)),
            pl.BlockSpec((1, D, tn), lambda g, j: (g, 0, j)),
            pl.BlockSpec((1, 1, tn), lambda g, j: (g, 0, j)),
        ],
        out_specs=pl.BlockSpec((1, B, tn), lambda g, j: (g, 0, j)),
        out_shape=jax.ShapeDtypeStruct((G, B, N), F32),
        name="adaln_mod",
    )(c, w, b.reshape(G, 1, N))


def _inproj_kernel(x_ref, sh_ref, sc_ref, g_ref, wko_ref, wqvt_ref, wgc_ref, bc_ref,
                   qt_ref, k_ref, vt_ref, o_ref, gcol_ref, grow_ref, *, dq, dm, heads, qscale):
    x = x_ref[0]
    tm = x.shape[0]
    h = (_rms(x) * g_ref[...] * (1.0 + sc_ref[0]) + sh_ref[0]).astype(BF16)
    nt = (((1,), (1,)), ((), ()))
    ko = jnp.dot(h, wko_ref[...], preferred_element_type=F32)
    k_ref[0] = ko[:, 0:dq].astype(BF16)
    o_ref[0] = ko[:, dq:dq + dm].astype(BF16)
    qvt = lax.dot_general(wqvt_ref[...], h, nt, preferred_element_type=F32)
    qt_ref[0] = (qvt[0:dq] * qscale).astype(BF16)
    dv = dm // heads
    dvx = dv + MLSTM_EXTRA_ROWS
    ones_row = (lax.broadcasted_iota(jnp.int32, (MLSTM_EXTRA_ROWS, tm), 0) == 0).astype(BF16)
    for hh in range(heads):
        vt_ref[0, hh * dvx:hh * dvx + dv, :] = qvt[dq + hh * dv:dq + (hh + 1) * dv].astype(BF16)
        vt_ref[0, hh * dvx + dv:(hh + 1) * dvx, :] = ones_row
    gc = _softcap(jnp.dot(h, wgc_ref[...], preferred_element_type=F32) + bc_ref[...])
    lane = lax.broadcasted_iota(jnp.int32, gc.shape, 1)
    g2 = jnp.where(lane < heads, gc, _log_sigmoid(gc)) * LOG2E
    gcol_ref[0] = g2
    grow_ref[0] = g2.T[0:grow_ref.shape[1]]


def _inproj_call(x, sh, sc, g, w_ko, w_qv_t, wg_col, b_col, *, dq, dm, heads):
    B, S, D = x.shape
    tm = min(ROW_TILE, S)
    dvx_all = dm + heads * MLSTM_EXTRA_ROWS
    row = lambda b, i: (b, i, 0)
    col = lambda b, i: (b, 0, i)
    kern = functools.partial(_inproj_kernel, dq=dq, dm=dm, heads=heads, qscale=(dq // heads) ** -0.5)
    return pl.pallas_call(
        kern,
        grid=(B, S // tm),
        in_specs=[
            pl.BlockSpec((1, tm, D), row),
            _chunk_spec(sh, D),
            _chunk_spec(sc, D),
            _resident((1, D)),
            _resident((D, dq + dm)),
            _resident((dq + dm, D)),
            _resident((D, LANES)),
            _resident((1, LANES)),
        ],
        out_specs=[
            pl.BlockSpec((1, dq, tm), col),
            pl.BlockSpec((1, tm, dq), row),
            pl.BlockSpec((1, dvx_all, tm), col),
            pl.BlockSpec((1, tm, dm), row),
            pl.BlockSpec((1, tm, LANES), row),
            pl.BlockSpec((1, 16, tm), col),
        ],
        out_shape=[
            jax.ShapeDtypeStruct((B, dq, S), BF16),
            jax.ShapeDtypeStruct((B, S, dq), BF16),
            jax.ShapeDtypeStruct((B, dvx_all, S), BF16),
            jax.ShapeDtypeStruct((B, S, dm), BF16),
            jax.ShapeDtypeStruct((B, S, LANES), F32),
            jax.ShapeDtypeStruct((B, 16, S), F32),
        ],
        compiler_params=pltpu.CompilerParams(
            dimension_semantics=("arbitrary", "arbitrary"), vmem_limit_bytes=VMEM_LIMIT),
        name="mlstm_inproj",
    )(x, sh[0], sc[0], g, w_ko, w_qv_t, wg_col, b_col)


def _mlstm_kernel(qt_ref, k_ref, vt_ref, o_ref, gcol_ref, grow_ref, hn_ref, y_ref,
                  st_ref, m_ref, *, heads, dk, dv, L):
    @pl.when(pl.program_id(1) == 0)
    def _():
        st_ref[...] = jnp.zeros_like(st_ref)
        m_ref[...] = jnp.zeros_like(m_ref)

    gcol = gcol_ref[0]
    grow = grow_ref[0]
    lower, upper = _tri(L)
    bcol = _cumsum_rows(lower.astype(BF16), gcol)
    brow = _cumsum_lanes(grow, upper.astype(BF16))
    dvx = dv + MLSTM_EXTRA_ROWS
    dot = functools.partial(jnp.dot, preferred_element_type=F32)

    hs = range(heads)
    i_row = [grow[h:h + 1, :] for h in hs]
    b_row = [brow[heads + h:heads + h + 1, :] for h in hs]
    i_col = [gcol[:, h:h + 1] for h in hs]
    b_col = [bcol[:, heads + h:heads + h + 1] for h in hs]
    bl = [b[:, L - 1:L] for b in b_row]
    m = [m_ref[h][0:1, 0:1] for h in hs]
    qt = [qt_ref[0, h * dk:(h + 1) * dk, :] for h in hs]
    kh = [k_ref[0, :, h * dk:(h + 1) * dk] for h in hs]
    vt = [vt_ref[0, h * dvx:(h + 1) * dvx, :] for h in hs]
    st = [st_ref[h] for h in hs]

    qk = [dot(kh[h], qt[h]) for h in hs]
    sq = [dot(st[h].astype(BF16), qt[h]) for h in hs]
    m_t, a, w = [], [], []
    for h in hs:
        dmat = jnp.where(upper, b_row[h] + (i_col[h] - b_col[h]), -jnp.inf)
        inter = b_row[h] + m[h]
        m_t.append(jnp.maximum(inter, jnp.max(dmat, axis=0, keepdims=True)))
        a.append(jnp.exp2(inter - m_t[h]))
        w.append(jnp.exp2(dmat - m_t[h]))
    na = [a[h] * sq[h] + dot(vt[h], (qk[h] * w[h]).astype(BF16)) for h in hs]
    for h in hs:
        ws = bl[h] - b_col[h] + i_col[h]
        m_new = jnp.maximum(bl[h] + m[h], jnp.max(ws, axis=0, keepdims=True))
        kw = (kh[h].astype(F32) * jnp.exp2(ws - m_new)).astype(BF16)
        st_ref[h] = jnp.exp2(bl[h] + m[h] - m_new) * st[h] + dot(vt[h], kw)
        m_ref[h] = jnp.broadcast_to(m_new, m_ref.shape[1:])

        hT = na[h][0:dv] * (1.0 / jnp.maximum(jnp.abs(na[h][dv:dv + 1]), jnp.exp2(-m_t[h])))
        hnT = hT * lax.rsqrt(jnp.mean(hT * hT, axis=0, keepdims=True) + EPS)
        og = o_ref[0, :, h * dv:(h + 1) * dv].astype(F32)
        y_ref[0, :, h * dv:(h + 1) * dv] = (jax.nn.sigmoid(og) * (hnT.T * hn_ref[:, h * dv:(h + 1) * dv])).astype(BF16)


def _mlstm_call(qt, k, vt, o, gcol, grow, hn_g, *, heads):
    B, S, dq = k.shape
    dm = o.shape[-1]
    dk, dv = dq // heads, dm // heads
    dvx_all = vt.shape[1]
    L = min(MLSTM_CHUNK, S)
    row = lambda b, c: (b, c, 0)
    col = lambda b, c: (b, 0, c)
    kern = functools.partial(_mlstm_kernel, heads=heads, dk=dk, dv=dv, L=L)
    return pl.pallas_call(
        kern,
        grid=(B, S // L),
        in_specs=[
            pl.BlockSpec((1, dq, L), col),
            pl.BlockSpec((1, L, dq), row),
            pl.BlockSpec((1, dvx_all, L), col),
            pl.BlockSpec((1, L, dm), row),
            pl.BlockSpec((1, L, LANES), row),
            pl.BlockSpec((1, 16, L), col),
            _resident((1, dm)),
        ],
        out_specs=pl.BlockSpec((1, L, dm), row),
        out_shape=jax.ShapeDtypeStruct((B, S, dm), BF16),
        scratch_shapes=[
            pltpu.VMEM((heads, dv + MLSTM_EXTRA_ROWS, dk), F32),
            pltpu.VMEM((heads, 8, LANES), F32),
        ],
        compiler_params=pltpu.CompilerParams(
            dimension_semantics=("arbitrary", "arbitrary"), vmem_limit_bytes=VMEM_LIMIT),
        name="mlstm_recurrence",
    )(qt, k, vt, o, gcol, grow, hn_g)


def _mix_ffn_kernel(y_ref, x_ref, wo_ref, pg_ref, g1_ref, fg_ref, sh2_ref, sc2_ref, g2_ref, fpg_ref,
                    wg_ref, wu_ref, wd_ref, out_ref, act_ref, *, dff, y_transposed):
    dot = functools.partial(jnp.dot, preferred_element_type=F32)
    if y_transposed:
        t = lax.dot_general(y_ref[0], wo_ref[...], (((0,), (0,)), ((), ())), preferred_element_type=F32)
    else:
        t = dot(y_ref[0], wo_ref[...])
    x1 = x_ref[0] + g1_ref[0] * (_rms(t) * pg_ref[...])
    h2 = (_rms(x1) * fg_ref[...] * (1.0 + sc2_ref[0]) + sh2_ref[0]).astype(BF16)
    cw = FFN_COL_CHUNK
    for j in range(dff // cw):
        g = dot(h2, wg_ref[:, j * cw:(j + 1) * cw])
        u = dot(h2, wu_ref[:, j * cw:(j + 1) * cw])
        act_ref[:, j * cw:(j + 1) * cw] = (g * jax.nn.sigmoid(g) * u).astype(BF16)
    f = dot(act_ref[...], wd_ref[...])
    out_ref[0] = x1 + g2_ref[0] * (_rms(f) * fpg_ref[...])


def _mix_ffn_call(y, x, w_o, post_g, g1, ffn_pre_g, sh2, sc2, g2, ffn_post_g, w_in_all, w_down_all, layer,
                  *, y_transposed=False):
    B, S, D = x.shape
    dff = w_down_all.shape[1]
    tm = min(FFN_ROW_TILE, S)
    row = lambda b, i: (b, i, 0)
    layer_block = lambda shape, k: pl.BlockSpec((None,) + shape, lambda b, i: (layer, 0, k),
                                                pipeline_mode=pl.Buffered(1))
    return pl.pallas_call(
        functools.partial(_mix_ffn_kernel, dff=dff, y_transposed=y_transposed),
        grid=(B, S // tm),
        in_specs=[
            pl.BlockSpec((1, D, tm), lambda b, i: (b, 0, i)) if y_transposed else pl.BlockSpec((1, tm, D), row),
            pl.BlockSpec((1, tm, D), row),
            _resident((D, D)),
            _resident((1, D)),
            _chunk_spec(g1, D),
            _resident((1, D)),
            _chunk_spec(sh2, D), _chunk_spec(sc2, D), _chunk_spec(g2, D),
            _resident((1, D)),
            layer_block((D, dff), 0),
            layer_block((D, dff), 1),
            layer_block((dff, D), 0),
        ],
        out_specs=pl.BlockSpec((1, tm, D), row),
        out_shape=jax.ShapeDtypeStruct((B, S, D), F32),
        scratch_shapes=[pltpu.VMEM((tm, dff), BF16)],
        compiler_params=pltpu.CompilerParams(
            dimension_semantics=("arbitrary", "arbitrary"), vmem_limit_bytes=VMEM_LIMIT),
        name="mix_ffn",
    )(y, x, w_o, post_g, g1[0], ffn_pre_g, sh2[0], sc2[0], g2[0], ffn_post_g, w_in_all, w_in_all, w_down_all)


def _qkv_kernel(x_ref, gq_ref, shq_ref, scq_ref, gkv_ref, shkv_ref, sckv_ref,
                wqt_ref, wk_ref, wvt_ref, wfc_ref, bfc_ref,
                qt_ref, k_ref, vt_ref, fcol_ref, frow_ref, ccol_ref, *, qscale, hd):
    @pl.when(pl.program_id(1) == 0)
    def _():
        ccol_ref[...] = jnp.zeros_like(ccol_ref)

    dot = functools.partial(jnp.dot, preferred_element_type=F32)
    xn = _rms(x_ref[0])
    hq = (xn * gq_ref[...] * (1.0 + scq_ref[0]) + shq_ref[0]).astype(BF16)
    qt_ref[0] = (lax.dot_general(wqt_ref[...], hq, (((1,), (1,)), ((), ())), preferred_element_type=F32)
                 * qscale).astype(BF16)
    hkv = (xn * gkv_ref[...] * (1.0 + sckv_ref[0]) + shkv_ref[0]).astype(BF16)
    k_ref[0] = dot(hkv, wk_ref[...]).astype(BF16)
    vT = lax.dot_general(wvt_ref[...], hkv, (((1,), (1,)), ((), ())), preferred_element_type=F32)
    tm = x_ref.shape[1]
    hdx = hd + PV_EXTRA_ROWS
    ones_row = (lax.broadcasted_iota(jnp.int32, (PV_EXTRA_ROWS, tm), 0) == 0).astype(BF16)
    for h in range(vT.shape[0] // hd):
        vt_ref[0, h * hdx:h * hdx + hd, :] = vT[h * hd:(h + 1) * hd].astype(BF16)
        vt_ref[0, h * hdx + hd:(h + 1) * hdx, :] = ones_row
    lower, _ = _tri(tm)
    lf_col = _log_sigmoid(dot(hkv, wfc_ref[...]) + bfc_ref[...])
    f_col = _cumsum_rows(lower.astype(BF16), lf_col) + ccol_ref[0:1, :]
    ccol_ref[...] = jnp.broadcast_to(f_col[tm - 1:tm, :], ccol_ref.shape)
    f2 = f_col * LOG2E
    fcol_ref[0] = f2
    frow_ref[0] = f2.T[0:frow_ref.shape[1]]


def _qkv_call(x, gq, shq, scq, gkv, shkv, sckv, wq_t, wk, wv_t, wf_col, bf_col, *, heads):
    B, S, D = x.shape
    tm = min(ROW_TILE, S)
    hd = D // heads
    vt_rows = heads * (hd + PV_EXTRA_ROWS)
    row = lambda b, i: (b, i, 0)
    col = lambda b, i: (b, 0, i)
    return pl.pallas_call(
        functools.partial(_qkv_kernel, qscale=hd ** -0.5 * LOG2E, hd=hd),
        grid=(B, S // tm),
        in_specs=[
            pl.BlockSpec((1, tm, D), row),
            _resident((1, D)), _chunk_spec(shq, D), _chunk_spec(scq, D),
            _resident((1, D)), _chunk_spec(shkv, D), _chunk_spec(sckv, D),
            _resident((D, D)), _resident((D, D)), _resident((D, D)),
            _resident((D, LANES)), _resident((1, LANES)),
        ],
        out_specs=[
            pl.BlockSpec((1, D, tm), col),
            pl.BlockSpec((1, tm, D), row),
            pl.BlockSpec((1, vt_rows, tm), col),
            pl.BlockSpec((1, tm, LANES), row),
            pl.BlockSpec((1, heads, tm), col),
        ],
        out_shape=[
            jax.ShapeDtypeStruct((B, D, S), BF16),
            jax.ShapeDtypeStruct((B, S, D), BF16),
            jax.ShapeDtypeStruct((B, vt_rows, S), BF16),
            jax.ShapeDtypeStruct((B, S, LANES), F32),
            jax.ShapeDtypeStruct((B, heads, S), F32),
        ],
        scratch_shapes=[pltpu.VMEM((8, LANES), F32)],
        compiler_params=pltpu.CompilerParams(
            dimension_semantics=("arbitrary", "arbitrary"), vmem_limit_bytes=VMEM_LIMIT),
        name="fox_qkv",
    )(x, gq, shq[0], scq[0], gkv, shkv[0], sckv[0], wq_t, wk, wv_t, wf_col, bf_col)


def _fox_kernel(qt_ref, k_ref, vt_ref, fcol_ref, frow_ref, o_ref, fkb_ref, acc_ref, qm_ref,
                a0_ref, a1_ref, p0_ref, p1_ref, *, hd, tq, tk):
    hp = pl.program_id(1)
    S = k_ref.shape[1]
    nq = S // tq
    hdx = hd + PV_EXTRA_ROWS
    lane = lax.broadcasted_iota(jnp.int32, (1, LANES), 1)
    sub = lax.broadcasted_iota(jnp.int32, (LANES, 1), 0)
    a_ref = (a0_ref, a1_ref)
    p_ref = (p0_ref, p1_ref)

    def spread_fk(c):
        rows = pl.ds(pl.multiple_of(c * tk, tk), tk)
        fc = fcol_ref[0, rows, :]
        for h in range(2):
            col = jnp.sum(jnp.where(lane == 2 * hp + h, fc, 0.0), axis=1, keepdims=True)
            fkb_ref[h, rows, :] = jnp.broadcast_to(col, (tk, LANES))

    def masked_q(i, h):
        qT = qt_ref[0, :, pl.ds(pl.multiple_of(i * tq, tq), tq)].astype(F32)
        return jnp.where((sub < hd) == (h == 0), qT, 0.0).astype(BF16)

    def qk(j, h, qm=None):
        start = pl.multiple_of(j * tk, tk)
        fk = fkb_ref[h, pl.ds(start, tk), :]
        a_ref[h][:, 0:tq] = (jnp.dot(k_ref[0, pl.ds(start, tk), :], qm_ref[h] if qm is None else qm,
                                     preferred_element_type=F32)
                             - jnp.concatenate([fk] * (tq // LANES), axis=1))

    def softmax(h, m_prev, fq, diagonal):
        a = a_ref[h][:, 0:tq]
        if diagonal:
            r = lax.broadcasted_iota(jnp.int32, (tk, tq), 0)
            c = lax.broadcasted_iota(jnp.int32, (tk, tq), 1)
            a = jnp.where(r <= c, a, -jnp.inf)
        m_new = jnp.maximum(m_prev, jnp.max(a, axis=0, keepdims=True) + fq)
        alpha = jnp.exp2(m_prev - m_new)
        p_ref[h][:, 0:tq] = jnp.exp2(a - (m_new - fq)).astype(BF16)
        return m_new, alpha

    def pv(j, h, alpha):
        start = pl.multiple_of(j * tk, tk)
        v2t = vt_ref[0, h * hdx:(h + 1) * hdx, pl.ds(start, tk)]
        acc_ref[h] = alpha * acc_ref[h] + jnp.dot(v2t, p_ref[h][:, 0:tq], preferred_element_type=F32)

    def body(j, carry, fq, diagonal, deferred=True):
        m0, m1, alpha1 = carry
        if deferred:
            pv(j - 1, 1, alpha1)
        m0, alpha0 = softmax(0, m0, fq[0], diagonal)
        qk(j, 1)
        pv(j, 0, alpha0)
        if not diagonal:
            qk(j + 1, 0)
        m1, alpha1 = softmax(1, m1, fq[1], diagonal)
        return m0, m1, alpha1

    def query_block(i, first):
        qstart = pl.multiple_of(i * tq, tq)
        for h in range(2):
            qm_ref[h] = masked_q(i, h)
        fq = [frow_ref[0, 0, h:h + 1, pl.ds(qstart, tq)] for h in range(2)]
        acc_ref[...] = jnp.zeros(acc_ref.shape, F32)
        neg = jnp.full((1, tq), -jnp.inf, F32)
        carry = (neg, neg, jnp.ones((1, tq), F32))
        if first:
            _, _, alpha1 = body(i, carry, fq, True, deferred=False)
        else:
            carry = body(0, carry, fq, False, deferred=False)
            carry = lax.fori_loop(1, i, lambda j, c: body(j, c, fq, False), carry)
            _, _, alpha1 = body(i, carry, fq, True)
        qk(0, 0, masked_q(jnp.minimum(i + 1, nq - 1), 0))
        spread_fk(jnp.minimum(i + 1, nq - 1))
        pv(i, 1, alpha1)
        ot = jnp.concatenate([acc_ref[h, 0:hd, :] / acc_ref[h, hd:hd + 1, :] for h in range(2)], axis=0)
        o_ref[0, :, pl.ds(qstart, tq)] = ot.astype(BF16)

    spread_fk(0)
    qk(0, 0, masked_q(0, 0))
    query_block(jnp.int32(0), True)

    def later_block(i, carry):
        query_block(i, False)
        return carry

    lax.fori_loop(1, nq, later_block, 0)


def _fox_call(qt, k, vt, fcol, frow, *, heads):
    B, S, D = k.shape
    hd = D // heads
    assert 2 * hd == LANES
    t = min(ATTN_TILE, S)
    frow = frow.reshape(B, heads // 2, 2, S)
    return pl.pallas_call(
        functools.partial(_fox_kernel, hd=hd, tq=t, tk=t),
        grid=(B, heads // 2),
        in_specs=[
            pl.BlockSpec((1, LANES, S), lambda b, p: (b, p, 0)),
            pl.BlockSpec((1, S, LANES), lambda b, p: (b, 0, p)),
            pl.BlockSpec((1, 2 * (hd + PV_EXTRA_ROWS), S), lambda b, p: (b, p, 0)),
            pl.BlockSpec((1, S, LANES), lambda b, p: (b, 0, 0)),
            pl.BlockSpec((1, 1, 2, S), lambda b, p: (b, p, 0, 0)),
        ],
        out_specs=pl.BlockSpec((1, LANES, S), lambda b, p: (b, p, 0)),
        out_shape=jax.ShapeDtypeStruct((B, D, S), BF16),
        scratch_shapes=[
            pltpu.VMEM((2, S, LANES), F32),
            pltpu.VMEM((2, hd + PV_EXTRA_ROWS, t), F32),
            pltpu.VMEM((2, LANES, t), BF16),
            pltpu.VMEM((t, t + LANES), F32), pltpu.VMEM((t, t + LANES), F32),
            pltpu.VMEM((t, t + LANES), BF16), pltpu.VMEM((t, t + LANES), BF16),
        ],
        compiler_params=pltpu.CompilerParams(
            dimension_semantics=("arbitrary", "arbitrary"), vmem_limit_bytes=VMEM_LIMIT),
        name="fox_attention",
    )(qt, k, vt, fcol, frow)


def _pad_cols(w, n):
    return jnp.pad(w, ((0, 0), (0, n - w.shape[1])))


def kernel(x, c, mod_w, mod_b, mix_pre_g, mix_post_g, ffn_pre_g, ffn_post_g, ffn_w_in, ffn_w_out,
           a_w_in, a_gate_b, a_hnorm_g, a_w_out, kv_norm_g, kv_mod_w, kv_mod_b, kv_w, kv_fgate_b,
           b_w_q, b_w_out):
    B, S, D = x.shape
    depth = mod_w.shape[0]
    n_a = a_w_in.shape[0]
    m_heads = a_gate_b.shape[1] // 2
    f_heads = kv_fgate_b.shape[0]
    dq = (a_w_in.shape[2] - 2 * D - 2 * m_heads) // 2

    mods = _mod_call(c, mod_w, mod_b)
    kv_mods = _mod_call(c, kv_mod_w[None], kv_mod_b[None])[0]
    vecs = lambda m, n: [(m.reshape(B, 1, n * D), k) for k in range(n)]
    row = lambda g: g.reshape(1, D)
    ffn_in_bf16 = ffn_w_in.astype(BF16)
    ffn_out_bf16 = ffn_w_out.astype(BF16)

    k_sh = v_sh = fcol = frow = None
    for l in range(depth):
        sh1, sc1, g1, sh2, sc2, g2 = vecs(mods[l], 6)
        if l < n_a:
            w = a_w_in[l]
            nmain = 2 * dq + 2 * D
            wg = w[:, nmain:]
            w_ko = jnp.concatenate([w[:, dq:2 * dq], w[:, 2 * dq + D:nmain]], axis=1)
            w_qv = jnp.concatenate([w[:, :dq], w[:, 2 * dq:2 * dq + D]], axis=1)
            qt, k, vt, o, gcol, grow = _inproj_call(
                x, sh1, sc1, row(mix_pre_g[l]),
                w_ko.astype(BF16), w_qv.T.astype(BF16), _pad_cols(wg, LANES).astype(BF16),
                _pad_cols(a_gate_b[l][None, :], LANES), dq=dq, dm=D, heads=m_heads)
            y = _mlstm_call(qt, k, vt, o, gcol, grow, row(a_hnorm_g[l]), heads=m_heads)
            w_o = a_w_out[l]
        else:
            if l == n_a:
                kv_sh, kv_sc = vecs(kv_mods, 2)
                wf = kv_w[:, 2 * D:]
                shared = dict(
                    gkv=row(kv_norm_g), shkv=kv_sh, sckv=kv_sc,
                    wk=kv_w[:, :D].astype(BF16), wv=kv_w[:, D:2 * D].T.astype(BF16),
                    wf_col=_pad_cols(wf, LANES).astype(BF16), bf_col=_pad_cols(kv_fgate_b[None, :], LANES))
            qt, k_new, v_new, fcol_new, frow_new = _qkv_call(
                x, row(mix_pre_g[l]), sh1, sc1, shared["gkv"], shared["shkv"], shared["sckv"],
                b_w_q[l - n_a].T.astype(BF16), shared["wk"], shared["wv"],
                shared["wf_col"], shared["bf_col"], heads=f_heads)
            if l == n_a:
                k_sh, v_sh, fcol, frow = k_new, v_new, fcol_new, frow_new
            y = _fox_call(qt, k_sh, v_sh, fcol, frow, heads=f_heads)
            w_o = b_w_out[l - n_a]
        x = _mix_ffn_call(
            y, x, w_o.astype(BF16), row(mix_post_g[l]), g1, row(ffn_pre_g[l]), sh2, sc2, g2,
            row(ffn_post_g[l]), ffn_in_bf16, ffn_out_bf16, l, y_transposed=l >= n_a)
    return x
```

```python
import functools

import jax
import jax.numpy as jnp
from jax import lax
from jax.experimental import pallas as pl
from jax.experimental.pallas import tpu as pltpu

F32 = jnp.float32
BF16 = jnp.bfloat16

EPS = 1e-6
LOG2E = 1.4426950408889634
GATE_CAP = 15.0
LANES = 128
MLSTM_CHUNK = 256
ROW_TILE = 512
FFN_ROW_TILE = 512
ATTN_TILE = 512
FFN_COL_CHUNK = 256
PV_EXTRA_ROWS = 16
MLSTM_EXTRA_ROWS = 16
VMEM_LIMIT = 56 * 1024 * 1024


def _rms(x):
    return x * lax.rsqrt(jnp.mean(x * x, axis=-1, keepdims=True) + EPS)


def _log_sigmoid(x):
    return jnp.minimum(x, 0.0) - jnp.log(1.0 + jnp.exp(-jnp.abs(x)))


def _softcap(a):
    return GATE_CAP * jnp.tanh(a / GATE_CAP)


def _split3(x):
    hi = x.astype(BF16)
    r1 = x - hi.astype(F32)
    mid = r1.astype(BF16)
    lo = (r1 - mid.astype(F32)).astype(BF16)
    return hi, mid, lo


def _cumsum_rows(tri_lower, x):
    return sum(jnp.dot(tri_lower, p, preferred_element_type=F32) for p in _split3(x))


def _cumsum_lanes(x, tri_upper):
    return sum(jnp.dot(p, tri_upper, preferred_element_type=F32) for p in _split3(x))


def _tri(n):
    r = lax.broadcasted_iota(jnp.int32, (n, n), 0)
    c = lax.broadcasted_iota(jnp.int32, (n, n), 1)
    return r >= c, (r <= c)


def _chunk_spec(chunk, width):
    k = chunk[1]
    return pl.BlockSpec((1, 1, width), lambda b, i: (b, 0, k))


def _resident(shape):
    nd = len(shape)
    return pl.BlockSpec(shape, lambda *_: (0,) * nd, pipeline_mode=pl.Buffered(1))


def _mod_kernel(c_ref, w_ref, b_ref, o_ref):
    c = c_ref[...]
    cs = (c * jax.nn.sigmoid(c)).astype(BF16)
    o_ref[0] = jnp.dot(cs, w_ref[0].astype(BF16), preferred_element_type=F32) + b_ref[0]


def _mod_call(c, w, b):
    G, D, N = w.shape
    B = c.shape[0]
    tn = 1024
    return pl.pallas_call(
        _mod_kernel,
        grid=(G, N // tn),
        in_specs=[
            pl.BlockSpec((B, D), lambda g, j: (0, 0)),
            pl.BlockSpec((1, D, tn), lambda g, j: (g, 0, j)),
            pl.BlockSpec((1, 1, tn), lambda g, j: (g, 0, j)),
        ],
        out_specs=pl.BlockSpec((1, B, tn), lambda g, j: (g, 0, j)),
        out_shape=jax.ShapeDtypeStruct((G, B, N), F32),
        name="adaln_mod",
    )(c, w, b.reshape(G, 1, N))


def _inproj_kernel(x_ref, sh_ref, sc_ref, g_ref, wko_ref, wqv_ref, wgc_ref, bc_ref,
                   qt_ref, k_ref, vt_ref, o_ref, gcol_ref, grow_ref, *, dq, dm, heads, qscale):
    x = x_ref[0]
    tm = x.shape[0]
    h = (_rms(x) * g_ref[...] * (1.0 + sc_ref[0]) + sh_ref[0]).astype(BF16)
    nt = (((1,), (1,)), ((), ()))
    ko = jnp.dot(h, wko_ref[...], preferred_element_type=F32)
    k_ref[0] = ko[:, 0:dq].astype(BF16)
    o_ref[0] = ko[:, dq:dq + dm].astype(BF16)
    qvt = lax.dot_general(wqv_ref[...], h, (((0,), (1,)), ((), ())), preferred_element_type=F32)
    qt_ref[0] = (qvt[0:dq] * qscale).astype(BF16)
    dv = dm // heads
    dvx = dv + MLSTM_EXTRA_ROWS
    ones_row = (lax.broadcasted_iota(jnp.int32, (MLSTM_EXTRA_ROWS, tm), 0) == 0).astype(BF16)
    for hh in range(heads):
        vt_ref[0, hh * dvx:hh * dvx + dv, :] = qvt[dq + hh * dv:dq + (hh + 1) * dv].astype(BF16)
        vt_ref[0, hh * dvx + dv:(hh + 1) * dvx, :] = ones_row
    gc = _softcap(jnp.dot(h, wgc_ref[...], preferred_element_type=F32) + bc_ref[...])
    lane = lax.broadcasted_iota(jnp.int32, gc.shape, 1)
    g2 = jnp.where(lane < heads, gc, _log_sigmoid(gc)) * LOG2E
    gcol_ref[0] = g2
    grow_ref[0] = g2.T[0:grow_ref.shape[1]]


def _inproj_call(x, sh, sc, g, w_ko, w_qv, wg_col, b_col, *, dq, dm, heads):
    B, S, D = x.shape
    tm = min(ROW_TILE, S)
    dvx_all = dm + heads * MLSTM_EXTRA_ROWS
    row = lambda b, i: (b, i, 0)
    col = lambda b, i: (b, 0, i)
    kern = functools.partial(_inproj_kernel, dq=dq, dm=dm, heads=heads, qscale=(dq // heads) ** -0.5)
    return pl.pallas_call(
        kern,
        grid=(B, S // tm),
        in_specs=[
            pl.BlockSpec((1, tm, D), row),
            _chunk_spec(sh, D),
            _chunk_spec(sc, D),
            _resident((1, D)),
            _resident((D, dq + dm)),
            _resident((D, dq + dm)),
            _resident((D, LANES)),
            _resident((1, LANES)),
        ],
        out_specs=[
            pl.BlockSpec((1, dq, tm), col),
            pl.BlockSpec((1, tm, dq), row),
            pl.BlockSpec((1, dvx_all, tm), col),
            pl.BlockSpec((1, tm, dm), row),
            pl.BlockSpec((1, tm, LANES), row),
            pl.BlockSpec((1, 16, tm), col),
        ],
        out_shape=[
            jax.ShapeDtypeStruct((B, dq, S), BF16),
            jax.ShapeDtypeStruct((B, S, dq), BF16),
            jax.ShapeDtypeStruct((B, dvx_all, S), BF16),
            jax.ShapeDtypeStruct((B, S, dm), BF16),
            jax.ShapeDtypeStruct((B, S, LANES), F32),
            jax.ShapeDtypeStruct((B, 16, S), F32),
        ],
        compiler_params=pltpu.CompilerParams(
            dimension_semantics=("arbitrary", "arbitrary"), vmem_limit_bytes=VMEM_LIMIT),
        name="mlstm_inproj",
    )(x, sh[0], sc[0], g, w_ko, w_qv, wg_col, b_col)


def _mlstm_kernel(qt_ref, k_ref, vt_ref, o_ref, gcol_ref, grow_ref, hn_ref, y_ref,
                  st_ref, m_ref, *, heads, dk, dv, L):
    @pl.when(pl.program_id(1) == 0)
    def _():
        st_ref[...] = jnp.zeros_like(st_ref)
        m_ref[...] = jnp.zeros_like(m_ref)

    gcol = gcol_ref[0]
    grow = grow_ref[0]
    lower, upper = _tri(L)
    bcol = _cumsum_rows(lower.astype(BF16), gcol)
    brow = _cumsum_lanes(grow, upper.astype(BF16))
    dvx = dv + MLSTM_EXTRA_ROWS
    dot = functools.partial(jnp.dot, preferred_element_type=F32)

    hs = range(heads)
    i_row = [grow[h:h + 1, :] for h in hs]
    b_row = [brow[heads + h:heads + h + 1, :] for h in hs]
    i_col = [gcol[:, h:h + 1] for h in hs]
    b_col = [bcol[:, heads + h:heads + h + 1] for h in hs]
    bl = [b[:, L - 1:L] for b in b_row]
    m = [m_ref[h][0:1, 0:1] for h in hs]
    qt = [qt_ref[0, h * dk:(h + 1) * dk, :] for h in hs]
    kh = [k_ref[0, :, h * dk:(h + 1) * dk] for h in hs]
    vt = [vt_ref[0, h * dvx:(h + 1) * dvx, :] for h in hs]
    st = [st_ref[h] for h in hs]

    qk = [dot(kh[h], qt[h]) for h in hs]
    sq = [dot(st[h].astype(BF16), qt[h]) for h in hs]
    m_t, a, w = [], [], []
    for h in hs:
        dmat = jnp.where(upper, b_row[h] + (i_col[h] - b_col[h]), -jnp.inf)
        inter = b_row[h] + m[h]
        m_t.append(jnp.maximum(inter, jnp.max(dmat, axis=0, keepdims=True)))
        a.append(jnp.exp2(inter - m_t[h]))
        w.append(jnp.exp2(dmat - m_t[h]))
    na = [a[h] * sq[h] + dot(vt[h], (qk[h] * w[h]).astype(BF16)) for h in hs]
    for h in hs:
        ws = bl[h] - b_col[h] + i_col[h]
        m_new = jnp.maximum(bl[h] + m[h], jnp.max(ws, axis=0, keepdims=True))
        kw = (kh[h].astype(F32) * jnp.exp2(ws - m_new)).astype(BF16)
        st_ref[h] = jnp.exp2(bl[h] + m[h] - m_new) * st[h] + dot(vt[h], kw)
        m_ref[h] = jnp.broadcast_to(m_new, m_ref.shape[1:])

        hT = na[h][0:dv] * (1.0 / jnp.maximum(jnp.abs(na[h][dv:dv + 1]), jnp.exp2(-m_t[h])))
        hnT = hT * lax.rsqrt(jnp.mean(hT * hT, axis=0, keepdims=True) + EPS)
        og = o_ref[0, :, h * dv:(h + 1) * dv].astype(F32)
        y_ref[0, :, h * dv:(h + 1) * dv] = (jax.nn.sigmoid(og) * (hnT.T * hn_ref[:, h * dv:(h + 1) * dv])).astype(BF16)


def _mlstm_call(qt, k, vt, o, gcol, grow, hn_g, *, heads):
    B, S, dq = k.shape
    dm = o.shape[-1]
    dk, dv = dq // heads, dm // heads
    dvx_all = vt.shape[1]
    L = min(MLSTM_CHUNK, S)
    row = lambda b, c: (b, c, 0)
    col = lambda b, c: (b, 0, c)
    kern = functools.partial(_mlstm_kernel, heads=heads, dk=dk, dv=dv, L=L)
    return pl.pallas_call(
        kern,
        grid=(B, S // L),
        in_specs=[
            pl.BlockSpec((1, dq, L), col),
            pl.BlockSpec((1, L, dq), row),
            pl.BlockSpec((1, dvx_all, L), col),
            pl.BlockSpec((1, L, dm), row),
            pl.BlockSpec((1, L, LANES), row),
            pl.BlockSpec((1, 16, L), col),
            _resident((1, dm)),
        ],
        out_specs=pl.BlockSpec((1, L, dm), row),
        out_shape=jax.ShapeDtypeStruct((B, S, dm), BF16),
        scratch_shapes=[
            pltpu.VMEM((heads, dv + MLSTM_EXTRA_ROWS, dk), F32),
            pltpu.VMEM((heads, 8, LANES), F32),
        ],
        compiler_params=pltpu.CompilerParams(
            dimension_semantics=("arbitrary", "arbitrary"), vmem_limit_bytes=VMEM_LIMIT),
        name="mlstm_recurrence",
    )(qt, k, vt, o, gcol, grow, hn_g)


def _mix_ffn_kernel(y_ref, x_ref, wo_ref, pg_ref, g1_ref, fg_ref, sh2_ref, sc2_ref, g2_ref, fpg_ref,
                    wg_ref, wu_ref, wd_ref, out_ref, act_ref, *, dff, y_transposed):
    dot = functools.partial(jnp.dot, preferred_element_type=F32)
    if y_transposed:
        t = lax.dot_general(y_ref[0], wo_ref[...], (((0,), (0,)), ((), ())), preferred_element_type=F32)
    else:
        t = dot(y_ref[0], wo_ref[...])
    x1 = x_ref[0] + g1_ref[0] * (_rms(t) * pg_ref[...])
    h2 = (_rms(x1) * fg_ref[...] * (1.0 + sc2_ref[0]) + sh2_ref[0]).astype(BF16)
    cw = FFN_COL_CHUNK
    for j in range(dff // cw):
        g = dot(h2, wg_ref[:, j * cw:(j + 1) * cw])
        u = dot(h2, wu_ref[:, j * cw:(j + 1) * cw])
        act_ref[:, j * cw:(j + 1) * cw] = (g * jax.nn.sigmoid(g) * u).astype(BF16)
    f = dot(act_ref[...], wd_ref[...])
    out_ref[0] = x1 + g2_ref[0] * (_rms(f) * fpg_ref[...])


def _mix_ffn_call(y, x, w_o, post_g, g1, ffn_pre_g, sh2, sc2, g2, ffn_post_g, w_in_all, w_down_all, layer,
                  *, y_transposed=False):
    B, S, D = x.shape
    dff = w_down_all.shape[1]
    tm = min(FFN_ROW_TILE, S)
    row = lambda b, i: (b, i, 0)
    layer_block = lambda shape, k: pl.BlockSpec((None,) + shape, lambda b, i: (layer, 0, k),
                                                pipeline_mode=pl.Buffered(1))
    return pl.pallas_call(
        functools.partial(_mix_ffn_kernel, dff=dff, y_transposed=y_transposed),
        grid=(B, S // tm),
        in_specs=[
            pl.BlockSpec((1, D, tm), lambda b, i: (b, 0, i)) if y_transposed else pl.BlockSpec((1, tm, D), row),
            pl.BlockSpec((1, tm, D), row),
            _resident((D, D)),
            _resident((1, D)),
            _chunk_spec(g1, D),
            _resident((1, D)),
            _chunk_spec(sh2, D), _chunk_spec(sc2, D), _chunk_spec(g2, D),
            _resident((1, D)),
            layer_block((D, dff), 0),
            layer_block((D, dff), 1),
            layer_block((dff, D), 0),
        ],
        out_specs=pl.BlockSpec((1, tm, D), row),
        out_shape=jax.ShapeDtypeStruct((B, S, D), F32),
        scratch_shapes=[pltpu.VMEM((tm, dff), BF16)],
        compiler_params=pltpu.CompilerParams(
            dimension_semantics=("arbitrary", "arbitrary"), vmem_limit_bytes=VMEM_LIMIT),
        name="mix_ffn",
    )(y, x, w_o, post_g, g1[0], ffn_pre_g, sh2[0], sc2[0], g2[0], ffn_post_g, w_in_all, w_in_all, w_down_all)


def _qkv_kernel(x_ref, gq_ref, shq_ref, scq_ref, gkv_ref, shkv_ref, sckv_ref,
                wq_ref, wk_ref, wv_ref, wfc_ref, bfc_ref,
                qt_ref, k_ref, vt_ref, fcol_ref, frow_ref, ccol_ref, *, qscale, hd):
    @pl.when(pl.program_id(1) == 0)
    def _():
        ccol_ref[...] = jnp.zeros_like(ccol_ref)

    dot = functools.partial(jnp.dot, preferred_element_type=F32)
    xn = _rms(x_ref[0])
    hq = (xn * gq_ref[...] * (1.0 + scq_ref[0]) + shq_ref[0]).astype(BF16)
    qt_ref[0] = (lax.dot_general(wq_ref[...], hq, (((0,), (1,)), ((), ())), preferred_element_type=F32)
                 * qscale).astype(BF16)
    hkv = (xn * gkv_ref[...] * (1.0 + sckv_ref[0]) + shkv_ref[0]).astype(BF16)
    k_ref[0] = dot(hkv, wk_ref[...]).astype(BF16)
    vT = lax.dot_general(wv_ref[...], hkv, (((0,), (1,)), ((), ())), preferred_element_type=F32)
    tm = x_ref.shape[1]
    hdx = hd + PV_EXTRA_ROWS
    ones_row = (lax.broadcasted_iota(jnp.int32, (PV_EXTRA_ROWS, tm), 0) == 0).astype(BF16)
    for h in range(vT.shape[0] // hd):
        vt_ref[0, h * hdx:h * hdx + hd, :] = vT[h * hd:(h + 1) * hd].astype(BF16)
        vt_ref[0, h * hdx + hd:(h + 1) * hdx, :] = ones_row
    lower, _ = _tri(tm)
    lf_col = _log_sigmoid(dot(hkv, wfc_ref[...]) + bfc_ref[...])
    f_col = _cumsum_rows(lower.astype(BF16), lf_col) + ccol_ref[0:1, :]
    ccol_ref[...] = jnp.broadcast_to(f_col[tm - 1:tm, :], ccol_ref.shape)
    f2 = f_col * LOG2E
    fcol_ref[0] = f2
    frow_ref[0] = f2.T[0:frow_ref.shape[1]]


def _qkv_call(x, gq, shq, scq, gkv, shkv, sckv, wq, wk, wv, wf_col, bf_col, *, heads):
    B, S, D = x.shape
    tm = min(ROW_TILE, S)
    hd = D // heads
    vt_rows = heads * (hd + PV_EXTRA_ROWS)
    row = lambda b, i: (b, i, 0)
    col = lambda b, i: (b, 0, i)
    return pl.pallas_call(
        functools.partial(_qkv_kernel, qscale=hd ** -0.5 * LOG2E, hd=hd),
        grid=(B, S // tm),
        in_specs=[
            pl.BlockSpec((1, tm, D), row),
            _resident((1, D)), _chunk_spec(shq, D), _chunk_spec(scq, D),
            _resident((1, D)), _chunk_spec(shkv, D), _chunk_spec(sckv, D),
            _resident((D, D)), _resident((D, D)), _resident((D, D)),
            _resident((D, LANES)), _resident((1, LANES)),
        ],
        out_specs=[
            pl.BlockSpec((1, D, tm), col),
            pl.BlockSpec((1, tm, D), row),
            pl.BlockSpec((1, vt_rows, tm), col),
            pl.BlockSpec((1, tm, LANES), row),
            pl.BlockSpec((1, heads, tm), col),
        ],
        out_shape=[
            jax.ShapeDtypeStruct((B, D, S), BF16),
            jax.ShapeDtypeStruct((B, S, D), BF16),
            jax.ShapeDtypeStruct((B, vt_rows, S), BF16),
            jax.ShapeDtypeStruct((B, S, LANES), F32),
            jax.ShapeDtypeStruct((B, heads, S), F32),
        ],
        scratch_shapes=[pltpu.VMEM((8, LANES), F32)],
        compiler_params=pltpu.CompilerParams(
            dimension_semantics=("arbitrary", "arbitrary"), vmem_limit_bytes=VMEM_LIMIT),
        name="fox_qkv",
    )(x, gq, shq[0], scq[0], gkv, shkv[0], sckv[0], wq, wk, wv, wf_col, bf_col)


def _fox_kernel(qt_ref, k_ref, vt_ref, fcol_ref, frow_ref, o_ref, fkb_ref, acc_ref, qm_ref,
                a0_ref, a1_ref, p0_ref, p1_ref, *, hd, tq, tk):
    hp = pl.program_id(1)
    S = k_ref.shape[1]
    nq = S // tq
    hdx = hd + PV_EXTRA_ROWS
    lane = lax.broadcasted_iota(jnp.int32, (1, LANES), 1)
    sub = lax.broadcasted_iota(jnp.int32, (LANES, 1), 0)
    a_ref = (a0_ref, a1_ref)
    p_ref = (p0_ref, p1_ref)

    def spread_fk(c):
        rows = pl.ds(pl.multiple_of(c * tk, tk), tk)
        fc = fcol_ref[0, rows, :]
        for h in range(2):
            col = jnp.sum(jnp.where(lane == 2 * hp + h, fc, 0.0), axis=1, keepdims=True)
            fkb_ref[h, rows, :] = jnp.broadcast_to(col, (tk, LANES))

    def masked_q(i, h):
        qT = qt_ref[0, :, pl.ds(pl.multiple_of(i * tq, tq), tq)].astype(F32)
        return jnp.where((sub < hd) == (h == 0), qT, 0.0).astype(BF16)

    def qk(j, h, qm=None):
        start = pl.multiple_of(j * tk, tk)
        fk = fkb_ref[h, pl.ds(start, tk), :]
        a_ref[h][:, 0:tq] = (jnp.dot(k_ref[0, pl.ds(start, tk), :], qm_ref[h] if qm is None else qm,
                                     preferred_element_type=F32)
                             - jnp.concatenate([fk] * (tq // LANES), axis=1))

    def softmax(h, m_prev, fq, diagonal):
        a = a_ref[h][:, 0:tq]
        if diagonal:
            r = lax.broadcasted_iota(jnp.int32, (tk, tq), 0)
            c = lax.broadcasted_iota(jnp.int32, (tk, tq), 1)
            a = jnp.where(r <= c, a, -jnp.inf)
        m_new = jnp.maximum(m_prev, jnp.max(a, axis=0, keepdims=True) + fq)
        alpha = jnp.exp2(m_prev - m_new)
        p_ref[h][:, 0:tq] = jnp.exp2(a - (m_new - fq)).astype(BF16)
        return m_new, alpha

    def pv(j, h, alpha):
        start = pl.multiple_of(j * tk, tk)
        v2t = vt_ref[0, h * hdx:(h + 1) * hdx, pl.ds(start, tk)]
        acc_ref[h] = alpha * acc_ref[h] + jnp.dot(v2t, p_ref[h][:, 0:tq], preferred_element_type=F32)

    def body(j, carry, fq, diagonal, deferred=True):
        m0, m1, alpha1 = carry
        if deferred:
            pv(j - 1, 1, alpha1)
        m0, alpha0 = softmax(0, m0, fq[0], diagonal)
        qk(j, 1)
        pv(j, 0, alpha0)
        if not diagonal:
            qk(j + 1, 0)
        m1, alpha1 = softmax(1, m1, fq[1], diagonal)
        return m0, m1, alpha1

    def query_block(i, first):
        qstart = pl.multiple_of(i * tq, tq)
        for h in range(2):
            qm_ref[h] = masked_q(i, h)
        fq = [frow_ref[0, 0, h:h + 1, pl.ds(qstart, tq)] for h in range(2)]
        acc_ref[...] = jnp.zeros(acc_ref.shape, F32)
        neg = jnp.full((1, tq), -jnp.inf, F32)
        carry = (neg, neg, jnp.ones((1, tq), F32))
        if first:
            _, _, alpha1 = body(i, carry, fq, True, deferred=False)
        else:
            carry = body(0, carry, fq, False, deferred=False)
            carry = lax.fori_loop(1, i, lambda j, c: body(j, c, fq, False), carry)
            _, _, alpha1 = body(i, carry, fq, True)
        qk(0, 0, masked_q(jnp.minimum(i + 1, nq - 1), 0))
        spread_fk(jnp.minimum(i + 1, nq - 1))
        pv(i, 1, alpha1)
        ot = jnp.concatenate([acc_ref[h, 0:hd, :] / acc_ref[h, hd:hd + 1, :] for h in range(2)], axis=0)
        o_ref[0, :, pl.ds(qstart, tq)] = ot.astype(BF16)

    spread_fk(0)
    qk(0, 0, masked_q(0, 0))
    query_block(jnp.int32(0), True)

    def later_block(i, carry):
        query_block(i, False)
        return carry

    lax.fori_loop(1, nq, later_block, 0)


def _fox_call(qt, k, vt, fcol, frow, *, heads):
    B, S, D = k.shape
    hd = D // heads
    assert 2 * hd == LANES
    t = min(ATTN_TILE, S)
    frow = frow.reshape(B, heads // 2, 2, S)
    return pl.pallas_call(
        functools.partial(_fox_kernel, hd=hd, tq=t, tk=t),
        grid=(B, heads // 2),
        in_specs=[
            pl.BlockSpec((1, LANES, S), lambda b, p: (b, p, 0)),
            pl.BlockSpec((1, S, LANES), lambda b, p: (b, 0, p)),
            pl.BlockSpec((1, 2 * (hd + PV_EXTRA_ROWS), S), lambda b, p: (b, p, 0)),
            pl.BlockSpec((1, S, LANES), lambda b, p: (b, 0, 0)),
            pl.BlockSpec((1, 1, 2, S), lambda b, p: (b, p, 0, 0)),
        ],
        out_specs=pl.BlockSpec((1, LANES, S), lambda b, p: (b, p, 0)),
        out_shape=jax.ShapeDtypeStruct((B, D, S), BF16),
        scratch_shapes=[
            pltpu.VMEM((2, S, LANES), F32),
            pltpu.VMEM((2, hd + PV_EXTRA_ROWS, t), F32),
            pltpu.VMEM((2, LANES, t), BF16),
            pltpu.VMEM((t, t + LANES), F32), pltpu.VMEM((t, t + LANES), F32),
            pltpu.VMEM((t, t + LANES), BF16), pltpu.VMEM((t, t + LANES), BF16),
        ],
        compiler_params=pltpu.CompilerParams(
            dimension_semantics=("arbitrary", "arbitrary"), vmem_limit_bytes=VMEM_LIMIT),
        name="fox_attention",
    )(qt, k, vt, fcol, frow)


def _pad_cols(w, n):
    return jnp.pad(w, ((0, 0), (0, n - w.shape[1])))


def kernel(x, c, mod_w, mod_b, mix_pre_g, mix_post_g, ffn_pre_g, ffn_post_g, ffn_w_in, ffn_w_out,
           a_w_in, a_gate_b, a_hnorm_g, a_w_out, kv_norm_g, kv_mod_w, kv_mod_b, kv_w, kv_fgate_b,
           b_w_q, b_w_out):
    B, S, D = x.shape
    depth = mod_w.shape[0]
    n_a = a_w_in.shape[0]
    m_heads = a_gate_b.shape[1] // 2
    f_heads = kv_fgate_b.shape[0]
    dq = (a_w_in.shape[2] - 2 * D - 2 * m_heads) // 2

    mods = _mod_call(c, mod_w, mod_b)
    kv_mods = _mod_call(c, kv_mod_w[None], kv_mod_b[None])[0]
    vecs = lambda m, n: [(m.reshape(B, 1, n * D), k) for k in range(n)]
    row = lambda g: g.reshape(1, D)
    ffn_in_bf16 = ffn_w_in.astype(BF16)
    ffn_out_bf16 = ffn_w_out.astype(BF16)

    k_sh = v_sh = fcol = frow = None
    for l in range(depth):
        sh1, sc1, g1, sh2, sc2, g2 = vecs(mods[l], 6)
        if l < n_a:
            w = a_w_in[l]
            nmain = 2 * dq + 2 * D
            wg = w[:, nmain:]
            w_ko = jnp.concatenate([w[:, dq:2 * dq], w[:, 2 * dq + D:nmain]], axis=1)
            w_qv = jnp.concatenate([w[:, :dq], w[:, 2 * dq:2 * dq + D]], axis=1)
            qt, k, vt, o, gcol, grow = _inproj_call(
                x, sh1, sc1, row(mix_pre_g[l]),
                w_ko.astype(BF16), w_qv.astype(BF16), _pad_cols(wg, LANES).astype(BF16),
                _pad_cols(a_gate_b[l][None, :], LANES), dq=dq, dm=D, heads=m_heads)
            y = _mlstm_call(qt, k, vt, o, gcol, grow, row(a_hnorm_g[l]), heads=m_heads)
            w_o = a_w_out[l]
        else:
            if l == n_a:
                kv_sh, kv_sc = vecs(kv_mods, 2)
                wf = kv_w[:, 2 * D:]
                shared = dict(
                    gkv=row(kv_norm_g), shkv=kv_sh, sckv=kv_sc,
                    wk=kv_w[:, :D].astype(BF16), wv=kv_w[:, D:2 * D].astype(BF16),
                    wf_col=_pad_cols(wf, LANES).astype(BF16), bf_col=_pad_cols(kv_fgate_b[None, :], LANES))
            qt, k_new, v_new, fcol_new, frow_new = _qkv_call(
                x, row(mix_pre_g[l]), sh1, sc1, shared["gkv"], shared["shkv"], shared["sckv"],
                b_w_q[l - n_a].astype(BF16), shared["wk"], shared["wv"],
                shared["wf_col"], shared["bf_col"], heads=f_heads)
            if l == n_a:
                k_sh, v_sh, fcol, frow = k_new, v_new, fcol_new, frow_new
            y = _fox_call(qt, k_sh, v_sh, fcol, frow, heads=f_heads)
            w_o = b_w_out[l - n_a]
        x = _mix_ffn_call(
            y, x, w_o.astype(BF16), row(mix_post_g[l]), g1, row(ffn_pre_g[l]), sh2, sc2, g2,
            row(ffn_post_g[l]), ffn_in_bf16, ffn_out_bf16, l, y_transposed=l >= n_a)
    return x
```

```python
import functools

import jax
import jax.numpy as jnp
from jax import lax
from jax.experimental import pallas as pl
from jax.experimental.pallas import tpu as pltpu

F32 = jnp.float32
BF16 = jnp.bfloat16

EPS = 1e-6
LOG2E = 1.4426950408889634
GATE_CAP = 15.0
LANES = 128
MLSTM_CHUNK = 256
ROW_TILE = 512
FFN_ROW_TILE = 1024
ATTN_TILE = 512
FFN_COL_CHUNK = 256
PV_EXTRA_ROWS = 16
MLSTM_EXTRA_ROWS = 16
VMEM_LIMIT = 56 * 1024 * 1024


def _rms(x):
    return x * lax.rsqrt(jnp.mean(x * x, axis=-1, keepdims=True) + EPS)


def _log_sigmoid(x):
    return jnp.minimum(x, 0.0) - jnp.log(1.0 + jnp.exp(-jnp.abs(x)))


def _softcap(a):
    return GATE_CAP * jnp.tanh(a / GATE_CAP)


def _split3(x):
    hi = x.astype(BF16)
    r1 = x - hi.astype(F32)
    mid = r1.astype(BF16)
    lo = (r1 - mid.astype(F32)).astype(BF16)
    return hi, mid, lo


def _cumsum_rows(tri_lower, x):
    return sum(jnp.dot(tri_lower, p, preferred_element_type=F32) for p in _split3(x))


def _cumsum_lanes(x, tri_upper):
    return sum(jnp.dot(p, tri_upper, preferred_element_type=F32) for p in _split3(x))


def _tri(n):
    r = lax.broadcasted_iota(jnp.int32, (n, n), 0)
    c = lax.broadcasted_iota(jnp.int32, (n, n), 1)
    return r >= c, (r <= c)


def _chunk_spec(chunk, width):
    k = chunk[1]
    return pl.BlockSpec((1, 1, width), lambda b, i: (b, 0, k))


def _resident(shape):
    nd = len(shape)
    return pl.BlockSpec(shape, lambda *_: (0,) * nd, pipeline_mode=pl.Buffered(1))


def _mod_kernel(c_ref, w_ref, b_ref, o_ref):
    c = c_ref[...]
    cs = (c * jax.nn.sigmoid(c)).astype(BF16)
    o_ref[0] = jnp.dot(cs, w_ref[0].astype(BF16), preferred_element_type=F32) + b_ref[0]


def _mod_call(c, w, b):
    G, D, N = w.shape
    B = c.shape[0]
    tn = 1024
    return pl.pallas_call(
        _mod_kernel,
        grid=(G, N // tn),
        in_specs=[
            pl.BlockSpec((B, D), lambda g, j: (0, 0)),
            pl.BlockSpec((1, D, tn), lambda g, j: (g, 0, j)),
            pl.BlockSpec((1, 1, tn), lambda g, j: (g, 0, j)),
        ],
        out_specs=pl.BlockSpec((1, B, tn), lambda g, j: (g, 0, j)),
        out_shape=jax.ShapeDtypeStruct((G, B, N), F32),
        name="adaln_mod",
    )(c, w, b.reshape(G, 1, N))


def _inproj_kernel(x_ref, sh_ref, sc_ref, g_ref, wko_ref, wqv_ref, wgc_ref, bc_ref,
                   qt_ref, k_ref, vt_ref, o_ref, gcol_ref, grow_ref, *, dq, dm, heads, qscale):
    x = x_ref[0]
    tm = x.shape[0]
    h = (_rms(x) * g_ref[...] * (1.0 + sc_ref[0]) + sh_ref[0]).astype(BF16)
    nt = (((1,), (1,)), ((), ()))
    ko = jnp.dot(h, wko_ref[...], preferred_element_type=F32)
    k_ref[0] = ko[:, 0:dq].astype(BF16)
    o_ref[0] = ko[:, dq:dq + dm].astype(BF16)
    qvt = lax.dot_general(wqv_ref[...], h, (((0,), (1,)), ((), ())), preferred_element_type=F32)
    qt_ref[0] = (qvt[0:dq] * qscale).astype(BF16)
    dv = dm // heads
    dvx = dv + MLSTM_EXTRA_ROWS
    ones_row = (lax.broadcasted_iota(jnp.int32, (MLSTM_EXTRA_ROWS, tm), 0) == 0).astype(BF16)
    for hh in range(heads):
        vt_ref[0, hh * dvx:hh * dvx + dv, :] = qvt[dq + hh * dv:dq + (hh + 1) * dv].astype(BF16)
        vt_ref[0, hh * dvx + dv:(hh + 1) * dvx, :] = ones_row
    gc = _softcap(jnp.dot(h, wgc_ref[...], preferred_element_type=F32) + bc_ref[...])
    lane = lax.broadcasted_iota(jnp.int32, gc.shape, 1)
    g2 = jnp.where(lane < heads, gc, _log_sigmoid(gc)) * LOG2E
    gcol_ref[0] = g2
    grow_ref[0] = g2.T[0:grow_ref.shape[1]]


def _inproj_call(x, sh, sc, g, w_ko, w_qv, wg_col, b_col, *, dq, dm, heads):
    B, S, D = x.shape
    tm = min(ROW_TILE, S)
    dvx_all = dm + heads * MLSTM_EXTRA_ROWS
    row = lambda b, i: (b, i, 0)
    col = lambda b, i: (b, 0, i)
    kern = functools.partial(_inproj_kernel, dq=dq, dm=dm, heads=heads, qscale=(dq // heads) ** -0.5)
    return pl.pallas_call(
        kern,
        grid=(B, S // tm),
        in_specs=[
            pl.BlockSpec((1, tm, D), row),
            _chunk_spec(sh, D),
            _chunk_spec(sc, D),
            _resident((1, D)),
            _resident((D, dq + dm)),
            _resident((D, dq + dm)),
            _resident((D, LANES)),
            _resident((1, LANES)),
        ],
        out_specs=[
            pl.BlockSpec((1, dq, tm), col),
            pl.BlockSpec((1, tm, dq), row),
            pl.BlockSpec((1, dvx_all, tm), col),
            pl.BlockSpec((1, tm, dm), row),
            pl.BlockSpec((1, tm, LANES), row),
            pl.BlockSpec((1, 16, tm), col),
        ],
        out_shape=[
            jax.ShapeDtypeStruct((B, dq, S), BF16),
            jax.ShapeDtypeStruct((B, S, dq), BF16),
            jax.ShapeDtypeStruct((B, dvx_all, S), BF16),
            jax.ShapeDtypeStruct((B, S, dm), BF16),
            jax.ShapeDtypeStruct((B, S, LANES), F32),
            jax.ShapeDtypeStruct((B, 16, S), F32),
        ],
        compiler_params=pltpu.CompilerParams(
            dimension_semantics=("arbitrary", "arbitrary"), vmem_limit_bytes=VMEM_LIMIT),
        name="mlstm_inproj",
    )(x, sh[0], sc[0], g, w_ko, w_qv, wg_col, b_col)


def _mlstm_kernel(qt_ref, k_ref, vt_ref, o_ref, gcol_ref, grow_ref, hn_ref, y_ref,
                  st_ref, m_ref, *, heads, dk, dv, L):
    @pl.when(pl.program_id(1) == 0)
    def _():
        st_ref[...] = jnp.zeros_like(st_ref)
        m_ref[...] = jnp.zeros_like(m_ref)

    gcol = gcol_ref[0]
    grow = grow_ref[0]
    lower, upper = _tri(L)
    bcol = _cumsum_rows(lower.astype(BF16), gcol)
    brow = _cumsum_lanes(grow, upper.astype(BF16))
    dvx = dv + MLSTM_EXTRA_ROWS
    dot = functools.partial(jnp.dot, preferred_element_type=F32)

    hs = range(heads)
    i_row = [grow[h:h + 1, :] for h in hs]
    b_row = [brow[heads + h:heads + h + 1, :] for h in hs]
    i_col = [gcol[:, h:h + 1] for h in hs]
    b_col = [bcol[:, heads + h:heads + h + 1] for h in hs]
    bl = [b[:, L - 1:L] for b in b_row]
    m = [m_ref[h][0:1, 0:1] for h in hs]
    qt = [qt_ref[0, h * dk:(h + 1) * dk, :] for h in hs]
    kh = [k_ref[0, :, h * dk:(h + 1) * dk] for h in hs]
    vt = [vt_ref[0, h * dvx:(h + 1) * dvx, :] for h in hs]
    st = [st_ref[h] for h in hs]

    qk = [dot(kh[h], qt[h]) for h in hs]
    sq = [dot(st[h].astype(BF16), qt[h]) for h in hs]
    m_t, a, w = [], [], []
    for h in hs:
        dmat = jnp.where(upper, b_row[h] + (i_col[h] - b_col[h]), -jnp.inf)
        inter = b_row[h] + m[h]
        m_t.append(jnp.maximum(inter, jnp.max(dmat, axis=0, keepdims=True)))
        a.append(jnp.exp2(inter - m_t[h]))
        w.append(jnp.exp2(dmat - m_t[h]))
    na = [a[h] * sq[h] + dot(vt[h], (qk[h] * w[h]).astype(BF16)) for h in hs]
    for h in hs:
        ws = bl[h] - b_col[h] + i_col[h]
        m_new = jnp.maximum(bl[h] + m[h], jnp.max(ws, axis=0, keepdims=True))
        kw = (kh[h].astype(F32) * jnp.exp2(ws - m_new)).astype(BF16)
        st_ref[h] = jnp.exp2(bl[h] + m[h] - m_new) * st[h] + dot(vt[h], kw)
        m_ref[h] = jnp.broadcast_to(m_new, m_ref.shape[1:])

        hT = na[h][0:dv] * (1.0 / jnp.maximum(jnp.abs(na[h][dv:dv + 1]), jnp.exp2(-m_t[h])))
        hnT = hT * lax.rsqrt(jnp.mean(hT * hT, axis=0, keepdims=True) + EPS)
        og = o_ref[0, :, h * dv:(h + 1) * dv].astype(F32)
        y_ref[0, :, h * dv:(h + 1) * dv] = (jax.nn.sigmoid(og) * (hnT.T * hn_ref[:, h * dv:(h + 1) * dv])).astype(BF16)


def _mlstm_call(qt, k, vt, o, gcol, grow, hn_g, *, heads):
    B, S, dq = k.shape
    dm = o.shape[-1]
    dk, dv = dq // heads, dm // heads
    dvx_all = vt.shape[1]
    L = min(MLSTM_CHUNK, S)
    row = lambda b, c: (b, c, 0)
    col = lambda b, c: (b, 0, c)
    kern = functools.partial(_mlstm_kernel, heads=heads, dk=dk, dv=dv, L=L)
    return pl.pallas_call(
        kern,
        grid=(B, S // L),
        in_specs=[
            pl.BlockSpec((1, dq, L), col),
            pl.BlockSpec((1, L, dq), row),
            pl.BlockSpec((1, dvx_all, L), col),
            pl.BlockSpec((1, L, dm), row),
            pl.BlockSpec((1, L, LANES), row),
            pl.BlockSpec((1, 16, L), col),
            _resident((1, dm)),
        ],
        out_specs=pl.BlockSpec((1, L, dm), row),
        out_shape=jax.ShapeDtypeStruct((B, S, dm), BF16),
        scratch_shapes=[
            pltpu.VMEM((heads, dv + MLSTM_EXTRA_ROWS, dk), F32),
            pltpu.VMEM((heads, 8, LANES), F32),
        ],
        compiler_params=pltpu.CompilerParams(
            dimension_semantics=("arbitrary", "arbitrary"), vmem_limit_bytes=VMEM_LIMIT),
        name="mlstm_recurrence",
    )(qt, k, vt, o, gcol, grow, hn_g)


def _mix_ffn_kernel(y_ref, x_ref, wo_ref, pg_ref, g1_ref, fg_ref, sh2_ref, sc2_ref, g2_ref, fpg_ref,
                    wg_ref, wu_ref, wd_ref, out_ref, act_ref, *, dff, y_transposed):
    dot = functools.partial(jnp.dot, preferred_element_type=F32)
    if y_transposed:
        t = lax.dot_general(y_ref[0], wo_ref[...], (((0,), (0,)), ((), ())), preferred_element_type=F32)
    else:
        t = dot(y_ref[0], wo_ref[...])
    x1 = x_ref[0] + g1_ref[0] * (_rms(t) * pg_ref[...])
    h2 = (_rms(x1) * fg_ref[...] * (1.0 + sc2_ref[0]) + sh2_ref[0]).astype(BF16)
    cw = FFN_COL_CHUNK
    for j in range(dff // cw):
        g = dot(h2, wg_ref[:, j * cw:(j + 1) * cw])
        u = dot(h2, wu_ref[:, j * cw:(j + 1) * cw])
        act_ref[:, j * cw:(j + 1) * cw] = (g * jax.nn.sigmoid(g) * u).astype(BF16)
    f = dot(act_ref[...], wd_ref[...])
    out_ref[0] = x1 + g2_ref[0] * (_rms(f) * fpg_ref[...])


def _mix_ffn_call(y, x, w_o, post_g, g1, ffn_pre_g, sh2, sc2, g2, ffn_post_g, w_in_all, w_down_all, layer,
                  *, y_transposed=False):
    B, S, D = x.shape
    dff = w_down_all.shape[1]
    tm = min(FFN_ROW_TILE, S)
    row = lambda b, i: (b, i, 0)
    layer_block = lambda shape, k: pl.BlockSpec((None,) + shape, lambda b, i: (layer, 0, k),
                                                pipeline_mode=pl.Buffered(1))
    return pl.pallas_call(
        functools.partial(_mix_ffn_kernel, dff=dff, y_transposed=y_transposed),
        grid=(B, S // tm),
        in_specs=[
            pl.BlockSpec((1, D, tm), lambda b, i: (b, 0, i)) if y_transposed else pl.BlockSpec((1, tm, D), row),
            pl.BlockSpec((1, tm, D), row),
            _resident((D, D)),
            _resident((1, D)),
            _chunk_spec(g1, D),
            _resident((1, D)),
            _chunk_spec(sh2, D), _chunk_spec(sc2, D), _chunk_spec(g2, D),
            _resident((1, D)),
            layer_block((D, dff), 0),
            layer_block((D, dff), 1),
            layer_block((dff, D), 0),
        ],
        out_specs=pl.BlockSpec((1, tm, D), row),
        out_shape=jax.ShapeDtypeStruct((B, S, D), F32),
        scratch_shapes=[pltpu.VMEM((tm, dff), BF16)],
        compiler_params=pltpu.CompilerParams(
            dimension_semantics=("arbitrary", "arbitrary"), vmem_limit_bytes=VMEM_LIMIT),
        name="mix_ffn",
    )(y, x, w_o, post_g, g1[0], ffn_pre_g, sh2[0], sc2[0], g2[0], ffn_post_g, w_in_all, w_in_all, w_down_all)


def _qkv_kernel(x_ref, gq_ref, shq_ref, scq_ref, gkv_ref, shkv_ref, sckv_ref,
                wq_ref, wk_ref, wv_ref, wfc_ref, bfc_ref,
                qt_ref, k_ref, vt_ref, fcol_ref, frow_ref, ccol_ref, *, qscale, hd):
    @pl.when(pl.program_id(1) == 0)
    def _():
        ccol_ref[...] = jnp.zeros_like(ccol_ref)

    dot = functools.partial(jnp.dot, preferred_element_type=F32)
    xn = _rms(x_ref[0])
    hq = (xn * gq_ref[...] * (1.0 + scq_ref[0]) + shq_ref[0]).astype(BF16)
    qt_ref[0] = (lax.dot_general(wq_ref[...], hq, (((0,), (1,)), ((), ())), preferred_element_type=F32)
                 * qscale).astype(BF16)
    hkv = (xn * gkv_ref[...] * (1.0 + sckv_ref[0]) + shkv_ref[0]).astype(BF16)
    k_ref[0] = dot(hkv, wk_ref[...]).astype(BF16)
    vT = lax.dot_general(wv_ref[...], hkv, (((0,), (1,)), ((), ())), preferred_element_type=F32)
    tm = x_ref.shape[1]
    hdx = hd + PV_EXTRA_ROWS
    ones_row = (lax.broadcasted_iota(jnp.int32, (PV_EXTRA_ROWS, tm), 0) == 0).astype(BF16)
    for h in range(vT.shape[0] // hd):
        vt_ref[0, h * hdx:h * hdx + hd, :] = vT[h * hd:(h + 1) * hd].astype(BF16)
        vt_ref[0, h * hdx + hd:(h + 1) * hdx, :] = ones_row
    lower, _ = _tri(tm)
    lf_col = _log_sigmoid(dot(hkv, wfc_ref[...]) + bfc_ref[...])
    f_col = _cumsum_rows(lower.astype(BF16), lf_col) + ccol_ref[0:1, :]
    ccol_ref[...] = jnp.broadcast_to(f_col[tm - 1:tm, :], ccol_ref.shape)
    f2 = f_col * LOG2E
    fcol_ref[0] = f2
    frow_ref[0] = f2.T[0:frow_ref.shape[1]]


def _qkv_call(x, gq, shq, scq, gkv, shkv, sckv, wq, wk, wv, wf_col, bf_col, *, heads):
    B, S, D = x.shape
    tm = min(ROW_TILE, S)
    hd = D // heads
    vt_rows = heads * (hd + PV_EXTRA_ROWS)
    row = lambda b, i: (b, i, 0)
    col = lambda b, i: (b, 0, i)
    return pl.pallas_call(
        functools.partial(_qkv_kernel, qscale=hd ** -0.5 * LOG2E, hd=hd),
        grid=(B, S // tm),
        in_specs=[
            pl.BlockSpec((1, tm, D), row),
            _resident((1, D)), _chunk_spec(shq, D), _chunk_spec(scq, D),
            _resident((1, D)), _chunk_spec(shkv, D), _chunk_spec(sckv, D),
            _resident((D, D)), _resident((D, D)), _resident((D, D)),
            _resident((D, LANES)), _resident((1, LANES)),
        ],
        out_specs=[
            pl.BlockSpec((1, D, tm), col),
            pl.BlockSpec((1, tm, D), row),
            pl.BlockSpec((1, vt_rows, tm), col),
            pl.BlockSpec((1, tm, LANES), row),
            pl.BlockSpec((1, heads, tm), col),
        ],
        out_shape=[
            jax.ShapeDtypeStruct((B, D, S), BF16),
            jax.ShapeDtypeStruct((B, S, D), BF16),
            jax.ShapeDtypeStruct((B, vt_rows, S), BF16),
            jax.ShapeDtypeStruct((B, S, LANES), F32),
            jax.ShapeDtypeStruct((B, heads, S), F32),
        ],
        scratch_shapes=[pltpu.VMEM((8, LANES), F32)],
        compiler_params=pltpu.CompilerParams(
            dimension_semantics=("arbitrary", "arbitrary"), vmem_limit_bytes=VMEM_LIMIT),
        name="fox_qkv",
    )(x, gq, shq[0], scq[0], gkv, shkv[0], sckv[0], wq, wk, wv, wf_col, bf_col)


def _fox_kernel(qt_ref, k_ref, vt_ref, fcol_ref, frow_ref, o_ref, fkb_ref, acc_ref, qm_ref,
                a0_ref, a1_ref, p0_ref, p1_ref, *, hd, tq, tk):
    hp = pl.program_id(1)
    S = k_ref.shape[1]
    nq = S // tq
    hdx = hd + PV_EXTRA_ROWS
    lane = lax.broadcasted_iota(jnp.int32, (1, LANES), 1)
    sub = lax.broadcasted_iota(jnp.int32, (LANES, 1), 0)
    a_ref = (a0_ref, a1_ref)
    p_ref = (p0_ref, p1_ref)

    def spread_fk(c):
        rows = pl.ds(pl.multiple_of(c * tk, tk), tk)
        fc = fcol_ref[0, rows, :]
        for h in range(2):
            col = jnp.sum(jnp.where(lane == 2 * hp + h, fc, 0.0), axis=1, keepdims=True)
            fkb_ref[h, rows, :] = jnp.broadcast_to(col, (tk, LANES))

    def masked_q(i, h):
        qT = qt_ref[0, :, pl.ds(pl.multiple_of(i * tq, tq), tq)].astype(F32)
        return jnp.where((sub < hd) == (h == 0), qT, 0.0).astype(BF16)

    def qk(j, h, qm=None):
        start = pl.multiple_of(j * tk, tk)
        fk = fkb_ref[h, pl.ds(start, tk), :]
        a_ref[h][:, 0:tq] = (jnp.dot(k_ref[0, pl.ds(start, tk), :], qm_ref[h] if qm is None else qm,
                                     preferred_element_type=F32)
                             - jnp.concatenate([fk] * (tq // LANES), axis=1))

    def softmax(h, m_prev, fq, diagonal):
        a = a_ref[h][:, 0:tq]
        if diagonal:
            r = lax.broadcasted_iota(jnp.int32, (tk, tq), 0)
            c = lax.broadcasted_iota(jnp.int32, (tk, tq), 1)
            a = jnp.where(r <= c, a, -jnp.inf)
        m_new = jnp.maximum(m_prev, jnp.max(a, axis=0, keepdims=True) + fq)
        alpha = jnp.exp2(m_prev - m_new)
        p_ref[h][:, 0:tq] = jnp.exp2(a - (m_new - fq)).astype(BF16)
        return m_new, alpha

    def pv(j, h, alpha):
        start = pl.multiple_of(j * tk, tk)
        v2t = vt_ref[0, h * hdx:(h + 1) * hdx, pl.ds(start, tk)]
        acc_ref[h] = alpha * acc_ref[h] + jnp.dot(v2t, p_ref[h][:, 0:tq], preferred_element_type=F32)

    def body(j, carry, fq, diagonal, deferred=True):
        m0, m1, alpha1 = carry
        if deferred:
            pv(j - 1, 1, alpha1)
        m0, alpha0 = softmax(0, m0, fq[0], diagonal)
        qk(j, 1)
        pv(j, 0, alpha0)
        if not diagonal:
            qk(j + 1, 0)
        m1, alpha1 = softmax(1, m1, fq[1], diagonal)
        return m0, m1, alpha1

    def query_block(i, first):
        qstart = pl.multiple_of(i * tq, tq)
        for h in range(2):
            qm_ref[h] = masked_q(i, h)
        fq = [frow_ref[0, 0, h:h + 1, pl.ds(qstart, tq)] for h in range(2)]
        acc_ref[...] = jnp.zeros(acc_ref.shape, F32)
        neg = jnp.full((1, tq), -jnp.inf, F32)
        carry = (neg, neg, jnp.ones((1, tq), F32))
        if first:
            _, _, alpha1 = body(i, carry, fq, True, deferred=False)
        else:
            carry = body(0, carry, fq, False, deferred=False)
            carry = lax.fori_loop(1, i, lambda j, c: body(j, c, fq, False), carry)
            _, _, alpha1 = body(i, carry, fq, True)
        qk(0, 0, masked_q(jnp.minimum(i + 1, nq - 1), 0))
        spread_fk(jnp.minimum(i + 1, nq - 1))
        pv(i, 1, alpha1)
        ot = jnp.concatenate([acc_ref[h, 0:hd, :] / acc_ref[h, hd:hd + 1, :] for h in range(2)], axis=0)
        o_ref[0, :, pl.ds(qstart, tq)] = ot.astype(BF16)

    spread_fk(0)
    qk(0, 0, masked_q(0, 0))
    query_block(jnp.int32(0), True)

    def later_block(i, carry):
        query_block(i, False)
        return carry

    lax.fori_loop(1, nq, later_block, 0)


def _fox_call(qt, k, vt, fcol, frow, *, heads):
    B, S, D = k.shape
    hd = D // heads
    assert 2 * hd == LANES
    t = min(ATTN_TILE, S)
    frow = frow.reshape(B, heads // 2, 2, S)
    return pl.pallas_call(
        functools.partial(_fox_kernel, hd=hd, tq=t, tk=t),
        grid=(B, heads // 2),
        in_specs=[
            pl.BlockSpec((1, LANES, S), lambda b, p: (b, p, 0)),
            pl.BlockSpec((1, S, LANES), lambda b, p: (b, 0, p)),
            pl.BlockSpec((1, 2 * (hd + PV_EXTRA_ROWS), S), lambda b, p: (b, p, 0)),
            pl.BlockSpec((1, S, LANES), lambda b, p: (b, 0, 0)),
            pl.BlockSpec((1, 1, 2, S), lambda b, p: (b, p, 0, 0)),
        ],
        out_specs=pl.BlockSpec((1, LANES, S), lambda b, p: (b, p, 0)),
        out_shape=jax.ShapeDtypeStruct((B, D, S), BF16),
        scratch_shapes=[
            pltpu.VMEM((2, S, LANES), F32),
            pltpu.VMEM((2, hd + PV_EXTRA_ROWS, t), F32),
            pltpu.VMEM((2, LANES, t), BF16),
            pltpu.VMEM((t, t + LANES), F32), pltpu.VMEM((t, t + LANES), F32),
            pltpu.VMEM((t, t + LANES), BF16), pltpu.VMEM((t, t + LANES), BF16),
        ],
        compiler_params=pltpu.CompilerParams(
            dimension_semantics=("arbitrary", "arbitrary"), vmem_limit_bytes=VMEM_LIMIT),
        name="fox_attention",
    )(qt, k, vt, fcol, frow)


def _pad_cols(w, n):
    return jnp.pad(w, ((0, 0), (0, n - w.shape[1])))


def kernel(x, c, mod_w, mod_b, mix_pre_g, mix_post_g, ffn_pre_g, ffn_post_g, ffn_w_in, ffn_w_out,
           a_w_in, a_gate_b, a_hnorm_g, a_w_out, kv_norm_g, kv_mod_w, kv_mod_b, kv_w, kv_fgate_b,
           b_w_q, b_w_out):
    B, S, D = x.shape
    depth = mod_w.shape[0]
    n_a = a_w_in.shape[0]
    m_heads = a_gate_b.shape[1] // 2
    f_heads = kv_fgate_b.shape[0]
    dq = (a_w_in.shape[2] - 2 * D - 2 * m_heads) // 2

    mods = _mod_call(c, mod_w, mod_b)
    kv_mods = _mod_call(c, kv_mod_w[None], kv_mod_b[None])[0]
    vecs = lambda m, n: [(m.reshape(B, 1, n * D), k) for k in range(n)]
    row = lambda g: g.reshape(1, D)
    ffn_in_bf16 = ffn_w_in.astype(BF16)
    ffn_out_bf16 = ffn_w_out.astype(BF16)

    k_sh = v_sh = fcol = frow = None
    for l in range(depth):
        sh1, sc1, g1, sh2, sc2, g2 = vecs(mods[l], 6)
        if l < n_a:
            w = a_w_in[l]
            nmain = 2 * dq + 2 * D
            wg = w[:, nmain:]
            w_ko = jnp.concatenate([w[:, dq:2 * dq], w[:, 2 * dq + D:nmain]], axis=1)
            w_qv = jnp.concatenate([w[:, :dq], w[:, 2 * dq:2 * dq + D]], axis=1)
            qt, k, vt, o, gcol, grow = _inproj_call(
                x, sh1, sc1, row(mix_pre_g[l]),
                w_ko.astype(BF16), w_qv.astype(BF16), _pad_cols(wg, LANES).astype(BF16),
                _pad_cols(a_gate_b[l][None, :], LANES), dq=dq, dm=D, heads=m_heads)
            y = _mlstm_call(qt, k, vt, o, gcol, grow, row(a_hnorm_g[l]), heads=m_heads)
            w_o = a_w_out[l]
        else:
            if l == n_a:
                kv_sh, kv_sc = vecs(kv_mods, 2)
                wf = kv_w[:, 2 * D:]
                shared = dict(
                    gkv=row(kv_norm_g), shkv=kv_sh, sckv=kv_sc,
                    wk=kv_w[:, :D].astype(BF16), wv=kv_w[:, D:2 * D].astype(BF16),
                    wf_col=_pad_cols(wf, LANES).astype(BF16), bf_col=_pad_cols(kv_fgate_b[None, :], LANES))
            qt, k_new, v_new, fcol_new, frow_new = _qkv_call(
                x, row(mix_pre_g[l]), sh1, sc1, shared["gkv"], shared["shkv"], shared["sckv"],
                b_w_q[l - n_a].astype(BF16), shared["wk"], shared["wv"],
                shared["wf_col"], shared["bf_col"], heads=f_heads)
            if l == n_a:
                k_sh, v_sh, fcol, frow = k_new, v_new, fcol_new, frow_new
            y = _fox_call(qt, k_sh, v_sh, fcol, frow, heads=f_heads)
            w_o = b_w_out[l - n_a]
        x = _mix_ffn_call(
            y, x, w_o.astype(BF16), row(mix_post_g[l]), g1, row(ffn_pre_g[l]), sh2, sc2, g2,
            row(ffn_post_g[l]), ffn_in_bf16, ffn_out_bf16, l, y_transposed=l >= n_a)
    return x
```
